```python
import math
import jax, jax.numpy as jnp
from jax import lax
import numpy as np

D_MODEL = 2048
BATCH = 2
SEQ = 4096
DEPTH = 1

N_META = 16
GDN_HEADS = 16
GDN_DK = 128
GDN_DV = 128
CONV_K = 4
CHUNK = 64
FOX_HEADS = 16
FOX_DH = 128
Q_BLOCK = 128
D_FF = 4 * D_MODEL
NORM_EPS = 1e-6

GDN_QK = GDN_HEADS * GDN_DK
GDN_V = GDN_HEADS * GDN_DV
GDN_CONV_DIM = 2 * GDN_QK + GDN_V
FOX_W = FOX_HEADS * FOX_DH
IN_SPLITS = (GDN_CONV_DIM, GDN_V, GDN_HEADS, GDN_HEADS, 3 * FOX_W, FOX_HEADS, D_MODEL, D_MODEL)
N_IN = sum(IN_SPLITS)
IN_SPLIT_IDX = tuple(int(i) for i in np.cumsum(IN_SPLITS)[:-1])

kernel_name = "gdn_fox_gated_hybrid_block"


def rms_norm(x, g):
    xf = x.astype(jnp.float32)
    y = xf * lax.rsqrt(jnp.mean(jnp.square(xf), axis=-1, keepdims=True) + NORM_EPS)
    return (y * g.astype(jnp.float32)).astype(x.dtype)


def l2_normalize(x):
    return x * lax.rsqrt(jnp.sum(jnp.square(x), axis=-1, keepdims=True) + NORM_EPS)


def causal_depthwise_conv(x, w):
    L = x.shape[1]
    xp = jnp.pad(x, ((0, 0), (CONV_K - 1, 0), (0, 0)))
    return sum(xp[:, k:k + L] * w[k] for k in range(CONV_K))


def chunked_gated_delta_rule(q, k, v, g, beta):
    B, H, L, DK = q.shape
    DV = v.shape[-1]
    N = L // CHUNK
    rs = lambda t: t.reshape(B, H, N, CHUNK, *t.shape[3:])
    q, k, v, g, beta = rs(q) * DK ** -0.5, rs(k), rs(v), rs(g), rs(beta)
    g = jnp.cumsum(g, axis=-1)
    causal = jnp.tril(jnp.ones((CHUNK, CHUNK), bool))
    strict = jnp.tril(jnp.ones((CHUNK, CHUNK), bool), -1)
    decay = jnp.exp(jnp.where(causal, g[..., :, None] - g[..., None, :], -jnp.inf))
    k_beta = k * beta[..., None]
    a_mat = jnp.where(strict, jnp.einsum('bhncd,bhnsd->bhncs', k_beta, k) * decay, 0.0)
    eye = jnp.eye(CHUNK, dtype=q.dtype)
    rhs = jnp.concatenate([v * beta[..., None], k_beta * jnp.exp(g)[..., None]], axis=-1)
    sol = lax.linalg.triangular_solve(a_mat + eye, rhs, left_side=True, lower=True, unit_diagonal=True)
    w_val, w_key = sol[..., :DV], sol[..., DV:]
    qk_intra = jnp.where(causal, jnp.einsum('bhncd,bhnsd->bhncs', q, k) * decay, 0.0)
    q_dec = q * jnp.exp(g)[..., None]
    k_dec = k * jnp.exp(g[..., -1:] - g)[..., None]
    g_last = jnp.exp(g[..., -1])

    def step(S, inp):
        qk_c, q_dec_c, k_dec_c, w_val_c, w_key_c, g_last_c = inp
        v_new = w_val_c - jnp.einsum('bhck,bhkv->bhcv', w_key_c, S)
        o = jnp.einsum('bhck,bhkv->bhcv', q_dec_c, S) + jnp.einsum('bhcs,bhsv->bhcv', qk_c, v_new)
        S = S * g_last_c[..., None, None] + jnp.einsum('bhck,bhcv->bhkv', k_dec_c, v_new)
        return S, o

    xs = tuple(jnp.moveaxis(t, 2, 0) for t in (qk_intra, q_dec, k_dec, w_val, w_key, g_last))
    S0 = jnp.zeros((B, H, DK, DV), q.dtype)
    _, o = lax.scan(step, S0, xs)
    return jnp.moveaxis(o, 0, 2).reshape(B, H, L, DV)


def gated_deltanet_branch(qkv, z, b, a, conv_w, a_log, dt_bias, norm_g):
    B, L, _ = qkv.shape
    out_dtype = qkv.dtype
    f32 = jnp.float32
    pad = (-L) % CHUNK
    Lp = L + pad
    front = lambda t: jnp.pad(t, ((0, 0), (pad, 0)) + ((0, 0),) * (t.ndim - 2))
    qkv = jax.nn.silu(causal_depthwise_conv(front(qkv.astype(f32)), conv_w.astype(f32)))
    q, k, v = jnp.split(qkv, [GDN_QK, 2 * GDN_QK], axis=-1)
    q = l2_normalize(q.reshape(B, Lp, GDN_HEADS, GDN_DK))
    k = l2_normalize(k.reshape(B, Lp, GDN_HEADS, GDN_DK))
    v = v.reshape(B, Lp, GDN_HEADS, GDN_DV)
    beta = front(jax.nn.sigmoid(b.astype(f32)))
    g = front(-jnp.exp(a_log.astype(f32)) * jax.nn.softplus(a.astype(f32) + dt_bias.astype(f32)))
    bhl = lambda t: jnp.moveaxis(t, 2, 1)
    o = chunked_gated_delta_rule(bhl(q), bhl(k), bhl(v), bhl(g), bhl(beta))
    o = jnp.moveaxis(o, 1, 2)[:, pad:]
    o = rms_norm(o, norm_g) * jax.nn.silu(z.astype(f32).reshape(B, L, GDN_HEADS, GDN_DV))
    return o.reshape(B, L, GDN_V).astype(out_dtype)


def fox_branch(qkv, f_logit, q_norm_g, k_norm_g, f_bias):
    B, L, _ = qkv.shape
    pad = (-L) % Q_BLOCK
    Lp = L + pad
    nblk = Lp // Q_BLOCK
    q, k, v = jnp.split(qkv, 3, axis=-1)
    heads = lambda t: jnp.pad(t, ((0, 0), (0, pad), (0, 0), (0, 0))).transpose(0, 2, 1, 3)
    q = heads(rms_norm(q.reshape(B, L, FOX_HEADS, FOX_DH), q_norm_g))
    k = heads(rms_norm(k.reshape(B, L, FOX_HEADS, FOX_DH), k_norm_g))
    v = heads(v.reshape(B, L, FOX_HEADS, FOX_DH))
    log_f = jax.nn.log_sigmoid(f_logit.astype(jnp.float32) + f_bias.astype(jnp.float32))
    c = jnp.cumsum(log_f, axis=1)
    c = jnp.pad(c, ((0, 0), (0, pad), (0, 0)), mode='edge').transpose(0, 2, 1)
    kpos = jnp.arange(Lp)
    scale = FOX_DH ** -0.5

    def block(i):
        start = i * Q_BLOCK
        qs = lax.dynamic_slice_in_dim(q, start, Q_BLOCK, axis=2)
        cq = lax.dynamic_slice_in_dim(c, start, Q_BLOCK, axis=2)
        s = jnp.einsum('bhqd,bhkd->bhqk', qs, k).astype(jnp.float32) * scale
        s = s + cq[..., :, None] - c[..., None, :]
        qpos = start + jnp.arange(Q_BLOCK)
        s = jnp.where(kpos[None, :] <= qpos[:, None], s, -jnp.inf)
        p = jax.nn.softmax(s, axis=-1).astype(v.dtype)
        return jnp.einsum('bhqk,bhkd->bhqd', p, v)

    o = lax.map(block, jnp.arange(nblk))
    o = jnp.transpose(o, (1, 0, 3, 2, 4)).reshape(B, Lp, FOX_W)
    return o[:, :L]


def hybrid_layer(h, mix_norm_g, w_in, conv_w, a_log, dt_bias, gdn_norm_g, w_o_gdn,
                 fox_q_norm_g, fox_k_norm_g, fox_f_bias, w_o_fox, w_out, mlp_norm_g, w_up, w_down):
    u = rms_norm(h, mix_norm_g)
    p = u @ w_in
    qkv_a, z_a, b_a, a_a, qkv_b, f_b, gate_a, gate_b = jnp.split(p, IN_SPLIT_IDX, axis=-1)
    y_a = gated_deltanet_branch(qkv_a, z_a, b_a, a_a, conv_w, a_log, dt_bias, gdn_norm_g)
    y_b = fox_branch(qkv_b, f_b, fox_q_norm_g, fox_k_norm_g, fox_f_bias)
    mix = jax.nn.sigmoid(gate_a) * (y_a @ w_o_gdn) + jax.nn.sigmoid(gate_b) * (y_b @ w_o_fox)
    h = h + mix @ w_out
    u = rms_norm(h, mlp_norm_g)
    h = h + jnp.square(jax.nn.relu(u @ w_up)) @ w_down
    return h


def setup_inputs(seed: int = 0) -> dict:
    key = jax.random.key(seed)
    ks = jax.random.split(key, 20)
    f32 = jnp.float32
    nrm = lambda k, shape, scale: jax.random.normal(k, shape, f32) * scale
    gain = lambda k, shape: 1.0 + 0.02 * jax.random.normal(k, shape, f32)
    x = nrm(ks[0], (BATCH, SEQ, D_MODEL), 1.0)
    meta_tokens = nrm(ks[1], (N_META, D_MODEL), 1.0)
    mix_norm_g = gain(ks[2], (DEPTH, D_MODEL))
    w_in = nrm(ks[3], (DEPTH, D_MODEL, N_IN), D_MODEL ** -0.5)
    conv_w = nrm(ks[4], (DEPTH, CONV_K, GDN_CONV_DIM), CONV_K ** -0.5)
    a_log = jnp.log(jax.random.uniform(ks[5], (DEPTH, GDN_HEADS), f32, 1.0, 16.0))
    dt = jnp.exp(jax.random.uniform(ks[6], (DEPTH, GDN_HEADS), f32, math.log(1e-3), math.log(1e-1)))
    dt_bias = dt + jnp.log(-jnp.expm1(-dt))
    gdn_norm_g = gain(ks[7], (DEPTH, GDN_DV))
    w_o_gdn = nrm(ks[8], (DEPTH, GDN_V, D_MODEL), GDN_V ** -0.5)
    fox_q_norm_g = gain(ks[9], (DEPTH, FOX_DH))
    fox_k_norm_g = gain(ks[10], (DEPTH, FOX_DH))
    fox_f_bias = jax.random.uniform(ks[11], (DEPTH, FOX_HEADS), f32, 3.0, 6.0)
    w_o_fox = nrm(ks[12], (DEPTH, FOX_W, D_MODEL), FOX_W ** -0.5)
    w_out = nrm(ks[13], (DEPTH, D_MODEL, D_MODEL), D_MODEL ** -0.5)
    mlp_norm_g = gain(ks[14], (DEPTH, D_MODEL))
    w_up = nrm(ks[15], (DEPTH, D_MODEL, D_FF), D_MODEL ** -0.5)
    w_down = nrm(ks[16], (DEPTH, D_FF, D_MODEL), D_FF ** -0.5)
    final_norm_g = gain(ks[17], (D_MODEL,))
    return {"x": x, "meta_tokens": meta_tokens, "mix_norm_g": mix_norm_g, "w_in": w_in,
            "conv_w": conv_w, "a_log": a_log, "dt_bias": dt_bias, "gdn_norm_g": gdn_norm_g,
            "w_o_gdn": w_o_gdn, "fox_q_norm_g": fox_q_norm_g, "fox_k_norm_g": fox_k_norm_g,
            "fox_f_bias": fox_f_bias, "w_o_fox": w_o_fox, "w_out": w_out, "mlp_norm_g": mlp_norm_g,
            "w_up": w_up, "w_down": w_down, "final_norm_g": final_norm_g}


def reference(x, meta_tokens, mix_norm_g, w_in, conv_w, a_log, dt_bias, gdn_norm_g, w_o_gdn,
              fox_q_norm_g, fox_k_norm_g, fox_f_bias, w_o_fox, w_out, mlp_norm_g, w_up, w_down,
              final_norm_g):
    B = x.shape[0]
    meta = jnp.broadcast_to(meta_tokens[None].astype(x.dtype), (B, N_META, D_MODEL))
    h = jnp.concatenate([meta, x], axis=1)
    for l in range(DEPTH):
        h = hybrid_layer(h, mix_norm_g[l], w_in[l], conv_w[l], a_log[l], dt_bias[l], gdn_norm_g[l],
                         w_o_gdn[l], fox_q_norm_g[l], fox_k_norm_g[l], fox_f_bias[l], w_o_fox[l],
                         w_out[l], mlp_norm_g[l], w_up[l], w_down[l])
    return rms_norm(h, final_norm_g)[:, N_META:]
```

```python
import functools

import numpy as np
import jax
import jax.numpy as jnp
from jax import lax
from jax.experimental import pallas as pl
from jax.experimental.pallas import tpu as pltpu

F32 = jnp.float32
BF16 = jnp.bfloat16

NORM_EPS = 1e-6
N_META = 16
HEADS = 16
HEAD_DIM = 128
CONV_K = 4
LANES = 128
CHUNK = 128
FRONT_PAD = CHUNK - N_META
NEG_BIG = -1e30
VMEM_LIMIT = 56 * 1024 * 1024

COL_GDN_Q, COL_GDN_K, COL_GDN_V, COL_Z = 0, 16, 32, 48
COL_FOX_Q, COL_FOX_K, COL_FOX_V = 64, 80, 96
P1_COLS = 112 * LANES
SM_B, SM_A, SM_F = 0, 16, 32


def _params(*sem):
    return pltpu.CompilerParams(dimension_semantics=sem, vmem_limit_bytes=VMEM_LIMIT)


def _norm_rows(x, g):
    ms = jnp.mean(x * x, axis=-1, keepdims=True)
    return x * lax.rsqrt(ms + NORM_EPS) * g


def _split3(x):
    hi = x.astype(BF16)
    r = x - hi.astype(F32)
    mid = r.astype(BF16)
    lo = (r - mid.astype(F32)).astype(BF16)
    return hi, mid, lo


def _dot(a, b):
    return jnp.dot(a, b, preferred_element_type=F32)


def _dot_nt(a, b):
    return lax.dot_general(a, b, (((1,), (1,)), ((), ())), preferred_element_type=F32)


def _dot_tn(a, b):
    return lax.dot_general(a, b, (((0,), (0,)), ((), ())), preferred_element_type=F32)


def _sigmoid(x):
    return 1.0 / (1.0 + jnp.exp(-x))


def _silu(x):
    return x * _sigmoid(x)


def _softplus(x):
    return jnp.maximum(x, 0.0) + jnp.log1p(jnp.exp(-jnp.abs(x)))


def _inproj_kernel(x_ref, g_ref, w_ref, ws_ref, o_ref, os_ref, xn_ref, *, tm):
    @pl.when(pl.program_id(1) == 0)
    def _():
        def body(r, c):
            rows = pl.ds(pl.multiple_of(r * 128, 128), 128)
            xn_ref[rows, :] = _norm_rows(x_ref[rows, :], g_ref[...]).astype(BF16)
            return c
        lax.fori_loop(0, tm // 128, body, 0)
        os_ref[...] = _dot(xn_ref[...], ws_ref[...])

    o_ref[...] = _dot(xn_ref[...], w_ref[...]).astype(o_ref.dtype)


def _inproj(h2d, g, w, ws, *, tm, tn):
    m, d = h2d.shape
    n = w.shape[1]
    return pl.pallas_call(
        functools.partial(_inproj_kernel, tm=tm),
        grid=(m // tm, n // tn),
        in_specs=[
            pl.BlockSpec((tm, d), lambda i, j: (i, 0)),
            pl.BlockSpec((1, d), lambda i, j: (0, 0)),
            pl.BlockSpec((d, tn), lambda i, j: (0, j)),
            pl.BlockSpec((d, LANES), lambda i, j: (0, 0)),
        ],
        out_specs=[
            pl.BlockSpec((tm, tn), lambda i, j: (i, j)),
            pl.BlockSpec((tm, LANES), lambda i, j: (i, 0)),
        ],
        out_shape=[
            jax.ShapeDtypeStruct((m, n), BF16),
            jax.ShapeDtypeStruct((m, LANES), F32),
        ],
        scratch_shapes=[pltpu.VMEM((tm, d), BF16)],
        compiler_params=_params("parallel", "arbitrary"),
        name="inproj",
    )(h2d, g, w, ws)


def _gdn_kernel(q_ref, k_ref, v_ref, z_ref, ps_ref, wq_ref, wk_ref, wv_ref,
                alog_ref, dtb_ref, gn_ref, ltri_ref, o_ref, s_ref, tail_ref, *, hg):
    g_idx = pl.program_id(1)
    c = pl.program_id(2)

    @pl.when(c == 0)
    def _():
        s_ref[...] = jnp.zeros_like(s_ref)
        tail_ref[...] = jnp.zeros_like(tail_ref)

    row = lax.broadcasted_iota(jnp.int32, (CHUNK, LANES), 0)
    col = lax.broadcasted_iota(jnp.int32, (CHUNK, LANES), 1)
    tril = row >= col
    strict = row > col
    valid = (c * CHUNK + row) >= FRONT_PAD

    xs = ps_ref[...]
    beta_all = jnp.where(valid, _sigmoid(xs), 0.0)
    gstep = -jnp.exp(alog_ref[...]) * _softplus(xs + dtb_ref[...])
    gstep = jnp.where(valid, gstep, 0.0)
    gcum = _dot(ltri_ref[...], jnp.concatenate(_split3(gstep), axis=0))
    y = jnp.where(col < SM_A, beta_all, gcum)
    yp = jnp.concatenate(_split3(y), axis=1)

    r256 = lax.broadcasted_iota(jnp.int32, (LANES, 2 * LANES), 0)
    l256 = lax.broadcasted_iota(jnp.int32, (LANES, 2 * LANES), 1)

    for u in range(hg):
        h = g_idx * hg + u
        lo, hi = u * HEAD_DIM, (u + 1) * HEAD_DIM
        sel = r256 == jnp.where(l256 < LANES, SM_A + h, SM_B + h)
        sel = jnp.where(sel, 1.0, 0.0).astype(BF16)
        gb = _dot(yp, jnp.concatenate([sel, sel, sel], axis=0))
        g_col = gb[:, :LANES]
        beta_col = gb[:, LANES:]
        onehot = jnp.where(col == SM_A + h, 1.0, 0.0).astype(BF16)
        g_row = _dot_nt(jnp.concatenate([onehot, onehot, onehot], axis=1), yp)
        g_last = g_col[CHUNK - 1:CHUNK, :]

        def conv_act(x_ref, w_ref, t):
            x = x_ref[:, lo:hi].astype(F32)
            ext = jnp.concatenate([tail_ref[t, :, lo:hi], x], axis=0)
            w = w_ref[:, lo:hi]
            acc = x * w[CONV_K - 1:CONV_K, :]
            for s in range(1, CONV_K):
                acc = acc + ext[8 - s:8 - s + CHUNK, :] * w[CONV_K - 1 - s:CONV_K - s, :]
            tail_ref[t, :, lo:hi] = x[CHUNK - 8:, :]
            return _silu(acc)

        qc = conv_act(q_ref, wq_ref, 0)
        kc = conv_act(k_ref, wk_ref, 1)
        v = conv_act(v_ref, wv_ref, 2)
        q = qc * lax.rsqrt(jnp.sum(qc * qc, axis=-1, keepdims=True) + NORM_EPS) * (HEAD_DIM ** -0.5)
        k = kc * lax.rsqrt(jnp.sum(kc * kc, axis=-1, keepdims=True) + NORM_EPS)

        eg = jnp.exp(g_col)
        decay = jnp.where(tril, jnp.exp(jnp.minimum(g_col - g_row, 0.0)), 0.0)
        kb = k * beta_col
        k16 = k.astype(BF16)
        kq = _dot_nt(jnp.concatenate([kb, q], axis=0).astype(BF16), k16)
        x0 = jnp.where(strict, -(kq[:CHUNK] * decay), 0.0)
        qk_m = kq[CHUNK:] * decay

        x16 = x0.astype(BF16)
        yk = _dot(x16, x16)
        sk = x0
        nlev = int(np.log2(CHUNK)) - 1
        for lev in range(nlev):
            yk16 = yk.astype(BF16)
            if lev < nlev - 1:
                zz = _dot(jnp.concatenate([sk, yk], axis=0).astype(BF16), yk16)
                sk = sk + yk + zz[:CHUNK]
                yk = zz[CHUNK:]
            else:
                sk = sk + yk + _dot(sk.astype(BF16), yk16)

        rhs = jnp.concatenate([v * beta_col, kb * eg], axis=1)
        wsol = rhs + _dot(sk.astype(BF16), rhs.astype(BF16))
        w_val = wsol[:, :HEAD_DIM]
        w_key = wsol[:, HEAD_DIM:]
        q_dec = q * eg
        k_dec = k * jnp.exp(g_last - g_col)

        state = s_ref[u]
        r1 = _dot(jnp.concatenate([w_key, q_dec], axis=0).astype(BF16), state.astype(BF16))
        v_new = (w_val - r1[:CHUNK]).astype(BF16)
        o = r1[CHUNK:] + _dot(qk_m.astype(BF16), v_new)
        s_ref[u] = state * jnp.exp(g_last) + _dot_tn(k_dec.astype(BF16), v_new)

        on = _norm_rows(o, gn_ref[...])
        o_ref[:, lo:hi] = (on * _silu(z_ref[:, lo:hi].astype(F32))).astype(o_ref.dtype)


def _gdn(p1, ps, conv_w, alog_row, dtb_row, gnorm, ltri3, *, seq, hg):
    b, lp, _ = p1.shape
    nc = lp // CHUNK
    gw = hg * HEAD_DIM
    ng = HEADS // hg
    col = lambda base: (lambda bi, gi, ci: (bi, ci, base * LANES // gw + gi))
    wcol = lambda base: (lambda bi, gi, ci: (0, base * LANES // gw + gi))
    const = lambda bi, gi, ci: (0, 0)
    return pl.pallas_call(
        functools.partial(_gdn_kernel, hg=hg),
        grid=(b, ng, nc),
        in_specs=[
            pl.BlockSpec((None, CHUNK, gw), col(COL_GDN_Q)),
            pl.BlockSpec((None, CHUNK, gw), col(COL_GDN_K)),
            pl.BlockSpec((None, CHUNK, gw), col(COL_GDN_V)),
            pl.BlockSpec((None, CHUNK, gw), col(COL_Z)),
            pl.BlockSpec((None, CHUNK, LANES), lambda bi, gi, ci: (bi, ci, 0)),
            pl.BlockSpec((CONV_K, gw), wcol(COL_GDN_Q)),
            pl.BlockSpec((CONV_K, gw), wcol(COL_GDN_K)),
            pl.BlockSpec((CONV_K, gw), wcol(COL_GDN_V)),
            pl.BlockSpec((1, LANES), const),
            pl.BlockSpec((1, LANES), const),
            pl.BlockSpec((1, HEAD_DIM), const),
            pl.BlockSpec((CHUNK, 3 * CHUNK), const),
        ],
        out_specs=pl.BlockSpec((None, CHUNK, gw),
                               lambda bi, gi, ci: (bi, jnp.maximum(ci - 1, 0), gi)),
        out_shape=jax.ShapeDtypeStruct((b, seq, HEADS * HEAD_DIM), BF16),
        scratch_shapes=[
            pltpu.VMEM((hg, HEAD_DIM, HEAD_DIM), F32),
            pltpu.VMEM((3, 8, gw), F32),
        ],
        compiler_params=_params("parallel", "parallel", "arbitrary"),
        name="gdn",
    )(p1, p1, p1, p1, ps, conv_w, conv_w, conv_w, alog_row, dtb_row, gnorm, ltri3)


def _fox_prep_kernel(ps_ref, fb_ref, ltri_ref, sq_ref, sk_ref, cq_ref, ck_ref,
                     eq_ref, ek_ref, *, nblk):
    def body(i, carry):
        rows = pl.ds(pl.multiple_of(i * 128, 128), 128)
        x = ps_ref[rows, :] + fb_ref[...]
        ls = jnp.minimum(x, 0.0) - jnp.log1p(jnp.exp(-jnp.abs(x)))
        cum = _dot(ltri_ref[...], jnp.concatenate(_split3(ls), axis=0)) + carry
        cp = jnp.concatenate(_split3(cum), axis=1)
        eq_ref[rows, :] = (_dot(cp, sq_ref[...]) + cq_ref[...]).astype(BF16)
        ek_ref[rows, :] = (_dot(cp, sk_ref[...]) + ck_ref[...]).astype(BF16)
        return cum[127:128, :]

    lax.fori_loop(0, nblk, body, jnp.zeros((1, LANES), F32))


def _fox_prep(ps, fb_row, ltri3, sq, sk, cq, ck):
    b, lp, _ = ps.shape
    const = lambda bi: (0, 0)
    return pl.pallas_call(
        functools.partial(_fox_prep_kernel, nblk=lp // 128),
        grid=(b,),
        in_specs=[
            pl.BlockSpec((None, lp, LANES), lambda bi: (bi, 0, 0)),
            pl.BlockSpec((1, LANES), const),
            pl.BlockSpec((128, 384), const),
            pl.BlockSpec((384, LANES), const),
            pl.BlockSpec((384, LANES), const),
            pl.BlockSpec((1, LANES), const),
            pl.BlockSpec((1, LANES), const),
        ],
        out_specs=[
            pl.BlockSpec((None, lp, LANES), lambda bi: (bi, 0, 0)),
            pl.BlockSpec((None, lp, LANES), lambda bi: (bi, 0, 0)),
        ],
        out_shape=[jax.ShapeDtypeStruct((b, lp, LANES), BF16)] * 2,
        compiler_params=_params("parallel"),
        name="fox_prep",
    )(ps, fb_row, ltri3, sq, sk, cq, ck)


def _fox_kernel(q_ref, k_ref, v_ref, eq_ref, ek_ref, gq_ref, gk_ref, o_ref, kaug_ref,
                *, tq, lp):
    h = pl.program_id(1)
    i = pl.program_id(2)

    @pl.when(i == 0)
    def _():
        def body(r, c):
            rows = pl.ds(pl.multiple_of(r * 128, 128), 128)
            kn = _norm_rows(k_ref[rows, :].astype(F32), gk_ref[...])
            kaug_ref[rows, 0:HEAD_DIM] = kn.astype(BF16)
            kaug_ref[rows, HEAD_DIM:2 * HEAD_DIM] = ek_ref[rows, :]
            return c
        lax.fori_loop(0, lp // 128, body, 0)

    q0 = pl.multiple_of(CHUNK + i * tq, 128)
    rows_q = pl.ds(q0, tq)
    qn = _norm_rows(q_ref[rows_q, :].astype(F32), gq_ref[...]) * (HEAD_DIM ** -0.5)
    lane = lax.broadcasted_iota(jnp.int32, (tq, LANES), 1)
    mine = lax.shift_right_logical(lane, 3) == h
    eqm = jnp.where(mine, eq_ref[rows_q, :], jnp.zeros((), BF16))
    qaug = jnp.concatenate([qn.astype(BF16), eqm], axis=1)

    s = _dot_nt(qaug, kaug_ref[0:CHUNK, :])
    s = jnp.where(lane >= FRONT_PAD, s, NEG_BIG)
    m = jnp.max(s, axis=-1, keepdims=True)
    p = jnp.exp(s - m)
    l = jnp.sum(p, axis=-1, keepdims=True)
    acc = _dot(p.astype(BF16), v_ref[0:CHUNK, :])

    def update(s, vrows, m, l, acc):
        m_new = jnp.maximum(m, jnp.max(s, axis=-1, keepdims=True))
        alpha = jnp.exp(m - m_new)
        p = jnp.exp(s - m_new)
        l = alpha * l + jnp.sum(p, axis=-1, keepdims=True)
        acc = alpha * acc + _dot(p.astype(BF16), vrows)
        return m_new, l, acc

    def body(j, carry):
        ks = pl.ds(pl.multiple_of(CHUNK + j * tq, 128), tq)
        s = _dot_nt(qaug, kaug_ref[ks, :])
        return update(s, v_ref[ks, :], *carry)

    m, l, acc = lax.fori_loop(0, i, body, (m, l, acc))

    s = _dot_nt(qaug, kaug_ref[rows_q, :])
    rr = lax.broadcasted_iota(jnp.int32, (tq, tq), 0)
    cc = lax.broadcasted_iota(jnp.int32, (tq, tq), 1)
    s = jnp.where(rr >= cc, s, NEG_BIG)
    m, l, acc = update(s, v_ref[rows_q, :], m, l, acc)
    o_ref[...] = (acc / l).astype(o_ref.dtype)


def _fox(p1, eq, ek, gq, gk, *, seq, tq):
    b, lp, _ = p1.shape
    colmap = lambda base: (lambda bi, hi, qi: (bi, 0, base + hi))
    const = lambda bi, hi, qi: (0, 0)
    return pl.pallas_call(
        functools.partial(_fox_kernel, tq=tq, lp=lp),
        grid=(b, HEADS, seq // tq),
        in_specs=[
            pl.BlockSpec((None, lp, HEAD_DIM), colmap(COL_FOX_Q)),
            pl.BlockSpec((None, lp, HEAD_DIM), colmap(COL_FOX_K)),
            pl.BlockSpec((None, lp, HEAD_DIM), colmap(COL_FOX_V)),
            pl.BlockSpec((None, lp, LANES), lambda bi, hi, qi: (bi, 0, 0)),
            pl.BlockSpec((None, lp, LANES), lambda bi, hi, qi: (bi, 0, 0)),
            pl.BlockSpec((1, HEAD_DIM), const),
            pl.BlockSpec((1, HEAD_DIM), const),
        ],
        out_specs=pl.BlockSpec((None, tq, HEAD_DIM), lambda bi, hi, qi: (bi, qi, hi)),
        out_shape=jax.ShapeDtypeStruct((b, seq, HEADS * HEAD_DIM), BF16),
        scratch_shapes=[pltpu.VMEM((lp, 2 * HEAD_DIM), BF16)],
        compiler_params=_params("parallel", "parallel", "arbitrary"),
        name="fox",
    )(p1, p1, p1, eq, ek, gq, gk)


def _merge_kernel(x_ref, g_ref, ya_ref, yb_ref, wga_ref, wgb_ref, wa_ref, wb_ref, wo_ref,
                  o_ref, u_ref):
    j = pl.program_id(1)

    @pl.when(j == 0)
    def _():
        x = x_ref[...]
        u_ref[...] = _norm_rows(x, g_ref[...]).astype(BF16)
        o_ref[...] = x

    u = u_ref[...]
    ga = _dot(u, wga_ref[...])
    gb = _dot(u, wgb_ref[...])
    pa = _dot(ya_ref[...], wa_ref[...])
    pb = _dot(yb_ref[...], wb_ref[...])
    mix = _sigmoid(ga) * pa + _sigmoid(gb) * pb
    o_ref[...] += _dot(mix.astype(BF16), wo_ref[...])


def _merge(x2d, g, ya, yb, wga, wgb, wa, wb, wo, *, tm, tj):
    m, d = x2d.shape
    rowblk = lambda i, j: (i, 0)
    colblk = lambda i, j: (0, j)
    return pl.pallas_call(
        _merge_kernel,
        grid=(m // tm, d // tj),
        in_specs=[
            pl.BlockSpec((tm, d), rowblk),
            pl.BlockSpec((1, d), lambda i, j: (0, 0)),
            pl.BlockSpec((tm, d), rowblk),
            pl.BlockSpec((tm, d), rowblk),
            pl.BlockSpec((d, tj), colblk),
            pl.BlockSpec((d, tj), colblk),
            pl.BlockSpec((d, tj), colblk),
            pl.BlockSpec((d, tj), colblk),
            pl.BlockSpec((tj, d), lambda i, j: (j, 0)),
        ],
        out_specs=pl.BlockSpec((tm, d), rowblk),
        out_shape=jax.ShapeDtypeStruct((m, d), F32),
        scratch_shapes=[pltpu.VMEM((tm, d), BF16)],
        compiler_params=_params("parallel", "arbitrary"),
        name="merge",
    )(x2d, g, ya, yb, wga, wgb, wa, wb, wo)


def _mlp_kernel(h_ref, g_ref, wu_ref, wd_ref, gf_ref, o_ref, u_ref):
    f = pl.program_id(1)

    @pl.when(f == 0)
    def _():
        x = h_ref[...]
        u_ref[...] = _norm_rows(x, g_ref[...]).astype(BF16)
        o_ref[...] = x

    a = jnp.maximum(_dot(u_ref[...], wu_ref[...]), 0.0)
    o_ref[...] += _dot((a * a).astype(BF16), wd_ref[...])

    @pl.when(f == pl.num_programs(1) - 1)
    def _():
        o_ref[...] = _norm_rows(o_ref[...], gf_ref[...])


def _mlp(h2d, g, wu, wd, gf, *, tm, tf):
    m, d = h2d.shape
    dff = wu.shape[1]
    rowblk = lambda i, f: (i, 0)
    return pl.pallas_call(
        _mlp_kernel,
        grid=(m // tm, dff // tf),
        in_specs=[
            pl.BlockSpec((tm, d), rowblk),
            pl.BlockSpec((1, d), lambda i, f: (0, 0)),
            pl.BlockSpec((d, tf), lambda i, f: (0, f)),
            pl.BlockSpec((tf, d), lambda i, f: (f, 0)),
            pl.BlockSpec((1, d), lambda i, f: (0, 0)),
        ],
        out_specs=pl.BlockSpec((tm, d), rowblk),
        out_shape=jax.ShapeDtypeStruct((m, d), F32),
        scratch_shapes=[pltpu.VMEM((tm, d), BF16)],
        compiler_params=_params("parallel", "arbitrary"),
        name="mlp",
    )(h2d, g, wu, wd, gf)


def _constants():
    r = np.arange(128)
    ltri = (r[:, None] >= r[None, :]).astype(np.float32)
    ltri3 = np.concatenate([ltri, ltri, ltri], axis=1)
    sq = np.zeros((384, LANES), np.float32)
    sk = np.zeros((384, LANES), np.float32)
    cq = np.zeros((1, LANES), np.float32)
    ck = np.zeros((1, LANES), np.float32)
    for h in range(HEADS):
        for p in range(3):
            sq[p * 128 + SM_F + h, 8 * h + p] = 1.0
            sk[p * 128 + SM_F + h, 8 * h + 3 + p] = -1.0
            cq[0, 8 * h + 3 + p] = 1.0
            ck[0, 8 * h + p] = 1.0
    as_bf = lambda a: jnp.asarray(a, BF16)
    return as_bf(ltri3), as_bf(sq), as_bf(sk), jnp.asarray(cq), jnp.asarray(ck)


def _row128(vec, offset):
    return jnp.zeros((1, LANES), F32).at[0, offset:offset + HEADS].set(vec.astype(F32))


def _pick_rows_tile(total, target):
    best = 128
    for t in range(128, target + 1, 128):
        if total % t == 0:
            best = t
    return best


def _layer(x, meta_tokens, mix_norm_g, w_in, conv_w, a_log, dt_bias, gdn_norm_g, w_o_gdn,
           fox_q_norm_g, fox_k_norm_g, fox_f_bias, w_o_fox, w_out, mlp_norm_g, w_up, w_down,
           final_norm_g):
    b, seq, d = x.shape
    lp = CHUNK + seq
    qk = HEADS * HEAD_DIM

    o_z = 3 * qk
    o_b = o_z + qk
    o_a = o_b + HEADS
    o_fq = o_a + HEADS
    o_f = o_fq + 3 * qk
    o_ga = o_f + HEADS
    o_gb = o_ga + d

    w1 = jnp.concatenate([w_in[:, :o_b], w_in[:, o_fq:o_f]], axis=1).astype(BF16)
    ws = jnp.concatenate(
        [w_in[:, o_b:o_fq], w_in[:, o_f:o_ga],
         jnp.zeros((d, LANES - 3 * HEADS), w_in.dtype)], axis=1).astype(BF16)
    wga = w_in[:, o_ga:o_gb].astype(BF16)
    wgb = w_in[:, o_gb:].astype(BF16)

    meta = jnp.broadcast_to(meta_tokens[None].astype(x.dtype), (b, N_META, d))
    h_pad = jnp.concatenate([jnp.zeros((b, FRONT_PAD, d), x.dtype), meta, x], axis=1)

    row = lambda v: v.reshape(1, -1).astype(F32)
    ltri3, sq, sk, cq, ck = _constants()

    tm_in = _pick_rows_tile(lp, 1408)
    p1, ps = _inproj(h_pad.reshape(b * lp, d), row(mix_norm_g), w1, ws, tm=tm_in, tn=512)
    p1 = p1.reshape(b, lp, P1_COLS)
    ps = ps.reshape(b, lp, LANES)

    ya = _gdn(p1, ps, conv_w.astype(F32), _row128(a_log, SM_A), _row128(dt_bias, SM_A),
              row(gdn_norm_g), ltri3, seq=seq, hg=4)

    eq, ek = _fox_prep(ps, _row128(fox_f_bias, SM_F), ltri3, sq, sk, cq, ck)
    yb = _fox(p1, eq, ek, row(fox_q_norm_g), row(fox_k_norm_g), seq=seq, tq=512)

    x2d = x.reshape(b * seq, d)
    h1 = _merge(x2d, row(mix_norm_g), ya.reshape(b * seq, qk), yb.reshape(b * seq, qk),
                wga, wgb, w_o_gdn.astype(BF16), w_o_fox.astype(BF16), w_out.astype(BF16),
                tm=512, tj=256)
    out = _mlp(h1, row(mlp_norm_g), w_up.astype(BF16), w_down.astype(BF16),
               row(final_norm_g), tm=512, tf=512)
    return out.reshape(b, seq, d)


def kernel(x, meta_tokens, mix_norm_g, w_in, conv_w, a_log, dt_bias, gdn_norm_g, w_o_gdn,
           fox_q_norm_g, fox_k_norm_g, fox_f_bias, w_o_fox, w_out, mlp_norm_g, w_up, w_down,
           final_norm_g):
    assert w_in.shape[0] == 1, "single-layer block"
    return _layer(x, meta_tokens, mix_norm_g[0], w_in[0], conv_w[0], a_log[0], dt_bias[0],
                  gdn_norm_g[0], w_o_gdn[0], fox_q_norm_g[0], fox_k_norm_g[0], fox_f_bias[0],
                  w_o_fox[0], w_out[0], mlp_norm_g[0], w_up[0], w_down[0], final_norm_g)
```

```python
import functools
import math

import numpy as np
import jax
import jax.numpy as jnp
from jax import lax
from jax.experimental import pallas as pl
from jax.experimental.pallas import tpu as pltpu

F32 = jnp.float32
BF16 = jnp.bfloat16

NORM_EPS = 1e-6
N_META = 16
HEADS = 16
HEAD_DIM = 128
CONV_K = 4
LANES = 128
CHUNK = 128
FRONT_PAD = CHUNK - N_META
NEG_BIG = -1e30
LOG2E = math.log2(math.e)
VMEM_LIMIT = 56 * 1024 * 1024

COL_GDN_Q, COL_GDN_K, COL_GDN_V, COL_Z = 0, 16, 32, 48
COL_FOX_Q, COL_FOX_K, COL_FOX_V = 64, 80, 96
P1_COLS = 112 * LANES
SM_B, SM_A, SM_F = 0, 16, 32
GQ_G, GQ_BETA, GQ_EG, GQ_EKD, GQ_BEG = range(5)
N_GQ = 5
FOX_EXTRA = 8
FOX_M_LANE = 6
FOX_SAFE_BOUND = 40.0


def _params(*sem):
    return pltpu.CompilerParams(dimension_semantics=sem, vmem_limit_bytes=VMEM_LIMIT)


def _norm_rows(x, g):
    ms = jnp.mean(x * x, axis=-1, keepdims=True)
    return x * lax.rsqrt(ms + NORM_EPS) * g


def _split3(x):
    hi = x.astype(BF16)
    r = x - hi.astype(F32)
    mid = r.astype(BF16)
    lo = (r - mid.astype(F32)).astype(BF16)
    return hi, mid, lo


def _dot(a, b):
    return jnp.dot(a, b, preferred_element_type=F32)


def _dot_nt(a, b):
    return lax.dot_general(a, b, (((1,), (1,)), ((), ())), preferred_element_type=F32)


def _dot_tn(a, b):
    return lax.dot_general(a, b, (((0,), (0,)), ((), ())), preferred_element_type=F32)


def _sigmoid(x):
    return 1.0 / (1.0 + jnp.exp(-x))


def _silu(x):
    return x * _sigmoid(x)


def _softplus(x):
    return jnp.maximum(x, 0.0) + jnp.log1p(jnp.exp(-jnp.abs(x)))


def _inproj_kernel(x_ref, g_ref, w_ref, ws_ref, o_ref, os_ref, xn_ref, *, tm):
    @pl.when(pl.program_id(1) == 0)
    def _():
        def body(r, c):
            rows = pl.ds(pl.multiple_of(r * 128, 128), 128)
            xn_ref[rows, :] = _norm_rows(x_ref[rows, :], g_ref[...]).astype(BF16)
            return c
        lax.fori_loop(0, tm // 128, body, 0)
        os_ref[...] = _dot(xn_ref[...], ws_ref[...])

    o_ref[...] = _dot(xn_ref[...], w_ref[...]).astype(o_ref.dtype)


def _inproj(h2d, g, w, ws, *, tm, tn):
    m, d = h2d.shape
    n = w.shape[1]
    return pl.pallas_call(
        functools.partial(_inproj_kernel, tm=tm),
        grid=(m // tm, n // tn),
        in_specs=[
            pl.BlockSpec((tm, d), lambda i, j: (i, 0)),
            pl.BlockSpec((1, d), lambda i, j: (0, 0)),
            pl.BlockSpec((d, tn), lambda i, j: (0, j)),
            pl.BlockSpec((d, LANES), lambda i, j: (0, 0)),
        ],
        out_specs=[
            pl.BlockSpec((tm, tn), lambda i, j: (i, j)),
            pl.BlockSpec((tm, LANES), lambda i, j: (i, 0)),
        ],
        out_shape=[
            jax.ShapeDtypeStruct((m, n), BF16),
            jax.ShapeDtypeStruct((m, LANES), F32),
        ],
        scratch_shapes=[pltpu.VMEM((tm, d), BF16)],
        compiler_params=_params("parallel", "arbitrary"),
        name="inproj",
    )(h2d, g, w, ws)


def _gate_prep_kernel(ps_ref, alog_ref, dtb_ref, fb_ref, ltri_ref, sq_ref, sk_ref, cq_ref, ck_ref,
                      yp_ref, eq_ref, ek_ref, *, nblk):
    lane = lax.broadcasted_iota(jnp.int32, (CHUNK, LANES), 1)
    rowi = lax.broadcasted_iota(jnp.int32, (CHUNK, LANES), 0)

    def body(i, carry):
        rows = pl.ds(pl.multiple_of(i * CHUNK, CHUNK), CHUNK)
        x = ps_ref[rows, :]
        valid = (i * CHUNK + rowi) >= FRONT_PAD

        beta = jnp.where(valid, _sigmoid(x), 0.0)
        gstep = jnp.where(valid, -jnp.exp(alog_ref[...]) * _softplus(x + dtb_ref[...]), 0.0)
        gcum = _dot(ltri_ref[...], jnp.concatenate(_split3(gstep), axis=0))
        eg = jnp.exp(gcum)
        ekd = jnp.exp(gcum[CHUNK - 1:CHUNK, :] - gcum)
        beg = pltpu.roll(beta, SM_A - SM_B, 1) * eg
        y = jnp.where(lane < 16 * (GQ_EKD + 1), pltpu.roll(ekd, 16 * GQ_EKD - SM_A, 1),
                      pltpu.roll(beg, 16 * GQ_BEG - SM_A, 1))
        y = jnp.where(lane < 16 * (GQ_EG + 1), pltpu.roll(eg, 16 * GQ_EG - SM_A, 1), y)
        y = jnp.where(lane < 16 * (GQ_BETA + 1), pltpu.roll(beta, 16 * GQ_BETA - SM_B, 1), y)
        y = jnp.where(lane < 16 * (GQ_G + 1), pltpu.roll(gcum, LANES + 16 * GQ_G - SM_A, 1), y)
        yp_ref[rows, :] = jnp.concatenate(_split3(y), axis=1)

        xf = x + fb_ref[...]
        ls = (jnp.minimum(xf, 0.0) - jnp.log1p(jnp.exp(-jnp.abs(xf)))) * LOG2E
        cum = _dot(ltri_ref[...], jnp.concatenate(_split3(ls), axis=0)) + carry
        cp = jnp.concatenate(_split3(cum), axis=1)
        eq_ref[rows, :] = (_dot(cp, sq_ref[...]) + cq_ref[...]).astype(BF16)
        ek_ref[rows, :] = (_dot(cp, sk_ref[...]) + ck_ref[...]).astype(BF16)
        return cum[CHUNK - 1:CHUNK, :]

    lax.fori_loop(0, nblk, body, jnp.zeros((1, LANES), F32))


def _gate_prep(ps, alog_row, dtb_row, fb_row, ltri3, sq, sk, cq, ck):
    b, lp, _ = ps.shape
    const = lambda bi: (0, 0)
    return pl.pallas_call(
        functools.partial(_gate_prep_kernel, nblk=lp // CHUNK),
        grid=(b,),
        in_specs=[
            pl.BlockSpec((None, lp, LANES), lambda bi: (bi, 0, 0)),
            pl.BlockSpec((1, LANES), const),
            pl.BlockSpec((1, LANES), const),
            pl.BlockSpec((1, LANES), const),
            pl.BlockSpec((CHUNK, 3 * CHUNK), const),
            pl.BlockSpec((3 * LANES, LANES), const),
            pl.BlockSpec((3 * LANES, LANES), const),
            pl.BlockSpec((1, LANES), const),
            pl.BlockSpec((1, LANES), const),
        ],
        out_specs=[
            pl.BlockSpec((None, lp, 3 * LANES), lambda bi: (bi, 0, 0)),
            pl.BlockSpec((None, lp, LANES), lambda bi: (bi, 0, 0)),
            pl.BlockSpec((None, lp, LANES), lambda bi: (bi, 0, 0)),
        ],
        out_shape=[
            jax.ShapeDtypeStruct((b, lp, 3 * LANES), BF16),
            jax.ShapeDtypeStruct((b, lp, LANES), BF16),
            jax.ShapeDtypeStruct((b, lp, LANES), BF16),
        ],
        compiler_params=_params("parallel"),
        name="gate_prep",
    )(ps, alog_row, dtb_row, fb_row, ltri3, sq, sk, cq, ck)


def _gdn_kernel(q_ref, k_ref, v_ref, z_ref, yp_ref, sel_ref, oneh_ref, wq_ref, wk_ref, wv_ref,
                gn_ref, shift_ref, o_ref, s_ref, prev_ref, *, hg):
    c = pl.program_id(2)

    @pl.when(c == 0)
    def _():
        s_ref[...] = jnp.zeros_like(s_ref)
        prev_ref[...] = jnp.zeros_like(prev_ref)

    row = lax.broadcasted_iota(jnp.int32, (CHUNK, CHUNK), 0)
    col = lax.broadcasted_iota(jnp.int32, (CHUNK, CHUNK), 1)
    tril = row >= col
    strict = row > col
    heads = range(hg)
    hsl = lambda a, u: a[:, u * HEAD_DIM:(u + 1) * HEAD_DIM]

    yp = yp_ref[...]
    gb = _dot(yp, sel_ref[...])
    g_rows = _dot_nt(oneh_ref[...], yp)
    gq = lambda u, q: gb[:, (u * N_GQ + q) * LANES:(u * N_GQ + q + 1) * LANES]

    def conv_act(x_ref, w_ref, t):
        x16 = x_ref[...]
        x2 = jnp.concatenate([prev_ref[t], x16], axis=0)
        prev_ref[t] = x16
        sh = _dot(shift_ref[...], x2)
        w = w_ref[...]
        acc = x16.astype(F32) * w[CONV_K - 1:CONV_K, :]
        for s in range(1, CONV_K):
            acc = acc + sh[(s - 1) * CHUNK:s * CHUNK, :] * w[CONV_K - 1 - s:CONV_K - s, :]
        return _silu(acc)

    qa = conv_act(q_ref, wq_ref, 0)
    ka = conv_act(k_ref, wk_ref, 1)
    va = conv_act(v_ref, wv_ref, 2)

    def l2n(x):
        return x * lax.rsqrt(jnp.sum(x * x, axis=-1, keepdims=True) + NORM_EPS)

    qn = [l2n(hsl(qa, u)) * (HEAD_DIM ** -0.5) for u in heads]
    kn = [l2n(hsl(ka, u)) for u in heads]
    k16 = [kn[u].astype(BF16) for u in heads]

    state = [s_ref[u] for u in heads]
    r1 = [_dot(jnp.concatenate([kn[u] * gq(u, GQ_BEG), qn[u] * gq(u, GQ_EG)], axis=0).astype(BF16),
               state[u].astype(BF16)) for u in heads]

    kq = [_dot_nt(jnp.concatenate([kn[u] * gq(u, GQ_BETA), qn[u]], axis=0).astype(BF16), k16[u])
          for u in heads]
    ediff = [jnp.exp(jnp.minimum(gq(u, GQ_G) - g_rows[u * CHUNK:(u + 1) * CHUNK, :], 0.0))
             for u in heads]
    a = [kq[u][:CHUNK] * jnp.where(strict, ediff[u], 0.0) for u in heads]
    qk_m = [(kq[u][CHUNK:] * jnp.where(tril, ediff[u], 0.0)).astype(BF16) for u in heads]

    a16 = [a[u].astype(BF16) for u in heads]
    yk = [_dot(a16[u], a16[u]) for u in heads]
    sk = [-a[u] for u in heads]
    nlev = int(np.log2(CHUNK)) - 1
    for lev in range(nlev):
        yk16 = [yk[u].astype(BF16) for u in heads]
        if lev < nlev - 1:
            zz = [_dot(jnp.concatenate([sk[u], yk[u]], axis=0).astype(BF16), yk16[u]) for u in heads]
            sk = [sk[u] + yk[u] + zz[u][:CHUNK] for u in heads]
            yk = [zz[u][CHUNK:] for u in heads]
        else:
            sk = [sk[u] + yk[u] + _dot(sk[u].astype(BF16), yk16[u]) for u in heads]

    rhs = [hsl(va, u) * gq(u, GQ_BETA) - r1[u][:CHUNK] for u in heads]
    v_new = [(rhs[u] + _dot(sk[u].astype(BF16), rhs[u].astype(BF16))).astype(BF16) for u in heads]
    o = [r1[u][CHUNK:] + _dot(qk_m[u], v_new[u]) for u in heads]
    for u in heads:
        eg_last = gq(u, GQ_EG)[CHUNK - 1:CHUNK, :]
        s_ref[u] = state[u] * eg_last + _dot_tn((kn[u] * gq(u, GQ_EKD)).astype(BF16), v_new[u])

    on = jnp.concatenate([_norm_rows(o[u], gn_ref[...]) for u in heads], axis=1)
    o_ref[...] = (on * _silu(z_ref[...].astype(F32))).astype(o_ref.dtype)


def _gdn(p1, yp, sel, oneh, conv_w, gnorm, shift3, *, seq, hg):
    b, lp, _ = p1.shape
    nc = lp // CHUNK
    gw = hg * HEAD_DIM
    ng = HEADS // hg
    col = lambda base: (lambda bi, gi, ci: (bi, ci, base * LANES // gw + gi))
    wcol = lambda base: (lambda bi, gi, ci: (0, base * LANES // gw + gi))
    const = lambda bi, gi, ci: (0, 0)
    return pl.pallas_call(
        functools.partial(_gdn_kernel, hg=hg),
        grid=(b, ng, nc),
        in_specs=[
            pl.BlockSpec((None, CHUNK, gw), col(COL_GDN_Q)),
            pl.BlockSpec((None, CHUNK, gw), col(COL_GDN_K)),
            pl.BlockSpec((None, CHUNK, gw), col(COL_GDN_V)),
            pl.BlockSpec((None, CHUNK, gw), col(COL_Z)),
            pl.BlockSpec((None, CHUNK, 3 * LANES), lambda bi, gi, ci: (bi, ci, 0)),
            pl.BlockSpec((3 * LANES, hg * N_GQ * LANES), lambda bi, gi, ci: (0, gi)),
            pl.BlockSpec((hg * CHUNK, 3 * LANES), lambda bi, gi, ci: (gi, 0)),
            pl.BlockSpec((CONV_K, gw), wcol(COL_GDN_Q)),
            pl.BlockSpec((CONV_K, gw), wcol(COL_GDN_K)),
            pl.BlockSpec((CONV_K, gw), wcol(COL_GDN_V)),
            pl.BlockSpec((1, HEAD_DIM), const),
            pl.BlockSpec((3 * CHUNK, 2 * CHUNK), const),
        ],
        out_specs=pl.BlockSpec((None, CHUNK, gw),
                               lambda bi, gi, ci: (bi, jnp.maximum(ci - 1, 0), gi)),
        out_shape=jax.ShapeDtypeStruct((b, seq, HEADS * HEAD_DIM), BF16),
        scratch_shapes=[
            pltpu.VMEM((hg, HEAD_DIM, HEAD_DIM), F32),
            pltpu.VMEM((3, CHUNK, gw), BF16),
        ],
        compiler_params=_params("parallel", "parallel", "arbitrary"),
        name="gdn",
    )(p1, p1, p1, p1, yp, sel, oneh, conv_w, conv_w, conv_w, gnorm, shift3)


def _fox_kernel(q_ref, k_ref, v_ref, eq_ref, ek_ref, gq_ref, gk_ref, o_ref,
                kaug_ref, vaug_ref, kmax_ref, *, tq, lp):
    h = pl.program_id(1)
    i = pl.program_id(2)

    @pl.when(i == 0)
    def _():
        def body(r, kmax):
            rows = pl.ds(pl.multiple_of(r * 128, 128), 128)
            kn = _norm_rows(k_ref[rows, :].astype(F32), gk_ref[...])
            kaug_ref[rows, 0:HEAD_DIM] = kn.astype(BF16)
            kaug_ref[rows, HEAD_DIM:2 * HEAD_DIM] = ek_ref[rows, :]
            vaug_ref[rows, 0:HEAD_DIM] = v_ref[rows, :]
            vaug_ref[rows, HEAD_DIM:2 * HEAD_DIM] = jnp.ones((128, HEAD_DIM), BF16)
            n2 = jnp.sum(kn * kn, axis=-1, keepdims=True)
            return jnp.maximum(kmax, jnp.max(n2, axis=0, keepdims=True))
        kmax2 = lax.fori_loop(0, lp // 128, body, jnp.zeros((1, 1), F32))
        kmax_ref[...] = jnp.broadcast_to(jnp.sqrt(kmax2), (1, LANES))

    q0 = pl.multiple_of(CHUNK + i * tq, 128)
    rows_q = pl.ds(q0, tq)
    qn = _norm_rows(q_ref[rows_q, :].astype(F32), gq_ref[...]) * (HEAD_DIM ** -0.5 * LOG2E)
    lane = lax.broadcasted_iota(jnp.int32, (tq, LANES), 1)
    mine = lax.shift_right_logical(lane, 3) == h
    eqm = jnp.where(mine, eq_ref[rows_q, :], jnp.zeros((), BF16))
    q16 = qn.astype(BF16)

    qnorm = jnp.sqrt(jnp.sum(qn * qn, axis=-1, keepdims=True))
    bound = (qnorm * kmax_ref[:, 0:1] * 1.02).astype(BF16)
    safe = jnp.max(bound.astype(F32)) <= FOX_SAFE_BOUND

    rr = lax.broadcasted_iota(jnp.int32, (tq, tq), 0)
    cc = lax.broadcasted_iota(jnp.int32, (tq, tq), 1)
    causal = rr >= cc

    def blocks(j):
        return pl.ds(pl.multiple_of(CHUNK + j * tq, 128), tq)

    @pl.when(safe)
    def _():
        eqb = jnp.where(lane == FOX_EXTRA * h + FOX_M_LANE, -bound, eqm)
        qaug = jnp.concatenate([q16, eqb], axis=1)
        s = _dot_nt(qaug, kaug_ref[0:CHUNK, :])
        p = jnp.exp2(jnp.where(lane >= FRONT_PAD, s, NEG_BIG))
        acc = _dot(p.astype(BF16), vaug_ref[0:CHUNK, :])

        def body(j, acc):
            s = _dot_nt(qaug, kaug_ref[blocks(j), :])
            return acc + _dot(jnp.exp2(s).astype(BF16), vaug_ref[blocks(j), :])
        acc = lax.fori_loop(0, i, body, acc)

        s = _dot_nt(qaug, kaug_ref[rows_q, :])
        p = jnp.exp2(jnp.where(causal, s, NEG_BIG))
        acc = acc + _dot(p.astype(BF16), vaug_ref[rows_q, :])
        o_ref[...] = (acc[:, :HEAD_DIM] / acc[:, HEAD_DIM:]).astype(o_ref.dtype)

    @pl.when(jnp.logical_not(safe))
    def _():
        qaug = jnp.concatenate([q16, eqm], axis=1)
        s = _dot_nt(qaug, kaug_ref[0:CHUNK, :])
        s = jnp.where(lane >= FRONT_PAD, s, NEG_BIG)
        m = jnp.max(s, axis=-1, keepdims=True)
        acc = _dot(jnp.exp2(s - m).astype(BF16), vaug_ref[0:CHUNK, :])

        def update(s, vrows, m, acc):
            m_new = jnp.maximum(m, jnp.max(s, axis=-1, keepdims=True))
            p = jnp.exp2(s - m_new)
            return m_new, jnp.exp2(m - m_new) * acc + _dot(p.astype(BF16), vrows)

        def body(j, carry):
            s = _dot_nt(qaug, kaug_ref[blocks(j), :])
            return update(s, vaug_ref[blocks(j), :], *carry)
        m, acc = lax.fori_loop(0, i, body, (m, acc))

        s = _dot_nt(qaug, kaug_ref[rows_q, :])
        m, acc = update(jnp.where(causal, s, NEG_BIG), vaug_ref[rows_q, :], m, acc)
        o_ref[...] = (acc[:, :HEAD_DIM] / acc[:, HEAD_DIM:]).astype(o_ref.dtype)


def _fox(p1, eq, ek, gq, gk, *, seq, tq):
    b, lp, _ = p1.shape
    colmap = lambda base: (lambda bi, hi, qi: (bi, 0, base + hi))
    const = lambda bi, hi, qi: (0, 0)
    return pl.pallas_call(
        functools.partial(_fox_kernel, tq=tq, lp=lp),
        grid=(b, HEADS, seq // tq),
        in_specs=[
            pl.BlockSpec((None, lp, HEAD_DIM), colmap(COL_FOX_Q)),
            pl.BlockSpec((None, lp, HEAD_DIM), colmap(COL_FOX_K)),
            pl.BlockSpec((None, lp, HEAD_DIM), colmap(COL_FOX_V)),
            pl.BlockSpec((None, lp, LANES), lambda bi, hi, qi: (bi, 0, 0)),
            pl.BlockSpec((None, lp, LANES), lambda bi, hi, qi: (bi, 0, 0)),
            pl.BlockSpec((1, HEAD_DIM), const),
            pl.BlockSpec((1, HEAD_DIM), const),
        ],
        out_specs=pl.BlockSpec((None, tq, HEAD_DIM), lambda bi, hi, qi: (bi, qi, hi)),
        out_shape=jax.ShapeDtypeStruct((b, seq, HEADS * HEAD_DIM), BF16),
        scratch_shapes=[
            pltpu.VMEM((lp, 2 * HEAD_DIM), BF16),
            pltpu.VMEM((lp, 2 * HEAD_DIM), BF16),
            pltpu.VMEM((1, LANES), F32),
        ],
        compiler_params=_params("parallel", "parallel", "arbitrary"),
        name="fox",
    )(p1, p1, p1, eq, ek, gq, gk)


def _merge_kernel(x_ref, g_ref, ya_ref, yb_ref, wga_ref, wgb_ref, wa_ref, wb_ref, wo_ref,
                  o_ref, u_ref):
    j = pl.program_id(1)

    @pl.when(j == 0)
    def _():
        x = x_ref[...]
        u_ref[...] = _norm_rows(x, g_ref[...]).astype(BF16)
        o_ref[...] = x

    u = u_ref[...]
    ga = _dot(u, wga_ref[...])
    gb = _dot(u, wgb_ref[...])
    pa = _dot(ya_ref[...], wa_ref[...])
    pb = _dot(yb_ref[...], wb_ref[...])
    mix = _sigmoid(ga) * pa + _sigmoid(gb) * pb
    o_ref[...] += _dot(mix.astype(BF16), wo_ref[...])


def _merge(x2d, g, ya, yb, wga, wgb, wa, wb, wo, *, tm, tj):
    m, d = x2d.shape
    rowblk = lambda i, j: (i, 0)
    colblk = lambda i, j: (0, j)
    return pl.pallas_call(
        _merge_kernel,
        grid=(m // tm, d // tj),
        in_specs=[
            pl.BlockSpec((tm, d), rowblk),
            pl.BlockSpec((1, d), lambda i, j: (0, 0)),
            pl.BlockSpec((tm, d), rowblk),
            pl.BlockSpec((tm, d), rowblk),
            pl.BlockSpec((d, tj), colblk),
            pl.BlockSpec((d, tj), colblk),
            pl.BlockSpec((d, tj), colblk),
            pl.BlockSpec((d, tj), colblk),
            pl.BlockSpec((tj, d), lambda i, j: (j, 0)),
        ],
        out_specs=pl.BlockSpec((tm, d), rowblk),
        out_shape=jax.ShapeDtypeStruct((m, d), F32),
        scratch_shapes=[pltpu.VMEM((tm, d), BF16)],
        compiler_params=_params("parallel", "arbitrary"),
        name="merge",
    )(x2d, g, ya, yb, wga, wgb, wa, wb, wo)


def _mlp_kernel(h_ref, g_ref, wu_ref, wd_ref, gf_ref, o_ref, u_ref):
    f = pl.program_id(1)

    @pl.when(f == 0)
    def _():
        x = h_ref[...]
        u_ref[...] = _norm_rows(x, g_ref[...]).astype(BF16)
        o_ref[...] = x

    a = jnp.maximum(_dot(u_ref[...], wu_ref[...]), 0.0)
    o_ref[...] += _dot((a * a).astype(BF16), wd_ref[...])

    @pl.when(f == pl.num_programs(1) - 1)
    def _():
        o_ref[...] = _norm_rows(o_ref[...], gf_ref[...])


def _mlp(h2d, g, wu, wd, gf, *, tm, tf):
    m, d = h2d.shape
    dff = wu.shape[1]
    rowblk = lambda i, f: (i, 0)
    return pl.pallas_call(
        _mlp_kernel,
        grid=(m // tm, dff // tf),
        in_specs=[
            pl.BlockSpec((tm, d), rowblk),
            pl.BlockSpec((1, d), lambda i, f: (0, 0)),
            pl.BlockSpec((d, tf), lambda i, f: (0, f)),
            pl.BlockSpec((tf, d), lambda i, f: (f, 0)),
            pl.BlockSpec((1, d), lambda i, f: (0, 0)),
        ],
        out_specs=pl.BlockSpec((tm, d), rowblk),
        out_shape=jax.ShapeDtypeStruct((m, d), F32),
        scratch_shapes=[pltpu.VMEM((tm, d), BF16)],
        compiler_params=_params("parallel", "arbitrary"),
        name="mlp",
    )(h2d, g, wu, wd, gf)


def _constants():
    r = np.arange(128)
    ltri = (r[:, None] >= r[None, :]).astype(np.float32)
    ltri3 = np.concatenate([ltri, ltri, ltri], axis=1)

    sq = np.zeros((3 * LANES, LANES), np.float32)
    sk = np.zeros((3 * LANES, LANES), np.float32)
    cq = np.zeros((1, LANES), np.float32)
    ck = np.zeros((1, LANES), np.float32)
    for h in range(HEADS):
        for p in range(3):
            sq[p * LANES + SM_F + h, FOX_EXTRA * h + p] = 1.0
            sk[p * LANES + SM_F + h, FOX_EXTRA * h + 3 + p] = -1.0
            cq[0, FOX_EXTRA * h + 3 + p] = 1.0
            ck[0, FOX_EXTRA * h + p] = 1.0
        ck[0, FOX_EXTRA * h + FOX_M_LANE] = 1.0

    shift3 = np.zeros((3 * CHUNK, 2 * CHUNK), np.float32)
    for s in range(1, CONV_K):
        shift3[(s - 1) * CHUNK + r, CHUNK + r - s] = 1.0

    src_lane = np.zeros((HEADS * N_GQ * LANES,), np.int64)
    for h in range(HEADS):
        for q in range(N_GQ):
            src_lane[(h * N_GQ + q) * LANES:(h * N_GQ + q + 1) * LANES] = 16 * q + h
    rows3 = np.arange(3 * LANES) % LANES
    sel = (rows3[:, None] == src_lane[None, :]).astype(np.float32)
    oneh = (np.repeat(16 * GQ_G + np.arange(HEADS), CHUNK)[:, None] == rows3[None, :]).astype(np.float32)

    as_bf = lambda a: jnp.asarray(a, BF16)
    return dict(ltri3=as_bf(ltri3), sq=as_bf(sq), sk=as_bf(sk), cq=jnp.asarray(cq),
                ck=jnp.asarray(ck), shift3=as_bf(shift3), sel=as_bf(sel), oneh=as_bf(oneh))


def _row128(vec, offset):
    return jnp.zeros((1, LANES), F32).at[0, offset:offset + HEADS].set(vec.astype(F32))


def _pick_rows_tile(total, target):
    best = 128
    for t in range(128, target + 1, 128):
        if total % t == 0:
            best = t
    return best


def _layer(x, meta_tokens, mix_norm_g, w_in, conv_w, a_log, dt_bias, gdn_norm_g, w_o_gdn,
           fox_q_norm_g, fox_k_norm_g, fox_f_bias, w_o_fox, w_out, mlp_norm_g, w_up, w_down,
           final_norm_g):
    b, seq, d = x.shape
    lp = CHUNK + seq
    qk = HEADS * HEAD_DIM

    o_z = 3 * qk
    o_b = o_z + qk
    o_a = o_b + HEADS
    o_fq = o_a + HEADS
    o_f = o_fq + 3 * qk
    o_ga = o_f + HEADS
    o_gb = o_ga + d

    w1 = jnp.concatenate([w_in[:, :o_b], w_in[:, o_fq:o_f]], axis=1).astype(BF16)
    ws = jnp.concatenate(
        [w_in[:, o_b:o_fq], w_in[:, o_f:o_ga],
         jnp.zeros((d, LANES - 3 * HEADS), w_in.dtype)], axis=1).astype(BF16)
    wga = w_in[:, o_ga:o_gb].astype(BF16)
    wgb = w_in[:, o_gb:].astype(BF16)

    meta = jnp.broadcast_to(meta_tokens[None].astype(x.dtype), (b, N_META, d))
    h_pad = jnp.concatenate([jnp.zeros((b, FRONT_PAD, d), x.dtype), meta, x], axis=1)

    row = lambda v: v.reshape(1, -1).astype(F32)
    cst = _constants()

    tm_in = _pick_rows_tile(lp, 1408)
    p1, ps = _inproj(h_pad.reshape(b * lp, d), row(mix_norm_g), w1, ws, tm=tm_in, tn=512)
    p1 = p1.reshape(b, lp, P1_COLS)
    ps = ps.reshape(b, lp, LANES)

    yp, eq, ek = _gate_prep(ps, _row128(a_log, SM_A), _row128(dt_bias, SM_A),
                            _row128(fox_f_bias, SM_F), cst["ltri3"], cst["sq"], cst["sk"],
                            cst["cq"], cst["ck"])
    ya = _gdn(p1, yp, cst["sel"], cst["oneh"], conv_w.astype(F32), row(gdn_norm_g),
              cst["shift3"], seq=seq, hg=8)
    yb = _fox(p1, eq, ek, row(fox_q_norm_g), row(fox_k_norm_g), seq=seq, tq=512)

    x2d = x.reshape(b * seq, d)
    h1 = _merge(x2d, row(mix_norm_g), ya.reshape(b * seq, qk), yb.reshape(b * seq, qk),
                wga, wgb, w_o_gdn.astype(BF16), w_o_fox.astype(BF16), w_out.astype(BF16),
                tm=512, tj=256)
    out = _mlp(h1, row(mlp_norm_g), w_up.astype(BF16), w_down.astype(BF16),
               row(final_norm_g), tm=512, tf=512)
    return out.reshape(b, seq, d)


def kernel(x, meta_tokens, mix_norm_g, w_in, conv_w, a_log, dt_bias, gdn_norm_g, w_o_gdn,
           fox_q_norm_g, fox_k_norm_g, fox_f_bias, w_o_fox, w_out, mlp_norm_g, w_up, w_down,
           final_norm_g):
    assert w_in.shape[0] == 1, "single-layer block"
    return _layer(x, meta_tokens, mix_norm_g[0], w_in[0], conv_w[0], a_log[0], dt_bias[0],
                  gdn_norm_g[0], w_o_gdn[0], fox_q_norm_g[0], fox_k_norm_g[0], fox_f_bias[0],
                  w_o_fox[0], w_out[0], mlp_norm_g[0], w_up[0], w_down[0], final_norm_g)
```

```python
import functools
import math

import numpy as np
import jax
import jax.numpy as jnp
from jax import lax
from jax.experimental import pallas as pl
from jax.experimental.pallas import tpu as pltpu

F32 = jnp.float32
BF16 = jnp.bfloat16

NORM_EPS = 1e-6
N_META = 16
HEADS = 16
HEAD_DIM = 128
CONV_K = 4
LANES = 128
CHUNK = 128
FRONT_PAD = CHUNK - N_META
NEG_BIG = -1e30
LOG2E = math.log2(math.e)
VMEM_LIMIT = 56 * 1024 * 1024

COL_GDN_Q, COL_GDN_K, COL_GDN_V, COL_Z = 0, 16, 32, 48
COL_FOX_Q, COL_FOX_K, COL_FOX_V = 64, 80, 96
P1_COLS = 112 * LANES
SM_B, SM_A, SM_F = 0, 16, 32
GQ_G, GQ_BETA, GQ_EG, GQ_EKD, GQ_BEG = range(5)
N_GQ = 5
FOX_EXTRA = 8
FOX_M_LANE = 6
FOX_SAFE_BOUND = 40.0


def _params(*sem):
    return pltpu.CompilerParams(dimension_semantics=sem, vmem_limit_bytes=VMEM_LIMIT)


def _norm_rows(x, g):
    ms = jnp.mean(x * x, axis=-1, keepdims=True)
    return x * lax.rsqrt(ms + NORM_EPS) * g


def _split3(x):
    hi = x.astype(BF16)
    r = x - hi.astype(F32)
    mid = r.astype(BF16)
    lo = (r - mid.astype(F32)).astype(BF16)
    return hi, mid, lo


def _dot(a, b):
    return jnp.dot(a, b, preferred_element_type=F32)


def _dot_nt(a, b):
    return lax.dot_general(a, b, (((1,), (1,)), ((), ())), preferred_element_type=F32)


def _dot_tn(a, b):
    return lax.dot_general(a, b, (((0,), (0,)), ((), ())), preferred_element_type=F32)


def _sigmoid(x):
    return 1.0 / (1.0 + jnp.exp(-x))


def _silu(x):
    return x * _sigmoid(x)


def _softplus(x):
    return jnp.maximum(x, 0.0) + jnp.log1p(jnp.exp(-jnp.abs(x)))


def _inproj_kernel(x_ref, g_ref, w_ref, ws_ref, o_ref, os_ref, xn_ref, *, tm):
    @pl.when(pl.program_id(1) == 0)
    def _():
        def body(r, c):
            rows = pl.ds(pl.multiple_of(r * 128, 128), 128)
            xn_ref[rows, :] = _norm_rows(x_ref[rows, :], g_ref[...]).astype(BF16)
            return c
        lax.fori_loop(0, tm // 128, body, 0)
        os_ref[...] = _dot(xn_ref[...], ws_ref[...])

    o_ref[...] = _dot(xn_ref[...], w_ref[...]).astype(o_ref.dtype)


def _inproj(h2d, g, w, ws, *, tm, tn):
    m, d = h2d.shape
    n = w.shape[1]
    return pl.pallas_call(
        functools.partial(_inproj_kernel, tm=tm),
        grid=(m // tm, n // tn),
        in_specs=[
            pl.BlockSpec((tm, d), lambda i, j: (i, 0)),
            pl.BlockSpec((1, d), lambda i, j: (0, 0)),
            pl.BlockSpec((d, tn), lambda i, j: (0, j)),
            pl.BlockSpec((d, LANES), lambda i, j: (0, 0)),
        ],
        out_specs=[
            pl.BlockSpec((tm, tn), lambda i, j: (i, j)),
            pl.BlockSpec((tm, LANES), lambda i, j: (i, 0)),
        ],
        out_shape=[
            jax.ShapeDtypeStruct((m, n), BF16),
            jax.ShapeDtypeStruct((m, LANES), F32),
        ],
        scratch_shapes=[pltpu.VMEM((tm, d), BF16)],
        compiler_params=_params("parallel", "arbitrary"),
        name="inproj",
    )(h2d, g, w, ws)


def _gate_prep_kernel(ps_ref, alog_ref, dtb_ref, fb_ref, ltri_ref, sq_ref, sk_ref, cq_ref, ck_ref,
                      yp_ref, eq_ref, ek_ref, *, nblk):
    lane = lax.broadcasted_iota(jnp.int32, (CHUNK, LANES), 1)
    rowi = lax.broadcasted_iota(jnp.int32, (CHUNK, LANES), 0)

    def body(i, carry):
        rows = pl.ds(pl.multiple_of(i * CHUNK, CHUNK), CHUNK)
        x = ps_ref[rows, :]
        valid = (i * CHUNK + rowi) >= FRONT_PAD

        beta = jnp.where(valid, _sigmoid(x), 0.0)
        gstep = jnp.where(valid, -jnp.exp(alog_ref[...]) * _softplus(x + dtb_ref[...]), 0.0)
        gcum = _dot(ltri_ref[...], jnp.concatenate(_split3(gstep), axis=0))
        eg = jnp.exp(gcum)
        ekd = jnp.exp(gcum[CHUNK - 1:CHUNK, :] - gcum)
        beg = pltpu.roll(beta, SM_A - SM_B, 1) * eg
        y = jnp.where(lane < 16 * (GQ_EKD + 1), pltpu.roll(ekd, 16 * GQ_EKD - SM_A, 1),
                      pltpu.roll(beg, 16 * GQ_BEG - SM_A, 1))
        y = jnp.where(lane < 16 * (GQ_EG + 1), pltpu.roll(eg, 16 * GQ_EG - SM_A, 1), y)
        y = jnp.where(lane < 16 * (GQ_BETA + 1), pltpu.roll(beta, 16 * GQ_BETA - SM_B, 1), y)
        y = jnp.where(lane < 16 * (GQ_G + 1), pltpu.roll(gcum, LANES + 16 * GQ_G - SM_A, 1), y)
        yp_ref[rows, :] = jnp.concatenate(_split3(y), axis=1)

        xf = x + fb_ref[...]
        ls = (jnp.minimum(xf, 0.0) - jnp.log1p(jnp.exp(-jnp.abs(xf)))) * LOG2E
        cum = _dot(ltri_ref[...], jnp.concatenate(_split3(ls), axis=0)) + carry
        cp = jnp.concatenate(_split3(cum), axis=1)
        eq_ref[rows, :] = (_dot(cp, sq_ref[...]) + cq_ref[...]).astype(BF16)
        ek_ref[rows, :] = (_dot(cp, sk_ref[...]) + ck_ref[...]).astype(BF16)
        return cum[CHUNK - 1:CHUNK, :]

    lax.fori_loop(0, nblk, body, jnp.zeros((1, LANES), F32))


def _gate_prep(ps, alog_row, dtb_row, fb_row, ltri3, sq, sk, cq, ck):
    b, lp, _ = ps.shape
    const = lambda bi: (0, 0)
    return pl.pallas_call(
        functools.partial(_gate_prep_kernel, nblk=lp // CHUNK),
        grid=(b,),
        in_specs=[
            pl.BlockSpec((None, lp, LANES), lambda bi: (bi, 0, 0)),
            pl.BlockSpec((1, LANES), const),
            pl.BlockSpec((1, LANES), const),
            pl.BlockSpec((1, LANES), const),
            pl.BlockSpec((CHUNK, 3 * CHUNK), const),
            pl.BlockSpec((3 * LANES, LANES), const),
            pl.BlockSpec((3 * LANES, LANES), const),
            pl.BlockSpec((1, LANES), const),
            pl.BlockSpec((1, LANES), const),
        ],
        out_specs=[
            pl.BlockSpec((None, lp, 3 * LANES), lambda bi: (bi, 0, 0)),
            pl.BlockSpec((None, lp, LANES), lambda bi: (bi, 0, 0)),
            pl.BlockSpec((None, lp, LANES), lambda bi: (bi, 0, 0)),
        ],
        out_shape=[
            jax.ShapeDtypeStruct((b, lp, 3 * LANES), BF16),
            jax.ShapeDtypeStruct((b, lp, LANES), BF16),
            jax.ShapeDtypeStruct((b, lp, LANES), BF16),
        ],
        compiler_params=_params("parallel"),
        name="gate_prep",
    )(ps, alog_row, dtb_row, fb_row, ltri3, sq, sk, cq, ck)


def _gdn_kernel(q_ref, k_ref, v_ref, z_ref, yp_ref, sel_ref, oneh_ref, wq_ref, wk_ref, wv_ref,
                gn_ref, shift_ref, o_ref, s_ref, prev_ref, *, hg):
    c = pl.program_id(2)

    @pl.when(c == 0)
    def _():
        s_ref[...] = jnp.zeros_like(s_ref)
        prev_ref[...] = jnp.zeros_like(prev_ref)

    row = lax.broadcasted_iota(jnp.int32, (CHUNK, CHUNK), 0)
    col = lax.broadcasted_iota(jnp.int32, (CHUNK, CHUNK), 1)
    tril = row >= col
    strict = row > col
    heads = range(hg)
    hsl = lambda a, u: a[:, u * HEAD_DIM:(u + 1) * HEAD_DIM]

    yp = yp_ref[...]
    gb = _dot(yp, sel_ref[...])
    g_rows = _dot_nt(oneh_ref[...], yp)
    gq = lambda u, q: gb[:, (u * N_GQ + q) * LANES:(u * N_GQ + q + 1) * LANES]

    def conv_act(x_ref, w_ref, t):
        x16 = x_ref[...]
        x2 = jnp.concatenate([prev_ref[t], x16], axis=0)
        prev_ref[t] = x16
        sh = _dot(shift_ref[...], x2)
        w = w_ref[...]
        acc = x16.astype(F32) * w[CONV_K - 1:CONV_K, :]
        for s in range(1, CONV_K):
            acc = acc + sh[(s - 1) * CHUNK:s * CHUNK, :] * w[CONV_K - 1 - s:CONV_K - s, :]
        return _silu(acc)

    qa = conv_act(q_ref, wq_ref, 0)
    ka = conv_act(k_ref, wk_ref, 1)
    va = conv_act(v_ref, wv_ref, 2)

    def l2n(x):
        return x * lax.rsqrt(jnp.sum(x * x, axis=-1, keepdims=True) + NORM_EPS)

    qn = [l2n(hsl(qa, u)) * (HEAD_DIM ** -0.5) for u in heads]
    kn = [l2n(hsl(ka, u)) for u in heads]
    k16 = [kn[u].astype(BF16) for u in heads]

    state = [s_ref[u] for u in heads]
    r1 = [_dot(jnp.concatenate([kn[u] * gq(u, GQ_BEG), qn[u] * gq(u, GQ_EG)], axis=0).astype(BF16),
               state[u].astype(BF16)) for u in heads]

    kq = [_dot_nt(jnp.concatenate([kn[u] * gq(u, GQ_BETA), qn[u]], axis=0).astype(BF16), k16[u])
          for u in heads]
    ediff = [jnp.exp(jnp.minimum(gq(u, GQ_G) - g_rows[u * CHUNK:(u + 1) * CHUNK, :], 0.0))
             for u in heads]
    a = [kq[u][:CHUNK] * jnp.where(strict, ediff[u], 0.0) for u in heads]
    qk_m = [(kq[u][CHUNK:] * jnp.where(tril, ediff[u], 0.0)).astype(BF16) for u in heads]

    a16 = [a[u].astype(BF16) for u in heads]
    yk = [_dot(a16[u], a16[u]) for u in heads]
    sk = [-a[u] for u in heads]
    nlev = int(np.log2(CHUNK)) - 1
    for lev in range(nlev):
        yk16 = [yk[u].astype(BF16) for u in heads]
        if lev < nlev - 1:
            zz = [_dot(jnp.concatenate([sk[u], yk[u]], axis=0).astype(BF16), yk16[u]) for u in heads]
            sk = [sk[u] + yk[u] + zz[u][:CHUNK] for u in heads]
            yk = [zz[u][CHUNK:] for u in heads]
        else:
            sk = [sk[u] + yk[u] + _dot(sk[u].astype(BF16), yk16[u]) for u in heads]

    rhs = [hsl(va, u) * gq(u, GQ_BETA) - r1[u][:CHUNK] for u in heads]
    v_new = [(rhs[u] + _dot(sk[u].astype(BF16), rhs[u].astype(BF16))).astype(BF16) for u in heads]
    o = [r1[u][CHUNK:] + _dot(qk_m[u], v_new[u]) for u in heads]
    for u in heads:
        eg_last = gq(u, GQ_EG)[CHUNK - 1:CHUNK, :]
        s_ref[u] = state[u] * eg_last + _dot_tn((kn[u] * gq(u, GQ_EKD)).astype(BF16), v_new[u])

    on = jnp.concatenate([_norm_rows(o[u], gn_ref[...]) for u in heads], axis=1)
    o_ref[...] = (on * _silu(z_ref[...].astype(F32))).astype(o_ref.dtype)


def _gdn(p1, yp, sel, oneh, conv_w, gnorm, shift3, *, seq, hg):
    b, lp, _ = p1.shape
    nc = lp // CHUNK
    gw = hg * HEAD_DIM
    ng = HEADS // hg
    col = lambda base: (lambda bi, gi, ci: (bi, ci, base * LANES // gw + gi))
    wcol = lambda base: (lambda bi, gi, ci: (0, base * LANES // gw + gi))
    const = lambda bi, gi, ci: (0, 0)
    return pl.pallas_call(
        functools.partial(_gdn_kernel, hg=hg),
        grid=(b, ng, nc),
        in_specs=[
            pl.BlockSpec((None, CHUNK, gw), col(COL_GDN_Q)),
            pl.BlockSpec((None, CHUNK, gw), col(COL_GDN_K)),
            pl.BlockSpec((None, CHUNK, gw), col(COL_GDN_V)),
            pl.BlockSpec((None, CHUNK, gw), col(COL_Z)),
            pl.BlockSpec((None, CHUNK, 3 * LANES), lambda bi, gi, ci: (bi, ci, 0)),
            pl.BlockSpec((3 * LANES, hg * N_GQ * LANES), lambda bi, gi, ci: (0, gi)),
            pl.BlockSpec((hg * CHUNK, 3 * LANES), lambda bi, gi, ci: (gi, 0)),
            pl.BlockSpec((CONV_K, gw), wcol(COL_GDN_Q)),
            pl.BlockSpec((CONV_K, gw), wcol(COL_GDN_K)),
            pl.BlockSpec((CONV_K, gw), wcol(COL_GDN_V)),
            pl.BlockSpec((1, HEAD_DIM), const),
            pl.BlockSpec((3 * CHUNK, 2 * CHUNK), const),
        ],
        out_specs=pl.BlockSpec((None, CHUNK, gw),
                               lambda bi, gi, ci: (bi, jnp.maximum(ci - 1, 0), gi)),
        out_shape=jax.ShapeDtypeStruct((b, seq, HEADS * HEAD_DIM), BF16),
        scratch_shapes=[
            pltpu.VMEM((hg, HEAD_DIM, HEAD_DIM), F32),
            pltpu.VMEM((3, CHUNK, gw), BF16),
        ],
        compiler_params=_params("parallel", "parallel", "arbitrary"),
        name="gdn",
    )(p1, p1, p1, p1, yp, sel, oneh, conv_w, conv_w, conv_w, gnorm, shift3)


def _fox_kernel(q_ref, k_ref, v_ref, eq_ref, ek_ref, gq_ref, gk_ref, o_ref,
                kaug_ref, vaug_ref, *, tq, lp):
    h = pl.program_id(1)
    i = pl.program_id(2)

    @pl.when(i == 0)
    def _():
        def build(rows, n):
            kn = _norm_rows(k_ref[rows, :].astype(F32), gk_ref[...])
            kaug_ref[rows, 0:HEAD_DIM] = kn.astype(BF16)
            kaug_ref[rows, HEAD_DIM:2 * HEAD_DIM] = ek_ref[rows, :]
            vaug_ref[rows, 0:HEAD_DIM] = v_ref[rows, :]
            vaug_ref[rows, HEAD_DIM:2 * HEAD_DIM] = jnp.ones((n, HEAD_DIM), BF16)

        def body(r, c):
            build(pl.ds(pl.multiple_of(r * tq, 128), tq), tq)
            return c
        lax.fori_loop(0, lp // tq, body, 0)
        if lp % tq:
            build(pl.ds(lp - lp % tq, lp % tq), lp % tq)

    q0 = pl.multiple_of(CHUNK + i * tq, 128)
    rows_q = pl.ds(q0, tq)
    qn = _norm_rows(q_ref[rows_q, :].astype(F32), gq_ref[...]) * (HEAD_DIM ** -0.5 * LOG2E)
    lane = lax.broadcasted_iota(jnp.int32, (tq, LANES), 1)
    mine = lax.shift_right_logical(lane, 3) == h
    eqm = jnp.where(mine, eq_ref[rows_q, :], jnp.zeros((), BF16))
    q16 = qn.astype(BF16)

    gmax = (jnp.max(jnp.abs(gq_ref[...]), axis=-1, keepdims=True)
            * jnp.max(jnp.abs(gk_ref[...]), axis=-1, keepdims=True))
    bound = (gmax * (HEAD_DIM ** 0.5 * LOG2E * 1.02)).astype(BF16)
    safe = bound.astype(F32)[0, 0] <= FOX_SAFE_BOUND

    rr = lax.broadcasted_iota(jnp.int32, (tq, tq), 0)
    cc = lax.broadcasted_iota(jnp.int32, (tq, tq), 1)
    causal = rr >= cc

    def blocks(j):
        return pl.ds(pl.multiple_of(CHUNK + j * tq, 128), tq)

    @pl.when(safe)
    def _():
        eqb = jnp.where(lane == FOX_EXTRA * h + FOX_M_LANE, -bound, eqm)
        qaug = jnp.concatenate([q16, eqb], axis=1)
        half = tq // 2

        def probs(qa, krows, mask=None):
            s = _dot_nt(qa, kaug_ref[krows, :])
            if mask is not None:
                s = jnp.where(mask, s, NEG_BIG)
            return jnp.exp2(s).astype(BF16)

        def finish(acc, rows):
            o_ref[rows, :] = (acc[:, :HEAD_DIM] / acc[:, HEAD_DIM:]).astype(o_ref.dtype)

        for ii in range(pl.cdiv(lp - CHUNK, tq)):
            @pl.when(i == ii)
            def _(ii=ii):
                acc = _dot(probs(qaug, slice(0, CHUNK), lane >= FRONT_PAD), vaug_ref[0:CHUNK, :])
                for j in range(ii):
                    kr = slice(CHUNK + j * tq, CHUNK + (j + 1) * tq)
                    acc = acc + _dot(probs(qaug, kr), vaug_ref[kr, :])
                k_lo = slice(CHUNK + ii * tq, CHUNK + ii * tq + half)
                k_all = slice(CHUNK + ii * tq, CHUNK + (ii + 1) * tq)
                p_lo = probs(qaug[:half], k_lo, causal[:half, :half])
                p_hi = probs(qaug[half:], k_all, causal[half:, :])
                finish(acc[:half] + _dot(p_lo, vaug_ref[k_lo, :]), slice(0, half))
                finish(acc[half:] + _dot(p_hi, vaug_ref[k_all, :]), slice(half, tq))

    @pl.when(jnp.logical_not(safe))
    def _():
        qaug = jnp.concatenate([q16, eqm], axis=1)
        s = _dot_nt(qaug, kaug_ref[0:CHUNK, :])
        s = jnp.where(lane >= FRONT_PAD, s, NEG_BIG)
        m = jnp.max(s, axis=-1, keepdims=True)
        acc = _dot(jnp.exp2(s - m).astype(BF16), vaug_ref[0:CHUNK, :])

        def update(s, vrows, m, acc):
            m_new = jnp.maximum(m, jnp.max(s, axis=-1, keepdims=True))
            p = jnp.exp2(s - m_new)
            return m_new, jnp.exp2(m - m_new) * acc + _dot(p.astype(BF16), vrows)

        def body(j, carry):
            s = _dot_nt(qaug, kaug_ref[blocks(j), :])
            return update(s, vaug_ref[blocks(j), :], *carry)
        m, acc = lax.fori_loop(0, i, body, (m, acc))

        s = _dot_nt(qaug, kaug_ref[rows_q, :])
        m, acc = update(jnp.where(causal, s, NEG_BIG), vaug_ref[rows_q, :], m, acc)
        o_ref[...] = (acc[:, :HEAD_DIM] / acc[:, HEAD_DIM:]).astype(o_ref.dtype)


def _fox(p1, eq, ek, gq, gk, *, seq, tq):
    b, lp, _ = p1.shape
    colmap = lambda base: (lambda bi, hi, qi: (bi, 0, base + hi))
    const = lambda bi, hi, qi: (0, 0)
    return pl.pallas_call(
        functools.partial(_fox_kernel, tq=tq, lp=lp),
        grid=(b, HEADS, seq // tq),
        in_specs=[
            pl.BlockSpec((None, lp, HEAD_DIM), colmap(COL_FOX_Q)),
            pl.BlockSpec((None, lp, HEAD_DIM), colmap(COL_FOX_K)),
            pl.BlockSpec((None, lp, HEAD_DIM), colmap(COL_FOX_V)),
            pl.BlockSpec((None, lp, LANES), lambda bi, hi, qi: (bi, 0, 0)),
            pl.BlockSpec((None, lp, LANES), lambda bi, hi, qi: (bi, 0, 0)),
            pl.BlockSpec((1, HEAD_DIM), const),
            pl.BlockSpec((1, HEAD_DIM), const),
        ],
        out_specs=pl.BlockSpec((None, tq, HEAD_DIM), lambda bi, hi, qi: (bi, qi, hi)),
        out_shape=jax.ShapeDtypeStruct((b, seq, HEADS * HEAD_DIM), BF16),
        scratch_shapes=[
            pltpu.VMEM((lp, 2 * HEAD_DIM), BF16),
            pltpu.VMEM((lp, 2 * HEAD_DIM), BF16),
        ],
        compiler_params=_params("parallel", "parallel", "arbitrary"),
        name="fox",
    )(p1, p1, p1, eq, ek, gq, gk)


def _merge_kernel(x_ref, g_ref, ya_ref, yb_ref, wga_ref, wgb_ref, wa_ref, wb_ref, wo_ref,
                  o_ref, u_ref):
    j = pl.program_id(1)

    @pl.when(j == 0)
    def _():
        x = x_ref[...]
        u_ref[...] = _norm_rows(x, g_ref[...]).astype(BF16)
        o_ref[...] = x

    u = u_ref[...]
    ga = _dot(u, wga_ref[...])
    gb = _dot(u, wgb_ref[...])
    pa = _dot(ya_ref[...], wa_ref[...])
    pb = _dot(yb_ref[...], wb_ref[...])
    mix = _sigmoid(ga) * pa + _sigmoid(gb) * pb
    o_ref[...] += _dot(mix.astype(BF16), wo_ref[...])


def _merge(x2d, g, ya, yb, wga, wgb, wa, wb, wo, *, tm, tj):
    m, d = x2d.shape
    rowblk = lambda i, j: (i, 0)
    colblk = lambda i, j: (0, j)
    return pl.pallas_call(
        _merge_kernel,
        grid=(m // tm, d // tj),
        in_specs=[
            pl.BlockSpec((tm, d), rowblk),
            pl.BlockSpec((1, d), lambda i, j: (0, 0)),
            pl.BlockSpec((tm, d), rowblk),
            pl.BlockSpec((tm, d), rowblk),
            pl.BlockSpec((d, tj), colblk),
            pl.BlockSpec((d, tj), colblk),
            pl.BlockSpec((d, tj), colblk),
            pl.BlockSpec((d, tj), colblk),
            pl.BlockSpec((tj, d), lambda i, j: (j, 0)),
        ],
        out_specs=pl.BlockSpec((tm, d), rowblk),
        out_shape=jax.ShapeDtypeStruct((m, d), F32),
        scratch_shapes=[pltpu.VMEM((tm, d), BF16)],
        compiler_params=_params("parallel", "arbitrary"),
        name="merge",
    )(x2d, g, ya, yb, wga, wgb, wa, wb, wo)


def _mlp_kernel(h_ref, g_ref, wu_ref, wd_ref, gf_ref, o_ref, u_ref):
    f = pl.program_id(1)

    @pl.when(f == 0)
    def _():
        x = h_ref[...]
        u_ref[...] = _norm_rows(x, g_ref[...]).astype(BF16)
        o_ref[...] = x

    a = jnp.maximum(_dot(u_ref[...], wu_ref[...]), 0.0)
    o_ref[...] += _dot((a * a).astype(BF16), wd_ref[...])

    @pl.when(f == pl.num_programs(1) - 1)
    def _():
        o_ref[...] = _norm_rows(o_ref[...], gf_ref[...])


def _mlp(h2d, g, wu, wd, gf, *, tm, tf):
    m, d = h2d.shape
    dff = wu.shape[1]
    rowblk = lambda i, f: (i, 0)
    return pl.pallas_call(
        _mlp_kernel,
        grid=(m // tm, dff // tf),
        in_specs=[
            pl.BlockSpec((tm, d), rowblk),
            pl.BlockSpec((1, d), lambda i, f: (0, 0)),
            pl.BlockSpec((d, tf), lambda i, f: (0, f)),
            pl.BlockSpec((tf, d), lambda i, f: (f, 0)),
            pl.BlockSpec((1, d), lambda i, f: (0, 0)),
        ],
        out_specs=pl.BlockSpec((tm, d), rowblk),
        out_shape=jax.ShapeDtypeStruct((m, d), F32),
        scratch_shapes=[pltpu.VMEM((tm, d), BF16)],
        compiler_params=_params("parallel", "arbitrary"),
        name="mlp",
    )(h2d, g, wu, wd, gf)


def _constants():
    r = np.arange(128)
    ltri = (r[:, None] >= r[None, :]).astype(np.float32)
    ltri3 = np.concatenate([ltri, ltri, ltri], axis=1)

    sq = np.zeros((3 * LANES, LANES), np.float32)
    sk = np.zeros((3 * LANES, LANES), np.float32)
    cq = np.zeros((1, LANES), np.float32)
    ck = np.zeros((1, LANES), np.float32)
    for h in range(HEADS):
        for p in range(3):
            sq[p * LANES + SM_F + h, FOX_EXTRA * h + p] = 1.0
            sk[p * LANES + SM_F + h, FOX_EXTRA * h + 3 + p] = -1.0
            cq[0, FOX_EXTRA * h + 3 + p] = 1.0
            ck[0, FOX_EXTRA * h + p] = 1.0
        ck[0, FOX_EXTRA * h + FOX_M_LANE] = 1.0

    shift3 = np.zeros((3 * CHUNK, 2 * CHUNK), np.float32)
    for s in range(1, CONV_K):
        shift3[(s - 1) * CHUNK + r, CHUNK + r - s] = 1.0

    src_lane = np.zeros((HEADS * N_GQ * LANES,), np.int64)
    for h in range(HEADS):
        for q in range(N_GQ):
            src_lane[(h * N_GQ + q) * LANES:(h * N_GQ + q + 1) * LANES] = 16 * q + h
    rows3 = np.arange(3 * LANES) % LANES
    sel = (rows3[:, None] == src_lane[None, :]).astype(np.float32)
    oneh = (np.repeat(16 * GQ_G + np.arange(HEADS), CHUNK)[:, None] == rows3[None, :]).astype(np.float32)

    as_bf = lambda a: jnp.asarray(a, BF16)
    return dict(ltri3=as_bf(ltri3), sq=as_bf(sq), sk=as_bf(sk), cq=jnp.asarray(cq),
                ck=jnp.asarray(ck), shift3=as_bf(shift3), sel=as_bf(sel), oneh=as_bf(oneh))


def _row128(vec, offset):
    return jnp.zeros((1, LANES), F32).at[0, offset:offset + HEADS].set(vec.astype(F32))


def _pick_rows_tile(total, target):
    best = 128
    for t in range(128, target + 1, 128):
        if total % t == 0:
            best = t
    return best


def _layer(x, meta_tokens, mix_norm_g, w_in, conv_w, a_log, dt_bias, gdn_norm_g, w_o_gdn,
           fox_q_norm_g, fox_k_norm_g, fox_f_bias, w_o_fox, w_out, mlp_norm_g, w_up, w_down,
           final_norm_g):
    b, seq, d = x.shape
    lp = CHUNK + seq
    qk = HEADS * HEAD_DIM

    o_z = 3 * qk
    o_b = o_z + qk
    o_a = o_b + HEADS
    o_fq = o_a + HEADS
    o_f = o_fq + 3 * qk
    o_ga = o_f + HEADS
    o_gb = o_ga + d

    w1 = jnp.concatenate([w_in[:, :o_b], w_in[:, o_fq:o_f]], axis=1).astype(BF16)
    ws = jnp.concatenate(
        [w_in[:, o_b:o_fq], w_in[:, o_f:o_ga],
         jnp.zeros((d, LANES - 3 * HEADS), w_in.dtype)], axis=1).astype(BF16)
    wga = w_in[:, o_ga:o_gb].astype(BF16)
    wgb = w_in[:, o_gb:].astype(BF16)

    meta = jnp.broadcast_to(meta_tokens[None].astype(x.dtype), (b, N_META, d))
    h_pad = jnp.concatenate([jnp.zeros((b, FRONT_PAD, d), x.dtype), meta, x], axis=1)

    row = lambda v: v.reshape(1, -1).astype(F32)
    cst = _constants()

    tm_in = _pick_rows_tile(lp, 1408)
    p1, ps = _inproj(h_pad.reshape(b * lp, d), row(mix_norm_g), w1, ws, tm=tm_in, tn=512)
    p1 = p1.reshape(b, lp, P1_COLS)
    ps = ps.reshape(b, lp, LANES)

    yp, eq, ek = _gate_prep(ps, _row128(a_log, SM_A), _row128(dt_bias, SM_A),
                            _row128(fox_f_bias, SM_F), cst["ltri3"], cst["sq"], cst["sk"],
                            cst["cq"], cst["ck"])
    ya = _gdn(p1, yp, cst["sel"], cst["oneh"], conv_w.astype(F32), row(gdn_norm_g),
              cst["shift3"], seq=seq, hg=8)
    yb = _fox(p1, eq, ek, row(fox_q_norm_g), row(fox_k_norm_g), seq=seq, tq=1024)

    x2d = x.reshape(b * seq, d)
    h1 = _merge(x2d, row(mix_norm_g), ya.reshape(b * seq, qk), yb.reshape(b * seq, qk),
                wga, wgb, w_o_gdn.astype(BF16), w_o_fox.astype(BF16), w_out.astype(BF16),
                tm=512, tj=256)
    out = _mlp(h1, row(mlp_norm_g), w_up.astype(BF16), w_down.astype(BF16),
               row(final_norm_g), tm=512, tf=512)
    return out.reshape(b, seq, d)


def kernel(x, meta_tokens, mix_norm_g, w_in, conv_w, a_log, dt_bias, gdn_norm_g, w_o_gdn,
           fox_q_norm_g, fox_k_norm_g, fox_f_bias, w_o_fox, w_out, mlp_norm_g, w_up, w_down,
           final_norm_g):
    assert w_in.shape[0] == 1, "single-layer block"
    return _layer(x, meta_tokens, mix_norm_g[0], w_in[0], conv_w[0], a_log[0], dt_bias[0],
                  gdn_norm_g[0], w_o_gdn[0], fox_q_norm_g[0], fox_k_norm_g[0], fox_f_bias[0],
                  w_o_fox[0], w_out[0], mlp_norm_g[0], w_up[0], w_down[0], final_norm_g)
```

```python
import functools
import math

import numpy as np
import jax
import jax.numpy as jnp
from jax import lax
from jax.experimental import pallas as pl
from jax.experimental.pallas import tpu as pltpu

F32 = jnp.float32
BF16 = jnp.bfloat16

NORM_EPS = 1e-6
N_META = 16
HEADS = 16
HEAD_DIM = 128
CONV_K = 4
LANES = 128
CHUNK = 128
FRONT_PAD = CHUNK - N_META
NEG_BIG = -1e30
LOG2E = math.log2(math.e)
VMEM_LIMIT = 56 * 1024 * 1024

COL_GDN_Q, COL_GDN_K, COL_GDN_V, COL_Z = 0, 16, 32, 48
COL_FOX_Q, COL_FOX_K, COL_FOX_V = 64, 80, 96
P1_COLS = 112 * LANES
SM_B, SM_A, SM_F = 0, 16, 32
GQ_G, GQ_BETA, GQ_EG, GQ_EKD, GQ_BEG = range(5)
N_GQ = 5
FOX_EXTRA = 8
FOX_M_LANE = 6
FOX_SAFE_BOUND = 40.0


def _params(*sem):
    return pltpu.CompilerParams(dimension_semantics=sem, vmem_limit_bytes=VMEM_LIMIT)


def _norm_rows(x, g):
    ms = jnp.mean(x * x, axis=-1, keepdims=True)
    return x * lax.rsqrt(ms + NORM_EPS) * g


def _split3(x):
    hi = x.astype(BF16)
    r = x - hi.astype(F32)
    mid = r.astype(BF16)
    lo = (r - mid.astype(F32)).astype(BF16)
    return hi, mid, lo


def _dot(a, b):
    return jnp.dot(a, b, preferred_element_type=F32)


def _dot_nt(a, b):
    return lax.dot_general(a, b, (((1,), (1,)), ((), ())), preferred_element_type=F32)


def _dot_tn(a, b):
    return lax.dot_general(a, b, (((0,), (0,)), ((), ())), preferred_element_type=F32)


def _sigmoid(x):
    return 1.0 / (1.0 + jnp.exp(-x))


def _silu(x):
    return x * _sigmoid(x)


def _softplus(x):
    return jnp.maximum(x, 0.0) + jnp.log1p(jnp.exp(-jnp.abs(x)))


def _inproj_kernel(x_ref, g_ref, wa_ref, wb_ref, ws_ref, o_ref, os_ref, xn_ref, *, tm, na):
    j = pl.program_id(1)

    @pl.when(j == 0)
    def _():
        def body(r, c):
            rows = pl.ds(pl.multiple_of(r * 128, 128), 128)
            xn_ref[rows, :] = _norm_rows(x_ref[rows, :], g_ref[...]).astype(BF16)
            return c
        lax.fori_loop(0, tm // 128, body, 0)
        os_ref[...] = _dot(xn_ref[...], ws_ref[...])

    @pl.when(j < na)
    def _():
        o_ref[...] = _dot(xn_ref[...], wa_ref[...]).astype(o_ref.dtype)

    @pl.when(j >= na)
    def _():
        o_ref[...] = _dot(xn_ref[...], wb_ref[...]).astype(o_ref.dtype)


def _inproj(h2d, g, wa, wb, ws, *, tm, tn):
    m, d = h2d.shape
    na, nb = wa.shape[1] // tn, wb.shape[1] // tn
    return pl.pallas_call(
        functools.partial(_inproj_kernel, tm=tm, na=na),
        grid=(m // tm, na + nb),
        in_specs=[
            pl.BlockSpec((tm, d), lambda i, j: (i, 0)),
            pl.BlockSpec((1, d), lambda i, j: (0, 0)),
            pl.BlockSpec((d, tn), lambda i, j: (0, jnp.minimum(j, na - 1))),
            pl.BlockSpec((d, tn), lambda i, j: (0, jnp.maximum(j - na, 0))),
            pl.BlockSpec((d, LANES), lambda i, j: (0, 0)),
        ],
        out_specs=[
            pl.BlockSpec((tm, tn), lambda i, j: (i, j)),
            pl.BlockSpec((tm, LANES), lambda i, j: (i, 0)),
        ],
        out_shape=[
            jax.ShapeDtypeStruct((m, (na + nb) * tn), BF16),
            jax.ShapeDtypeStruct((m, LANES), F32),
        ],
        scratch_shapes=[pltpu.VMEM((tm, d), BF16)],
        compiler_params=_params("parallel", "arbitrary"),
        name="inproj",
    )(h2d, g, wa, wb, ws)


def _gate_prep_kernel(ps_ref, alog_ref, dtb_ref, fb_ref, ltri_ref, sq_ref, sk_ref, cq_ref, ck_ref,
                      yp_ref, eq_ref, ek_ref, *, nblk):
    lane = lax.broadcasted_iota(jnp.int32, (CHUNK, LANES), 1)
    rowi = lax.broadcasted_iota(jnp.int32, (CHUNK, LANES), 0)

    def body(i, carry):
        rows = pl.ds(pl.multiple_of(i * CHUNK, CHUNK), CHUNK)
        x = ps_ref[rows, :]
        valid = (i * CHUNK + rowi) >= FRONT_PAD

        beta = jnp.where(valid, _sigmoid(x), 0.0)
        gstep = jnp.where(valid, -jnp.exp(alog_ref[...]) * _softplus(x + dtb_ref[...]), 0.0)
        gcum = _dot(ltri_ref[...], jnp.concatenate(_split3(gstep), axis=0))
        eg = jnp.exp(gcum)
        ekd = jnp.exp(gcum[CHUNK - 1:CHUNK, :] - gcum)
        beg = pltpu.roll(beta, SM_A - SM_B, 1) * eg
        y = jnp.where(lane < 16 * (GQ_EKD + 1), pltpu.roll(ekd, 16 * GQ_EKD - SM_A, 1),
                      pltpu.roll(beg, 16 * GQ_BEG - SM_A, 1))
        y = jnp.where(lane < 16 * (GQ_EG + 1), pltpu.roll(eg, 16 * GQ_EG - SM_A, 1), y)
        y = jnp.where(lane < 16 * (GQ_BETA + 1), pltpu.roll(beta, 16 * GQ_BETA - SM_B, 1), y)
        y = jnp.where(lane < 16 * (GQ_G + 1), pltpu.roll(gcum, LANES + 16 * GQ_G - SM_A, 1), y)
        yp_ref[rows, :] = jnp.concatenate(_split3(y), axis=1)

        xf = x + fb_ref[...]
        ls = (jnp.minimum(xf, 0.0) - jnp.log1p(jnp.exp(-jnp.abs(xf)))) * LOG2E
        cum = _dot(ltri_ref[...], jnp.concatenate(_split3(ls), axis=0)) + carry
        cp = jnp.concatenate(_split3(cum), axis=1)
        eq_ref[rows, :] = (_dot(cp, sq_ref[...]) + cq_ref[...]).astype(BF16)
        ek_ref[rows, :] = (_dot(cp, sk_ref[...]) + ck_ref[...]).astype(BF16)
        return cum[CHUNK - 1:CHUNK, :]

    lax.fori_loop(0, nblk, body, jnp.zeros((1, LANES), F32))


def _gate_prep(ps, alog_row, dtb_row, fb_row, ltri3, sq, sk, cq, ck):
    b, lp, _ = ps.shape
    const = lambda bi: (0, 0)
    return pl.pallas_call(
        functools.partial(_gate_prep_kernel, nblk=lp // CHUNK),
        grid=(b,),
        in_specs=[
            pl.BlockSpec((None, lp, LANES), lambda bi: (bi, 0, 0)),
            pl.BlockSpec((1, LANES), const),
            pl.BlockSpec((1, LANES), const),
            pl.BlockSpec((1, LANES), const),
            pl.BlockSpec((CHUNK, 3 * CHUNK), const),
            pl.BlockSpec((3 * LANES, LANES), const),
            pl.BlockSpec((3 * LANES, LANES), const),
            pl.BlockSpec((1, LANES), const),
            pl.BlockSpec((1, LANES), const),
        ],
        out_specs=[
            pl.BlockSpec((None, lp, 3 * LANES), lambda bi: (bi, 0, 0)),
            pl.BlockSpec((None, lp, LANES), lambda bi: (bi, 0, 0)),
            pl.BlockSpec((None, lp, LANES), lambda bi: (bi, 0, 0)),
        ],
        out_shape=[
            jax.ShapeDtypeStruct((b, lp, 3 * LANES), BF16),
            jax.ShapeDtypeStruct((b, lp, LANES), BF16),
            jax.ShapeDtypeStruct((b, lp, LANES), BF16),
        ],
        compiler_params=_params("parallel"),
        name="gate_prep",
    )(ps, alog_row, dtb_row, fb_row, ltri3, sq, sk, cq, ck)


def _gdn_kernel(q_ref, k_ref, v_ref, z_ref, yp_ref, sel_ref, oneh_ref, wq_ref, wk_ref, wv_ref,
                gn_ref, shift_ref, o_ref, s_ref, prev_ref, *, hg):
    c = pl.program_id(2)

    @pl.when(c == 0)
    def _():
        s_ref[...] = jnp.zeros_like(s_ref)
        prev_ref[...] = jnp.zeros_like(prev_ref)

    row = lax.broadcasted_iota(jnp.int32, (CHUNK, CHUNK), 0)
    col = lax.broadcasted_iota(jnp.int32, (CHUNK, CHUNK), 1)
    tril = row >= col
    strict = row > col
    heads = range(hg)
    hsl = lambda a, u: a[:, u * HEAD_DIM:(u + 1) * HEAD_DIM]

    yp = yp_ref[...]
    gb = _dot(yp, sel_ref[...])
    g_rows = _dot_nt(oneh_ref[...], yp)
    gq = lambda u, q: gb[:, (u * N_GQ + q) * LANES:(u * N_GQ + q + 1) * LANES]

    def conv_act(x_ref, w_ref, t):
        x16 = x_ref[...]
        x2 = jnp.concatenate([prev_ref[t], x16], axis=0)
        prev_ref[t] = x16
        sh = _dot(shift_ref[...], x2)
        w = w_ref[...]
        acc = x16.astype(F32) * w[CONV_K - 1:CONV_K, :]
        for s in range(1, CONV_K):
            acc = acc + sh[(s - 1) * CHUNK:s * CHUNK, :] * w[CONV_K - 1 - s:CONV_K - s, :]
        return _silu(acc)

    qa = conv_act(q_ref, wq_ref, 0)
    ka = conv_act(k_ref, wk_ref, 1)
    va = conv_act(v_ref, wv_ref, 2)

    def l2n(x):
        return x * lax.rsqrt(jnp.sum(x * x, axis=-1, keepdims=True) + NORM_EPS)

    qn = [l2n(hsl(qa, u)) * (HEAD_DIM ** -0.5) for u in heads]
    kn = [l2n(hsl(ka, u)) for u in heads]
    k16 = [kn[u].astype(BF16) for u in heads]

    state = [s_ref[u] for u in heads]
    r1 = [_dot(jnp.concatenate([kn[u] * gq(u, GQ_BEG), qn[u] * gq(u, GQ_EG)], axis=0).astype(BF16),
               state[u].astype(BF16)) for u in heads]

    kq = [_dot_nt(jnp.concatenate([kn[u] * gq(u, GQ_BETA), qn[u]], axis=0).astype(BF16), k16[u])
          for u in heads]
    ediff = [jnp.exp(jnp.minimum(gq(u, GQ_G) - g_rows[u * CHUNK:(u + 1) * CHUNK, :], 0.0))
             for u in heads]
    a = [kq[u][:CHUNK] * jnp.where(strict, ediff[u], 0.0) for u in heads]
    qk_m = [(kq[u][CHUNK:] * jnp.where(tril, ediff[u], 0.0)).astype(BF16) for u in heads]

    a16 = [a[u].astype(BF16) for u in heads]
    yk = [_dot(a16[u], a16[u]) for u in heads]
    sk = [-a[u] for u in heads]
    nlev = int(np.log2(CHUNK)) - 1
    for lev in range(nlev):
        yk16 = [yk[u].astype(BF16) for u in heads]
        if lev < nlev - 1:
            zz = [_dot(jnp.concatenate([sk[u], yk[u]], axis=0).astype(BF16), yk16[u]) for u in heads]
            sk = [sk[u] + yk[u] + zz[u][:CHUNK] for u in heads]
            yk = [zz[u][CHUNK:] for u in heads]
        else:
            sk = [sk[u] + yk[u] + _dot(sk[u].astype(BF16), yk16[u]) for u in heads]

    rhs = [hsl(va, u) * gq(u, GQ_BETA) - r1[u][:CHUNK] for u in heads]
    v_new = [(rhs[u] + _dot(sk[u].astype(BF16), rhs[u].astype(BF16))).astype(BF16) for u in heads]
    o = [r1[u][CHUNK:] + _dot(qk_m[u], v_new[u]) for u in heads]
    for u in heads:
        eg_last = gq(u, GQ_EG)[CHUNK - 1:CHUNK, :]
        s_ref[u] = state[u] * eg_last + _dot_tn((kn[u] * gq(u, GQ_EKD)).astype(BF16), v_new[u])

    on = jnp.concatenate([_norm_rows(o[u], gn_ref[...]) for u in heads], axis=1)
    o_ref[...] = (on * _silu(z_ref[...].astype(F32))).astype(o_ref.dtype)


def _gdn(p1, yp, sel, oneh, conv_w, gnorm, shift3, *, seq, hg):
    b, lp, _ = p1.shape
    nc = lp // CHUNK
    gw = hg * HEAD_DIM
    ng = HEADS // hg
    col = lambda base: (lambda bi, gi, ci: (bi, ci, base * LANES // gw + gi))
    wcol = lambda base: (lambda bi, gi, ci: (0, base * LANES // gw + gi))
    const = lambda bi, gi, ci: (0, 0)
    return pl.pallas_call(
        functools.partial(_gdn_kernel, hg=hg),
        grid=(b, ng, nc),
        in_specs=[
            pl.BlockSpec((None, CHUNK, gw), col(COL_GDN_Q)),
            pl.BlockSpec((None, CHUNK, gw), col(COL_GDN_K)),
            pl.BlockSpec((None, CHUNK, gw), col(COL_GDN_V)),
            pl.BlockSpec((None, CHUNK, gw), col(COL_Z)),
            pl.BlockSpec((None, CHUNK, 3 * LANES), lambda bi, gi, ci: (bi, ci, 0)),
            pl.BlockSpec((3 * LANES, hg * N_GQ * LANES), lambda bi, gi, ci: (0, gi)),
            pl.BlockSpec((hg * CHUNK, 3 * LANES), lambda bi, gi, ci: (gi, 0)),
            pl.BlockSpec((CONV_K, gw), wcol(COL_GDN_Q)),
            pl.BlockSpec((CONV_K, gw), wcol(COL_GDN_K)),
            pl.BlockSpec((CONV_K, gw), wcol(COL_GDN_V)),
            pl.BlockSpec((1, HEAD_DIM), const),
            pl.BlockSpec((3 * CHUNK, 2 * CHUNK), const),
        ],
        out_specs=pl.BlockSpec((None, CHUNK, gw),
                               lambda bi, gi, ci: (bi, jnp.maximum(ci - 1, 0), gi)),
        out_shape=jax.ShapeDtypeStruct((b, seq, HEADS * HEAD_DIM), BF16),
        scratch_shapes=[
            pltpu.VMEM((hg, HEAD_DIM, HEAD_DIM), F32),
            pltpu.VMEM((3, CHUNK, gw), BF16),
        ],
        compiler_params=_params("parallel", "parallel", "arbitrary"),
        name="gdn",
    )(p1, p1, p1, p1, yp, sel, oneh, conv_w, conv_w, conv_w, gnorm, shift3)


def _fox_kernel(q_ref, k_ref, v_ref, eq_ref, ek_ref, gq_ref, gk_ref, o_ref,
                kaug_ref, vaug_ref, *, tq, lp):
    h = pl.program_id(1)
    i = pl.program_id(2)

    @pl.when(i == 0)
    def _():
        def build(rows, n):
            kn = _norm_rows(k_ref[rows, :].astype(F32), gk_ref[...])
            kaug_ref[rows, 0:HEAD_DIM] = kn.astype(BF16)
            kaug_ref[rows, HEAD_DIM:2 * HEAD_DIM] = ek_ref[rows, :]
            vaug_ref[rows, 0:HEAD_DIM] = v_ref[rows, :]
            vaug_ref[rows, HEAD_DIM:2 * HEAD_DIM] = jnp.ones((n, HEAD_DIM), BF16)

        def body(r, c):
            build(pl.ds(pl.multiple_of(r * tq, 128), tq), tq)
            return c
        lax.fori_loop(0, lp // tq, body, 0)
        if lp % tq:
            build(pl.ds(lp - lp % tq, lp % tq), lp % tq)

    q0 = pl.multiple_of(CHUNK + i * tq, 128)
    rows_q = pl.ds(q0, tq)
    qn = _norm_rows(q_ref[rows_q, :].astype(F32), gq_ref[...]) * (HEAD_DIM ** -0.5 * LOG2E)
    lane = lax.broadcasted_iota(jnp.int32, (tq, LANES), 1)
    mine = lax.shift_right_logical(lane, 3) == h
    eqm = jnp.where(mine, eq_ref[rows_q, :], jnp.zeros((), BF16))
    q16 = qn.astype(BF16)

    gmax = (jnp.max(jnp.abs(gq_ref[...]), axis=-1, keepdims=True)
            * jnp.max(jnp.abs(gk_ref[...]), axis=-1, keepdims=True))
    bound = (gmax * (HEAD_DIM ** 0.5 * LOG2E * 1.02)).astype(BF16)
    safe = bound.astype(F32)[0, 0] <= FOX_SAFE_BOUND

    rr = lax.broadcasted_iota(jnp.int32, (tq, tq), 0)
    cc = lax.broadcasted_iota(jnp.int32, (tq, tq), 1)
    causal = rr >= cc

    def blocks(j):
        return pl.ds(pl.multiple_of(CHUNK + j * tq, 128), tq)

    @pl.when(safe)
    def _():
        eqb = jnp.where(lane == FOX_EXTRA * h + FOX_M_LANE, -bound, eqm)
        qaug = jnp.concatenate([q16, eqb], axis=1)
        half = tq // 2

        def probs(qa, krows, mask=None):
            s = _dot_nt(qa, kaug_ref[krows, :])
            if mask is not None:
                s = jnp.where(mask, s, NEG_BIG)
            return jnp.exp2(s).astype(BF16)

        def finish(acc, rows):
            o_ref[rows, :] = (acc[:, :HEAD_DIM] / acc[:, HEAD_DIM:]).astype(o_ref.dtype)

        for ii in range(pl.cdiv(lp - CHUNK, tq)):
            @pl.when(i == ii)
            def _(ii=ii):
                acc = _dot(probs(qaug, slice(0, CHUNK), lane >= FRONT_PAD), vaug_ref[0:CHUNK, :])
                for j in range(ii):
                    kr = slice(CHUNK + j * tq, CHUNK + (j + 1) * tq)
                    acc = acc + _dot(probs(qaug, kr), vaug_ref[kr, :])
                k_lo = slice(CHUNK + ii * tq, CHUNK + ii * tq + half)
                k_all = slice(CHUNK + ii * tq, CHUNK + (ii + 1) * tq)
                p_lo = probs(qaug[:half], k_lo, causal[:half, :half])
                p_hi = probs(qaug[half:], k_all, causal[half:, :])
                finish(acc[:half] + _dot(p_lo, vaug_ref[k_lo, :]), slice(0, half))
                finish(acc[half:] + _dot(p_hi, vaug_ref[k_all, :]), slice(half, tq))

    @pl.when(jnp.logical_not(safe))
    def _():
        qaug = jnp.concatenate([q16, eqm], axis=1)
        s = _dot_nt(qaug, kaug_ref[0:CHUNK, :])
        s = jnp.where(lane >= FRONT_PAD, s, NEG_BIG)
        m = jnp.max(s, axis=-1, keepdims=True)
        acc = _dot(jnp.exp2(s - m).astype(BF16), vaug_ref[0:CHUNK, :])

        def update(s, vrows, m, acc):
            m_new = jnp.maximum(m, jnp.max(s, axis=-1, keepdims=True))
            p = jnp.exp2(s - m_new)
            return m_new, jnp.exp2(m - m_new) * acc + _dot(p.astype(BF16), vrows)

        def body(j, carry):
            s = _dot_nt(qaug, kaug_ref[blocks(j), :])
            return update(s, vaug_ref[blocks(j), :], *carry)
        m, acc = lax.fori_loop(0, i, body, (m, acc))

        s = _dot_nt(qaug, kaug_ref[rows_q, :])
        m, acc = update(jnp.where(causal, s, NEG_BIG), vaug_ref[rows_q, :], m, acc)
        o_ref[...] = (acc[:, :HEAD_DIM] / acc[:, HEAD_DIM:]).astype(o_ref.dtype)


def _fox(p1, eq, ek, gq, gk, *, seq, tq):
    b, lp, _ = p1.shape
    colmap = lambda base: (lambda bi, hi, qi: (bi, 0, base + hi))
    const = lambda bi, hi, qi: (0, 0)
    return pl.pallas_call(
        functools.partial(_fox_kernel, tq=tq, lp=lp),
        grid=(b, HEADS, seq // tq),
        in_specs=[
            pl.BlockSpec((None, lp, HEAD_DIM), colmap(COL_FOX_Q)),
            pl.BlockSpec((None, lp, HEAD_DIM), colmap(COL_FOX_K)),
            pl.BlockSpec((None, lp, HEAD_DIM), colmap(COL_FOX_V)),
            pl.BlockSpec((None, lp, LANES), lambda bi, hi, qi: (bi, 0, 0)),
            pl.BlockSpec((None, lp, LANES), lambda bi, hi, qi: (bi, 0, 0)),
            pl.BlockSpec((1, HEAD_DIM), const),
            pl.BlockSpec((1, HEAD_DIM), const),
        ],
        out_specs=pl.BlockSpec((None, tq, HEAD_DIM), lambda bi, hi, qi: (bi, qi, hi)),
        out_shape=jax.ShapeDtypeStruct((b, seq, HEADS * HEAD_DIM), BF16),
        scratch_shapes=[
            pltpu.VMEM((lp, 2 * HEAD_DIM), BF16),
            pltpu.VMEM((lp, 2 * HEAD_DIM), BF16),
        ],
        compiler_params=_params("parallel", "parallel", "arbitrary"),
        name="fox",
    )(p1, p1, p1, eq, ek, gq, gk)


def _merge_kernel(x_ref, g_ref, ya_ref, yb_ref, wga_ref, wgb_ref, wa_ref, wb_ref, wo_ref,
                  o_ref, u_ref):
    j = pl.program_id(1)

    @pl.when(j == 0)
    def _():
        x = x_ref[...]
        u_ref[...] = _norm_rows(x, g_ref[...]).astype(BF16)
        o_ref[...] = x

    u = u_ref[...]
    ga = _dot(u, wga_ref[...])
    gb = _dot(u, wgb_ref[...])
    pa = _dot(ya_ref[...], wa_ref[...])
    pb = _dot(yb_ref[...], wb_ref[...])
    mix = _sigmoid(ga) * pa + _sigmoid(gb) * pb
    o_ref[...] += _dot(mix.astype(BF16), wo_ref[...])


def _merge(x2d, g, ya, yb, wga, wgb, wa, wb, wo, *, tm, tj):
    m, d = x2d.shape
    rowblk = lambda i, j: (i, 0)
    colblk = lambda i, j: (0, j)
    return pl.pallas_call(
        _merge_kernel,
        grid=(m // tm, d // tj),
        in_specs=[
            pl.BlockSpec((tm, d), rowblk),
            pl.BlockSpec((1, d), lambda i, j: (0, 0)),
            pl.BlockSpec((tm, d), rowblk),
            pl.BlockSpec((tm, d), rowblk),
            pl.BlockSpec((d, tj), colblk),
            pl.BlockSpec((d, tj), colblk),
            pl.BlockSpec((d, tj), colblk),
            pl.BlockSpec((d, tj), colblk),
            pl.BlockSpec((tj, d), lambda i, j: (j, 0)),
        ],
        out_specs=pl.BlockSpec((tm, d), rowblk),
        out_shape=jax.ShapeDtypeStruct((m, d), F32),
        scratch_shapes=[pltpu.VMEM((tm, d), BF16)],
        compiler_params=_params("parallel", "arbitrary"),
        name="merge",
    )(x2d, g, ya, yb, wga, wgb, wa, wb, wo)


def _mlp_kernel(h_ref, g_ref, wu_ref, wd_ref, gf_ref, o_ref, u_ref):
    f = pl.program_id(1)

    @pl.when(f == 0)
    def _():
        x = h_ref[...]
        u_ref[...] = _norm_rows(x, g_ref[...]).astype(BF16)
        o_ref[...] = x

    a = jnp.maximum(_dot(u_ref[...], wu_ref[...]), 0.0)
    o_ref[...] += _dot((a * a).astype(BF16), wd_ref[...])

    @pl.when(f == pl.num_programs(1) - 1)
    def _():
        o_ref[...] = _norm_rows(o_ref[...], gf_ref[...])


def _mlp(h2d, g, wu, wd, gf, *, tm, tf):
    m, d = h2d.shape
    dff = wu.shape[1]
    rowblk = lambda i, f: (i, 0)
    return pl.pallas_call(
        _mlp_kernel,
        grid=(m // tm, dff // tf),
        in_specs=[
            pl.BlockSpec((tm, d), rowblk),
            pl.BlockSpec((1, d), lambda i, f: (0, 0)),
            pl.BlockSpec((d, tf), lambda i, f: (0, f)),
            pl.BlockSpec((tf, d), lambda i, f: (f, 0)),
            pl.BlockSpec((1, d), lambda i, f: (0, 0)),
        ],
        out_specs=pl.BlockSpec((tm, d), rowblk),
        out_shape=jax.ShapeDtypeStruct((m, d), F32),
        scratch_shapes=[pltpu.VMEM((tm, d), BF16)],
        compiler_params=_params("parallel", "arbitrary"),
        name="mlp",
    )(h2d, g, wu, wd, gf)


def _constants():
    r = np.arange(128)
    ltri = (r[:, None] >= r[None, :]).astype(np.float32)
    ltri3 = np.concatenate([ltri, ltri, ltri], axis=1)

    sq = np.zeros((3 * LANES, LANES), np.float32)
    sk = np.zeros((3 * LANES, LANES), np.float32)
    cq = np.zeros((1, LANES), np.float32)
    ck = np.zeros((1, LANES), np.float32)
    for h in range(HEADS):
        for p in range(3):
            sq[p * LANES + SM_F + h, FOX_EXTRA * h + p] = 1.0
            sk[p * LANES + SM_F + h, FOX_EXTRA * h + 3 + p] = -1.0
            cq[0, FOX_EXTRA * h + 3 + p] = 1.0
            ck[0, FOX_EXTRA * h + p] = 1.0
        ck[0, FOX_EXTRA * h + FOX_M_LANE] = 1.0

    shift3 = np.zeros((3 * CHUNK, 2 * CHUNK), np.float32)
    for s in range(1, CONV_K):
        shift3[(s - 1) * CHUNK + r, CHUNK + r - s] = 1.0

    src_lane = np.zeros((HEADS * N_GQ * LANES,), np.int64)
    for h in range(HEADS):
        for q in range(N_GQ):
            src_lane[(h * N_GQ + q) * LANES:(h * N_GQ + q + 1) * LANES] = 16 * q + h
    rows3 = np.arange(3 * LANES) % LANES
    sel = (rows3[:, None] == src_lane[None, :]).astype(np.float32)
    oneh = (np.repeat(16 * GQ_G + np.arange(HEADS), CHUNK)[:, None] == rows3[None, :]).astype(np.float32)

    as_bf = lambda a: jnp.asarray(a, BF16)
    return dict(ltri3=as_bf(ltri3), sq=as_bf(sq), sk=as_bf(sk), cq=jnp.asarray(cq),
                ck=jnp.asarray(ck), shift3=as_bf(shift3), sel=as_bf(sel), oneh=as_bf(oneh))


def _row128(vec, offset):
    return jnp.zeros((1, LANES), F32).at[0, offset:offset + HEADS].set(vec.astype(F32))


def _pick_rows_tile(total, target):
    best = 128
    for t in range(128, target + 1, 128):
        if total % t == 0:
            best = t
    return best


def _layer(x, meta_tokens, mix_norm_g, w_in, conv_w, a_log, dt_bias, gdn_norm_g, w_o_gdn,
           fox_q_norm_g, fox_k_norm_g, fox_f_bias, w_o_fox, w_out, mlp_norm_g, w_up, w_down,
           final_norm_g):
    b, seq, d = x.shape
    lp = CHUNK + seq
    qk = HEADS * HEAD_DIM

    o_z = 3 * qk
    o_b = o_z + qk
    o_a = o_b + HEADS
    o_fq = o_a + HEADS
    o_f = o_fq + 3 * qk
    o_ga = o_f + HEADS
    o_gb = o_ga + d

    w_gdn = w_in[:, :o_b].astype(BF16)
    w_fox = w_in[:, o_fq:o_f].astype(BF16)
    ws = jnp.concatenate(
        [w_in[:, o_b:o_fq], w_in[:, o_f:o_ga],
         jnp.zeros((d, LANES - 3 * HEADS), w_in.dtype)], axis=1).astype(BF16)
    wga = w_in[:, o_ga:o_gb].astype(BF16)
    wgb = w_in[:, o_gb:].astype(BF16)

    meta = jnp.broadcast_to(meta_tokens[None].astype(x.dtype), (b, N_META, d))
    h_pad = jnp.concatenate([jnp.zeros((b, FRONT_PAD, d), x.dtype), meta, x], axis=1)

    row = lambda v: v.reshape(1, -1).astype(F32)
    cst = _constants()

    tm_in = _pick_rows_tile(lp, 1408)
    p1, ps = _inproj(h_pad.reshape(b * lp, d), row(mix_norm_g), w_gdn, w_fox, ws, tm=tm_in,
                     tn=512)
    p1 = p1.reshape(b, lp, P1_COLS)
    ps = ps.reshape(b, lp, LANES)

    yp, eq, ek = _gate_prep(ps, _row128(a_log, SM_A), _row128(dt_bias, SM_A),
                            _row128(fox_f_bias, SM_F), cst["ltri3"], cst["sq"], cst["sk"],
                            cst["cq"], cst["ck"])
    ya = _gdn(p1, yp, cst["sel"], cst["oneh"], conv_w.astype(F32), row(gdn_norm_g),
              cst["shift3"], seq=seq, hg=8)
    yb = _fox(p1, eq, ek, row(fox_q_norm_g), row(fox_k_norm_g), seq=seq, tq=1024)

    x2d = x.reshape(b * seq, d)
    h1 = _merge(x2d, row(mix_norm_g), ya.reshape(b * seq, qk), yb.reshape(b * seq, qk),
                wga, wgb, w_o_gdn.astype(BF16), w_o_fox.astype(BF16), w_out.astype(BF16),
                tm=512, tj=512)
    out = _mlp(h1, row(mlp_norm_g), w_up.astype(BF16), w_down.astype(BF16),
               row(final_norm_g), tm=512, tf=1024)
    return out.reshape(b, seq, d)


def kernel(x, meta_tokens, mix_norm_g, w_in, conv_w, a_log, dt_bias, gdn_norm_g, w_o_gdn,
           fox_q_norm_g, fox_k_norm_g, fox_f_bias, w_o_fox, w_out, mlp_norm_g, w_up, w_down,
           final_norm_g):
    assert w_in.shape[0] == 1, "single-layer block"
    return _layer(x, meta_tokens, mix_norm_g[0], w_in[0], conv_w[0], a_log[0], dt_bias[0],
                  gdn_norm_g[0], w_o_gdn[0], fox_q_norm_g[0], fox_k_norm_g[0], fox_f_bias[0],
                  w_o_fox[0], w_out[0], mlp_norm_g[0], w_up[0], w_down[0], final_norm_g)
```

```python
import functools
import math

import numpy as np
import jax
import jax.numpy as jnp
from jax import lax
from jax.experimental import pallas as pl
from jax.experimental.pallas import tpu as pltpu

F32 = jnp.float32
BF16 = jnp.bfloat16

NORM_EPS = 1e-6
N_META = 16
HEADS = 16
HEAD_DIM = 128
CONV_K = 4
LANES = 128
CHUNK = 128
FRONT_PAD = CHUNK - N_META
NEG_BIG = -1e30
LOG2E = math.log2(math.e)
VMEM_LIMIT = 56 * 1024 * 1024

COL_GDN_Q, COL_GDN_K, COL_GDN_V, COL_Z = 0, 16, 32, 48
COL_FOX_Q, COL_FOX_K, COL_FOX_V = 64, 80, 96
P1_COLS = 112 * LANES
SM_B, SM_A, SM_F = 0, 16, 32
GQ_G, GQ_BETA, GQ_EG, GQ_EKD, GQ_BEG = range(5)
N_GQ = 5
FOX_EXTRA = 8
FOX_M_LANE = 6
FOX_SAFE_BOUND = 40.0


def _params(*sem):
    return pltpu.CompilerParams(dimension_semantics=sem, vmem_limit_bytes=VMEM_LIMIT)


def _norm_rows(x, g):
    ms = jnp.mean(x * x, axis=-1, keepdims=True)
    return x * lax.rsqrt(ms + NORM_EPS) * g


def _split3(x):
    hi = x.astype(BF16)
    r = x - hi.astype(F32)
    mid = r.astype(BF16)
    lo = (r - mid.astype(F32)).astype(BF16)
    return hi, mid, lo


def _dot(a, b):
    return jnp.dot(a, b, preferred_element_type=F32)


def _dot_nt(a, b):
    return lax.dot_general(a, b, (((1,), (1,)), ((), ())), preferred_element_type=F32)


def _dot_tn(a, b):
    return lax.dot_general(a, b, (((0,), (0,)), ((), ())), preferred_element_type=F32)


def _sigmoid(x):
    return 1.0 / (1.0 + jnp.exp(-x))


def _silu(x):
    return x * _sigmoid(x)


def _softplus(x):
    return jnp.maximum(x, 0.0) + jnp.log1p(jnp.exp(-jnp.abs(x)))


def _prenorm_kernel(x_ref, meta_ref, g_ref, o_ref):
    r = pl.program_id(1)

    @pl.when(r == 0)
    def _():
        o_ref[0:FRONT_PAD, :] = jnp.zeros((FRONT_PAD, o_ref.shape[1]), o_ref.dtype)
        o_ref[FRONT_PAD:CHUNK, :] = _norm_rows(meta_ref[...], g_ref[...]).astype(o_ref.dtype)

    @pl.when(r > 0)
    def _():
        o_ref[...] = _norm_rows(x_ref[...], g_ref[...]).astype(o_ref.dtype)


def _prenorm(x, meta_tokens, g):
    b, seq, d = x.shape
    lp = CHUNK + seq
    return pl.pallas_call(
        _prenorm_kernel,
        grid=(b, lp // CHUNK),
        in_specs=[
            pl.BlockSpec((None, CHUNK, d), lambda bi, r: (bi, jnp.maximum(r - 1, 0), 0)),
            pl.BlockSpec((N_META, d), lambda bi, r: (0, 0)),
            pl.BlockSpec((1, d), lambda bi, r: (0, 0)),
        ],
        out_specs=pl.BlockSpec((None, CHUNK, d), lambda bi, r: (bi, r, 0)),
        out_shape=jax.ShapeDtypeStruct((b, lp, d), BF16),
        compiler_params=_params("parallel", "arbitrary"),
        name="prenorm",
    )(x, meta_tokens, g)


def _inproj_kernel(x_ref, wa_ref, wb_ref, ws_ref, o_ref, os_ref, *, na):
    j = pl.program_id(1)

    @pl.when(j == 0)
    def _():
        os_ref[...] = _dot(x_ref[...], ws_ref[...])

    @pl.when(j < na)
    def _():
        o_ref[...] = _dot(x_ref[...], wa_ref[...].astype(BF16)).astype(o_ref.dtype)

    @pl.when(j >= na)
    def _():
        o_ref[...] = _dot(x_ref[...], wb_ref[...]).astype(o_ref.dtype)


def _inproj(hn, w_all, n_a, wb, ws, *, tm, tn):
    m, d = hn.shape
    na, nb = n_a // tn, wb.shape[1] // tn
    return pl.pallas_call(
        functools.partial(_inproj_kernel, na=na),
        grid=(m // tm, na + nb),
        in_specs=[
            pl.BlockSpec((tm, d), lambda i, j: (i, 0)),
            pl.BlockSpec((d, tn), lambda i, j: (0, jnp.minimum(j, na - 1))),
            pl.BlockSpec((d, tn), lambda i, j: (0, jnp.maximum(j - na, 0))),
            pl.BlockSpec((d, LANES), lambda i, j: (0, 0)),
        ],
        out_specs=[
            pl.BlockSpec((tm, tn), lambda i, j: (i, j)),
            pl.BlockSpec((tm, LANES), lambda i, j: (i, 0)),
        ],
        out_shape=[
            jax.ShapeDtypeStruct((m, (na + nb) * tn), BF16),
            jax.ShapeDtypeStruct((m, LANES), F32),
        ],
        compiler_params=_params("parallel", "arbitrary"),
        name="inproj",
    )(hn, w_all, wb, ws)


def _gate_prep_kernel(ps_ref, alog_ref, dtb_ref, fb_ref, ltri_ref, sq_ref, sk_ref, cq_ref, ck_ref,
                      yp_ref, eq_ref, ek_ref, *, nblk):
    lane = lax.broadcasted_iota(jnp.int32, (CHUNK, LANES), 1)
    rowi = lax.broadcasted_iota(jnp.int32, (CHUNK, LANES), 0)

    def body(i, carry):
        rows = pl.ds(pl.multiple_of(i * CHUNK, CHUNK), CHUNK)
        x = ps_ref[rows, :]
        valid = (i * CHUNK + rowi) >= FRONT_PAD

        beta = jnp.where(valid, _sigmoid(x), 0.0)
        gstep = jnp.where(valid, -jnp.exp(alog_ref[...]) * _softplus(x + dtb_ref[...]), 0.0)
        gcum = _dot(ltri_ref[...], jnp.concatenate(_split3(gstep), axis=0))
        eg = jnp.exp(gcum)
        ekd = jnp.exp(gcum[CHUNK - 1:CHUNK, :] - gcum)
        beg = pltpu.roll(beta, SM_A - SM_B, 1) * eg
        y = jnp.where(lane < 16 * (GQ_EKD + 1), pltpu.roll(ekd, 16 * GQ_EKD - SM_A, 1),
                      pltpu.roll(beg, 16 * GQ_BEG - SM_A, 1))
        y = jnp.where(lane < 16 * (GQ_EG + 1), pltpu.roll(eg, 16 * GQ_EG - SM_A, 1), y)
        y = jnp.where(lane < 16 * (GQ_BETA + 1), pltpu.roll(beta, 16 * GQ_BETA - SM_B, 1), y)
        y = jnp.where(lane < 16 * (GQ_G + 1), pltpu.roll(gcum, LANES + 16 * GQ_G - SM_A, 1), y)
        yp_ref[rows, :] = jnp.concatenate(_split3(y), axis=1)

        xf = x + fb_ref[...]
        ls = (jnp.minimum(xf, 0.0) - jnp.log1p(jnp.exp(-jnp.abs(xf)))) * LOG2E
        cum = _dot(ltri_ref[...], jnp.concatenate(_split3(ls), axis=0)) + carry
        cp = jnp.concatenate(_split3(cum), axis=1)
        eq_ref[rows, :] = (_dot(cp, sq_ref[...]) + cq_ref[...]).astype(BF16)
        ek_ref[rows, :] = (_dot(cp, sk_ref[...]) + ck_ref[...]).astype(BF16)
        return cum[CHUNK - 1:CHUNK, :]

    lax.fori_loop(0, nblk, body, jnp.zeros((1, LANES), F32))


def _gate_prep(ps, alog_row, dtb_row, fb_row, ltri3, sq, sk, cq, ck):
    b, lp, _ = ps.shape
    const = lambda bi: (0, 0)
    return pl.pallas_call(
        functools.partial(_gate_prep_kernel, nblk=lp // CHUNK),
        grid=(b,),
        in_specs=[
            pl.BlockSpec((None, lp, LANES), lambda bi: (bi, 0, 0)),
            pl.BlockSpec((1, LANES), const),
            pl.BlockSpec((1, LANES), const),
            pl.BlockSpec((1, LANES), const),
            pl.BlockSpec((CHUNK, 3 * CHUNK), const),
            pl.BlockSpec((3 * LANES, LANES), const),
            pl.BlockSpec((3 * LANES, LANES), const),
            pl.BlockSpec((1, LANES), const),
            pl.BlockSpec((1, LANES), const),
        ],
        out_specs=[
            pl.BlockSpec((None, lp, 3 * LANES), lambda bi: (bi, 0, 0)),
            pl.BlockSpec((None, lp, LANES), lambda bi: (bi, 0, 0)),
            pl.BlockSpec((None, lp, LANES), lambda bi: (bi, 0, 0)),
        ],
        out_shape=[
            jax.ShapeDtypeStruct((b, lp, 3 * LANES), BF16),
            jax.ShapeDtypeStruct((b, lp, LANES), BF16),
            jax.ShapeDtypeStruct((b, lp, LANES), BF16),
        ],
        compiler_params=_params("parallel"),
        name="gate_prep",
    )(ps, alog_row, dtb_row, fb_row, ltri3, sq, sk, cq, ck)


def _gdn_kernel(q_ref, k_ref, v_ref, z_ref, yp_ref, sel_ref, oneh_ref, wq_ref, wk_ref, wv_ref,
                gn_ref, shift_ref, o_ref, s_ref, prev_ref, *, hg):
    c = pl.program_id(2)

    @pl.when(c == 0)
    def _():
        s_ref[...] = jnp.zeros_like(s_ref)
        prev_ref[...] = jnp.zeros_like(prev_ref)

    row = lax.broadcasted_iota(jnp.int32, (CHUNK, CHUNK), 0)
    col = lax.broadcasted_iota(jnp.int32, (CHUNK, CHUNK), 1)
    tril = row >= col
    strict = row > col
    heads = range(hg)
    hsl = lambda a, u: a[:, u * HEAD_DIM:(u + 1) * HEAD_DIM]

    yp = yp_ref[...]
    gb = _dot(yp, sel_ref[...])
    g_rows = _dot_nt(oneh_ref[...], yp)
    gq = lambda u, q: gb[:, (u * N_GQ + q) * LANES:(u * N_GQ + q + 1) * LANES]

    def conv_act(x_ref, w_ref, t):
        x16 = x_ref[...]
        x2 = jnp.concatenate([prev_ref[t], x16], axis=0)
        prev_ref[t] = x16
        sh = _dot(shift_ref[...], x2)
        w = w_ref[...]
        acc = x16.astype(F32) * w[CONV_K - 1:CONV_K, :]
        for s in range(1, CONV_K):
            acc = acc + sh[(s - 1) * CHUNK:s * CHUNK, :] * w[CONV_K - 1 - s:CONV_K - s, :]
        return _silu(acc)

    qa = conv_act(q_ref, wq_ref, 0)
    ka = conv_act(k_ref, wk_ref, 1)
    va = conv_act(v_ref, wv_ref, 2)

    def l2n(x):
        return x * lax.rsqrt(jnp.sum(x * x, axis=-1, keepdims=True) + NORM_EPS)

    qn = [l2n(hsl(qa, u)) * (HEAD_DIM ** -0.5) for u in heads]
    kn = [l2n(hsl(ka, u)) for u in heads]
    k16 = [kn[u].astype(BF16) for u in heads]

    state = [s_ref[u] for u in heads]
    r1 = [_dot(jnp.concatenate([kn[u] * gq(u, GQ_BEG), qn[u] * gq(u, GQ_EG)], axis=0).astype(BF16),
               state[u].astype(BF16)) for u in heads]

    kq = [_dot_nt(jnp.concatenate([kn[u] * gq(u, GQ_BETA), qn[u]], axis=0).astype(BF16), k16[u])
          for u in heads]
    ediff = [jnp.exp(jnp.minimum(gq(u, GQ_G) - g_rows[u * CHUNK:(u + 1) * CHUNK, :], 0.0))
             for u in heads]
    a = [kq[u][:CHUNK] * jnp.where(strict, ediff[u], 0.0) for u in heads]
    qk_m = [(kq[u][CHUNK:] * jnp.where(tril, ediff[u], 0.0)).astype(BF16) for u in heads]

    a16 = [a[u].astype(BF16) for u in heads]
    yk = [_dot(a16[u], a16[u]) for u in heads]
    sk = [-a[u] for u in heads]
    nlev = int(np.log2(CHUNK)) - 1
    for lev in range(nlev):
        yk16 = [yk[u].astype(BF16) for u in heads]
        if lev < nlev - 1:
            zz = [_dot(jnp.concatenate([sk[u], yk[u]], axis=0).astype(BF16), yk16[u]) for u in heads]
            sk = [sk[u] + yk[u] + zz[u][:CHUNK] for u in heads]
            yk = [zz[u][CHUNK:] for u in heads]
        else:
            sk = [sk[u] + yk[u] + _dot(sk[u].astype(BF16), yk16[u]) for u in heads]

    rhs = [hsl(va, u) * gq(u, GQ_BETA) - r1[u][:CHUNK] for u in heads]
    v_new = [(rhs[u] + _dot(sk[u].astype(BF16), rhs[u].astype(BF16))).astype(BF16) for u in heads]
    o = [r1[u][CHUNK:] + _dot(qk_m[u], v_new[u]) for u in heads]
    for u in heads:
        eg_last = gq(u, GQ_EG)[CHUNK - 1:CHUNK, :]
        s_ref[u] = state[u] * eg_last + _dot_tn((kn[u] * gq(u, GQ_EKD)).astype(BF16), v_new[u])

    on = jnp.concatenate([_norm_rows(o[u], gn_ref[...]) for u in heads], axis=1)
    o_ref[...] = (on * _silu(z_ref[...].astype(F32))).astype(o_ref.dtype)


def _gdn(p1, yp, sel, oneh, conv_w, gnorm, shift3, *, seq, hg):
    b, lp, _ = p1.shape
    nc = lp // CHUNK
    gw = hg * HEAD_DIM
    ng = HEADS // hg
    col = lambda base: (lambda bi, gi, ci: (bi, ci, base * LANES // gw + gi))
    wcol = lambda base: (lambda bi, gi, ci: (0, base * LANES // gw + gi))
    const = lambda bi, gi, ci: (0, 0)
    return pl.pallas_call(
        functools.partial(_gdn_kernel, hg=hg),
        grid=(b, ng, nc),
        in_specs=[
            pl.BlockSpec((None, CHUNK, gw), col(COL_GDN_Q)),
            pl.BlockSpec((None, CHUNK, gw), col(COL_GDN_K)),
            pl.BlockSpec((None, CHUNK, gw), col(COL_GDN_V)),
            pl.BlockSpec((None, CHUNK, gw), col(COL_Z)),
            pl.BlockSpec((None, CHUNK, 3 * LANES), lambda bi, gi, ci: (bi, ci, 0)),
            pl.BlockSpec((3 * LANES, hg * N_GQ * LANES), lambda bi, gi, ci: (0, gi)),
            pl.BlockSpec((hg * CHUNK, 3 * LANES), lambda bi, gi, ci: (gi, 0)),
            pl.BlockSpec((CONV_K, gw), wcol(COL_GDN_Q)),
            pl.BlockSpec((CONV_K, gw), wcol(COL_GDN_K)),
            pl.BlockSpec((CONV_K, gw), wcol(COL_GDN_V)),
            pl.BlockSpec((1, HEAD_DIM), const),
            pl.BlockSpec((3 * CHUNK, 2 * CHUNK), const),
        ],
        out_specs=pl.BlockSpec((None, CHUNK, gw),
                               lambda bi, gi, ci: (bi, jnp.maximum(ci - 1, 0), gi)),
        out_shape=jax.ShapeDtypeStruct((b, seq, HEADS * HEAD_DIM), BF16),
        scratch_shapes=[
            pltpu.VMEM((hg, HEAD_DIM, HEAD_DIM), F32),
            pltpu.VMEM((3, CHUNK, gw), BF16),
        ],
        compiler_params=_params("parallel", "parallel", "arbitrary"),
        name="gdn",
    )(p1, p1, p1, p1, yp, sel, oneh, conv_w, conv_w, conv_w, gnorm, shift3)


def _fox_kernel(q_ref, k_ref, v_ref, eq_ref, ek_ref, gq_ref, gk_ref, o_ref,
                kaug_ref, vaug_ref, *, tq, lp):
    h = pl.program_id(1)
    i = pl.program_id(2)

    @pl.when(i == 0)
    def _():
        def build(rows, n):
            kn = _norm_rows(k_ref[rows, :].astype(F32), gk_ref[...])
            kaug_ref[rows, 0:HEAD_DIM] = kn.astype(BF16)
            kaug_ref[rows, HEAD_DIM:2 * HEAD_DIM] = ek_ref[rows, :]
            vaug_ref[rows, 0:HEAD_DIM] = v_ref[rows, :]
            vaug_ref[rows, HEAD_DIM:2 * HEAD_DIM] = jnp.ones((n, HEAD_DIM), BF16)

        def body(r, c):
            build(pl.ds(pl.multiple_of(r * tq, 128), tq), tq)
            return c
        lax.fori_loop(0, lp // tq, body, 0)
        if lp % tq:
            build(pl.ds(lp - lp % tq, lp % tq), lp % tq)

    q0 = pl.multiple_of(CHUNK + i * tq, 128)
    rows_q = pl.ds(q0, tq)
    qn = _norm_rows(q_ref[rows_q, :].astype(F32), gq_ref[...]) * (HEAD_DIM ** -0.5 * LOG2E)
    lane = lax.broadcasted_iota(jnp.int32, (tq, LANES), 1)
    mine = lax.shift_right_logical(lane, 3) == h
    eqm = jnp.where(mine, eq_ref[rows_q, :], jnp.zeros((), BF16))
    q16 = qn.astype(BF16)

    gmax = (jnp.max(jnp.abs(gq_ref[...]), axis=-1, keepdims=True)
            * jnp.max(jnp.abs(gk_ref[...]), axis=-1, keepdims=True))
    bound = (gmax * (HEAD_DIM ** 0.5 * LOG2E * 1.02)).astype(BF16)
    safe = bound.astype(F32)[0, 0] <= FOX_SAFE_BOUND

    rr = lax.broadcasted_iota(jnp.int32, (tq, tq), 0)
    cc = lax.broadcasted_iota(jnp.int32, (tq, tq), 1)
    causal = rr >= cc

    def blocks(j):
        return pl.ds(pl.multiple_of(CHUNK + j * tq, 128), tq)

    @pl.when(safe)
    def _():
        eqb = jnp.where(lane == FOX_EXTRA * h + FOX_M_LANE, -bound, eqm)
        qaug = jnp.concatenate([q16, eqb], axis=1)
        half = tq // 2

        def probs(qa, krows, mask=None):
            s = _dot_nt(qa, kaug_ref[krows, :])
            if mask is not None:
                s = jnp.where(mask, s, NEG_BIG)
            return jnp.exp2(s).astype(BF16)

        def finish(acc, rows):
            o_ref[rows, :] = (acc[:, :HEAD_DIM] / acc[:, HEAD_DIM:]).astype(o_ref.dtype)

        for ii in range(pl.cdiv(lp - CHUNK, tq)):
            @pl.when(i == ii)
            def _(ii=ii):
                acc = _dot(probs(qaug, slice(0, CHUNK), lane >= FRONT_PAD), vaug_ref[0:CHUNK, :])
                for j in range(ii):
                    kr = slice(CHUNK + j * tq, CHUNK + (j + 1) * tq)
                    acc = acc + _dot(probs(qaug, kr), vaug_ref[kr, :])
                k_lo = slice(CHUNK + ii * tq, CHUNK + ii * tq + half)
                k_all = slice(CHUNK + ii * tq, CHUNK + (ii + 1) * tq)
                p_lo = probs(qaug[:half], k_lo, causal[:half, :half])
                p_hi = probs(qaug[half:], k_all, causal[half:, :])
                finish(acc[:half] + _dot(p_lo, vaug_ref[k_lo, :]), slice(0, half))
                finish(acc[half:] + _dot(p_hi, vaug_ref[k_all, :]), slice(half, tq))

    @pl.when(jnp.logical_not(safe))
    def _():
        qaug = jnp.concatenate([q16, eqm], axis=1)
        s = _dot_nt(qaug, kaug_ref[0:CHUNK, :])
        s = jnp.where(lane >= FRONT_PAD, s, NEG_BIG)
        m = jnp.max(s, axis=-1, keepdims=True)
        acc = _dot(jnp.exp2(s - m).astype(BF16), vaug_ref[0:CHUNK, :])

        def update(s, vrows, m, acc):
            m_new = jnp.maximum(m, jnp.max(s, axis=-1, keepdims=True))
            p = jnp.exp2(s - m_new)
            return m_new, jnp.exp2(m - m_new) * acc + _dot(p.astype(BF16), vrows)

        def body(j, carry):
            s = _dot_nt(qaug, kaug_ref[blocks(j), :])
            return update(s, vaug_ref[blocks(j), :], *carry)
        m, acc = lax.fori_loop(0, i, body, (m, acc))

        s = _dot_nt(qaug, kaug_ref[rows_q, :])
        m, acc = update(jnp.where(causal, s, NEG_BIG), vaug_ref[rows_q, :], m, acc)
        o_ref[...] = (acc[:, :HEAD_DIM] / acc[:, HEAD_DIM:]).astype(o_ref.dtype)


def _fox(p1, eq, ek, gq, gk, *, seq, tq):
    b, lp, _ = p1.shape
    colmap = lambda base: (lambda bi, hi, qi: (bi, 0, base + hi))
    const = lambda bi, hi, qi: (0, 0)
    return pl.pallas_call(
        functools.partial(_fox_kernel, tq=tq, lp=lp),
        grid=(b, HEADS, seq // tq),
        in_specs=[
            pl.BlockSpec((None, lp, HEAD_DIM), colmap(COL_FOX_Q)),
            pl.BlockSpec((None, lp, HEAD_DIM), colmap(COL_FOX_K)),
            pl.BlockSpec((None, lp, HEAD_DIM), colmap(COL_FOX_V)),
            pl.BlockSpec((None, lp, LANES), lambda bi, hi, qi: (bi, 0, 0)),
            pl.BlockSpec((None, lp, LANES), lambda bi, hi, qi: (bi, 0, 0)),
            pl.BlockSpec((1, HEAD_DIM), const),
            pl.BlockSpec((1, HEAD_DIM), const),
        ],
        out_specs=pl.BlockSpec((None, tq, HEAD_DIM), lambda bi, hi, qi: (bi, qi, hi)),
        out_shape=jax.ShapeDtypeStruct((b, seq, HEADS * HEAD_DIM), BF16),
        scratch_shapes=[
            pltpu.VMEM((lp, 2 * HEAD_DIM), BF16),
            pltpu.VMEM((lp, 2 * HEAD_DIM), BF16),
        ],
        compiler_params=_params("parallel", "parallel", "arbitrary"),
        name="fox",
    )(p1, p1, p1, eq, ek, gq, gk)


def _merge_kernel(x_ref, g_ref, ya_ref, yb_ref, wga_ref, wgb_ref, wa_ref, wb_ref, wo_ref,
                  o_ref, u_ref):
    j = pl.program_id(1)

    @pl.when(j == 0)
    def _():
        x = x_ref[...]
        u_ref[...] = _norm_rows(x, g_ref[...]).astype(BF16)
        o_ref[...] = x

    u = u_ref[...]
    ga = _dot(u, wga_ref[...])
    gb = _dot(u, wgb_ref[...])
    pa = _dot(ya_ref[...], wa_ref[...])
    pb = _dot(yb_ref[...], wb_ref[...])
    mix = _sigmoid(ga) * pa + _sigmoid(gb) * pb
    o_ref[...] += _dot(mix.astype(BF16), wo_ref[...])


def _merge(x2d, g, ya, yb, wga, wgb, wa, wb, wo, *, tm, tj):
    m, d = x2d.shape
    rowblk = lambda i, j: (i, 0)
    colblk = lambda i, j: (0, j)
    return pl.pallas_call(
        _merge_kernel,
        grid=(m // tm, d // tj),
        in_specs=[
            pl.BlockSpec((tm, d), rowblk),
            pl.BlockSpec((1, d), lambda i, j: (0, 0)),
            pl.BlockSpec((tm, d), rowblk),
            pl.BlockSpec((tm, d), rowblk),
            pl.BlockSpec((d, tj), colblk),
            pl.BlockSpec((d, tj), colblk),
            pl.BlockSpec((d, tj), colblk),
            pl.BlockSpec((d, tj), colblk),
            pl.BlockSpec((tj, d), lambda i, j: (j, 0)),
        ],
        out_specs=pl.BlockSpec((tm, d), rowblk),
        out_shape=jax.ShapeDtypeStruct((m, d), F32),
        scratch_shapes=[pltpu.VMEM((tm, d), BF16)],
        compiler_params=_params("parallel", "arbitrary"),
        name="merge",
    )(x2d, g, ya, yb, wga, wgb, wa, wb, wo)


def _mlp_kernel(h_ref, g_ref, wu_ref, wd_ref, gf_ref, o_ref, u_ref):
    f = pl.program_id(1)

    @pl.when(f == 0)
    def _():
        x = h_ref[...]
        u_ref[...] = _norm_rows(x, g_ref[...]).astype(BF16)
        o_ref[...] = x

    a = jnp.maximum(_dot(u_ref[...], wu_ref[...]), 0.0)
    o_ref[...] += _dot((a * a).astype(BF16), wd_ref[...])

    @pl.when(f == pl.num_programs(1) - 1)
    def _():
        o_ref[...] = _norm_rows(o_ref[...], gf_ref[...])


def _mlp(h2d, g, wu, wd, gf, *, tm, tf):
    m, d = h2d.shape
    dff = wu.shape[1]
    rowblk = lambda i, f: (i, 0)
    return pl.pallas_call(
        _mlp_kernel,
        grid=(m // tm, dff // tf),
        in_specs=[
            pl.BlockSpec((tm, d), rowblk),
            pl.BlockSpec((1, d), lambda i, f: (0, 0)),
            pl.BlockSpec((d, tf), lambda i, f: (0, f)),
            pl.BlockSpec((tf, d), lambda i, f: (f, 0)),
            pl.BlockSpec((1, d), lambda i, f: (0, 0)),
        ],
        out_specs=pl.BlockSpec((tm, d), rowblk),
        out_shape=jax.ShapeDtypeStruct((m, d), F32),
        scratch_shapes=[pltpu.VMEM((tm, d), BF16)],
        compiler_params=_params("parallel", "arbitrary"),
        name="mlp",
    )(h2d, g, wu, wd, gf)


def _constants():
    r = np.arange(128)
    ltri = (r[:, None] >= r[None, :]).astype(np.float32)
    ltri3 = np.concatenate([ltri, ltri, ltri], axis=1)

    sq = np.zeros((3 * LANES, LANES), np.float32)
    sk = np.zeros((3 * LANES, LANES), np.float32)
    cq = np.zeros((1, LANES), np.float32)
    ck = np.zeros((1, LANES), np.float32)
    for h in range(HEADS):
        for p in range(3):
            sq[p * LANES + SM_F + h, FOX_EXTRA * h + p] = 1.0
            sk[p * LANES + SM_F + h, FOX_EXTRA * h + 3 + p] = -1.0
            cq[0, FOX_EXTRA * h + 3 + p] = 1.0
            ck[0, FOX_EXTRA * h + p] = 1.0
        ck[0, FOX_EXTRA * h + FOX_M_LANE] = 1.0

    shift3 = np.zeros((3 * CHUNK, 2 * CHUNK), np.float32)
    for s in range(1, CONV_K):
        shift3[(s - 1) * CHUNK + r, CHUNK + r - s] = 1.0

    src_lane = np.zeros((HEADS * N_GQ * LANES,), np.int64)
    for h in range(HEADS):
        for q in range(N_GQ):
            src_lane[(h * N_GQ + q) * LANES:(h * N_GQ + q + 1) * LANES] = 16 * q + h
    rows3 = np.arange(3 * LANES) % LANES
    sel = (rows3[:, None] == src_lane[None, :]).astype(np.float32)
    oneh = (np.repeat(16 * GQ_G + np.arange(HEADS), CHUNK)[:, None] == rows3[None, :]).astype(np.float32)

    as_bf = lambda a: jnp.asarray(a, BF16)
    return dict(ltri3=as_bf(ltri3), sq=as_bf(sq), sk=as_bf(sk), cq=jnp.asarray(cq),
                ck=jnp.asarray(ck), shift3=as_bf(shift3), sel=as_bf(sel), oneh=as_bf(oneh))


def _row128(vec, offset):
    return jnp.zeros((1, LANES), F32).at[0, offset:offset + HEADS].set(vec.astype(F32))


def _pick_rows_tile(total, target):
    best = 128
    for t in range(128, target + 1, 128):
        if total % t == 0:
            best = t
    return best


def _layer(x, meta_tokens, mix_norm_g, w_in, conv_w, a_log, dt_bias, gdn_norm_g, w_o_gdn,
           fox_q_norm_g, fox_k_norm_g, fox_f_bias, w_o_fox, w_out, mlp_norm_g, w_up, w_down,
           final_norm_g):
    b, seq, d = x.shape
    lp = CHUNK + seq
    qk = HEADS * HEAD_DIM

    o_z = 3 * qk
    o_b = o_z + qk
    o_a = o_b + HEADS
    o_fq = o_a + HEADS
    o_f = o_fq + 3 * qk
    o_ga = o_f + HEADS
    o_gb = o_ga + d

    w_rest = w_in[:, o_b:].astype(BF16)
    rest = lambda lo, hi: w_rest[:, lo - o_b:hi - o_b]
    w_fox = rest(o_fq, o_f)
    ws = jnp.concatenate([rest(o_b, o_fq), rest(o_f, o_ga),
                          jnp.zeros((d, LANES - 3 * HEADS), BF16)], axis=1)
    wga = rest(o_ga, o_gb)
    wgb = rest(o_gb, w_in.shape[1])

    row = lambda v: v.reshape(1, -1).astype(F32)
    cst = _constants()

    hn = _prenorm(x, meta_tokens.astype(x.dtype), row(mix_norm_g))
    p1, ps = _inproj(hn.reshape(b * lp, d), w_in, o_b, w_fox, ws,
                     tm=_pick_rows_tile(lp, 1408), tn=1024)
    p1 = p1.reshape(b, lp, P1_COLS)
    ps = ps.reshape(b, lp, LANES)

    yp, eq, ek = _gate_prep(ps, _row128(a_log, SM_A), _row128(dt_bias, SM_A),
                            _row128(fox_f_bias, SM_F), cst["ltri3"], cst["sq"], cst["sk"],
                            cst["cq"], cst["ck"])
    ya = _gdn(p1, yp, cst["sel"], cst["oneh"], conv_w.astype(F32), row(gdn_norm_g),
              cst["shift3"], seq=seq, hg=8)
    yb = _fox(p1, eq, ek, row(fox_q_norm_g), row(fox_k_norm_g), seq=seq, tq=1024)

    x2d = x.reshape(b * seq, d)
    h1 = _merge(x2d, row(mix_norm_g), ya.reshape(b * seq, qk), yb.reshape(b * seq, qk),
                wga, wgb, w_o_gdn.astype(BF16), w_o_fox.astype(BF16), w_out.astype(BF16),
                tm=512, tj=512)
    out = _mlp(h1, row(mlp_norm_g), w_up.astype(BF16), w_down.astype(BF16),
               row(final_norm_g), tm=512, tf=1024)
    return out.reshape(b, seq, d)


def kernel(x, meta_tokens, mix_norm_g, w_in, conv_w, a_log, dt_bias, gdn_norm_g, w_o_gdn,
           fox_q_norm_g, fox_k_norm_g, fox_f_bias, w_o_fox, w_out, mlp_norm_g, w_up, w_down,
           final_norm_g):
    assert w_in.shape[0] == 1, "single-layer block"
    return _layer(x, meta_tokens, mix_norm_g[0], w_in[0], conv_w[0], a_log[0], dt_bias[0],
                  gdn_norm_g[0], w_o_gdn[0], fox_q_norm_g[0], fox_k_norm_g[0], fox_f_bias[0],
                  w_o_fox[0], w_out[0], mlp_norm_g[0], w_up[0], w_down[0], final_norm_g)
```

```python
import functools
import math

import numpy as np
import jax
import jax.numpy as jnp
from jax import lax
from jax.experimental import pallas as pl
from jax.experimental.pallas import tpu as pltpu

F32 = jnp.float32
BF16 = jnp.bfloat16

NORM_EPS = 1e-6
N_META = 16
HEADS = 16
HEAD_DIM = 128
CONV_K = 4
LANES = 128
CHUNK = 128
FRONT_PAD = CHUNK - N_META
NEG_BIG = -1e30
LOG2E = math.log2(math.e)
VMEM_LIMIT = 56 * 1024 * 1024

COL_GDN_Q, COL_GDN_K, COL_GDN_V, COL_Z = 0, 16, 32, 48
COL_FOX_Q, COL_FOX_K, COL_FOX_V = 64, 80, 96
P1_COLS = 112 * LANES
SM_B, SM_A, SM_F = 0, 16, 32
GQ_G, GQ_BETA, GQ_EG, GQ_EKD, GQ_BEG = range(5)
N_GQ = 5
FOX_EXTRA = 8
FOX_M_LANE = 6
FOX_SAFE_BOUND = 40.0


def _params(*sem):
    return pltpu.CompilerParams(dimension_semantics=sem, vmem_limit_bytes=VMEM_LIMIT)


def _norm_rows(x, g):
    ms = jnp.mean(x * x, axis=-1, keepdims=True)
    return x * lax.rsqrt(ms + NORM_EPS) * g


def _split3(x):
    hi = x.astype(BF16)
    r = x - hi.astype(F32)
    mid = r.astype(BF16)
    lo = (r - mid.astype(F32)).astype(BF16)
    return hi, mid, lo


def _dot(a, b):
    return jnp.dot(a, b, preferred_element_type=F32)


def _dot_nt(a, b):
    return lax.dot_general(a, b, (((1,), (1,)), ((), ())), preferred_element_type=F32)


def _dot_tn(a, b):
    return lax.dot_general(a, b, (((0,), (0,)), ((), ())), preferred_element_type=F32)


def _sigmoid(x):
    return 1.0 / (1.0 + jnp.exp(-x))


def _silu(x):
    return x * _sigmoid(x)


def _softplus(x):
    return jnp.maximum(x, 0.0) + jnp.log1p(jnp.exp(-jnp.abs(x)))


def _prenorm_kernel(x_ref, meta_ref, g_ref, o_ref):
    r = pl.program_id(1)

    @pl.when(r == 0)
    def _():
        o_ref[0:FRONT_PAD, :] = jnp.zeros((FRONT_PAD, o_ref.shape[1]), o_ref.dtype)
        o_ref[FRONT_PAD:CHUNK, :] = _norm_rows(meta_ref[...], g_ref[...]).astype(o_ref.dtype)

    @pl.when(r > 0)
    def _():
        o_ref[...] = _norm_rows(x_ref[...], g_ref[...]).astype(o_ref.dtype)


def _prenorm(x, meta_tokens, g):
    b, seq, d = x.shape
    lp = CHUNK + seq
    return pl.pallas_call(
        _prenorm_kernel,
        grid=(b, lp // CHUNK),
        in_specs=[
            pl.BlockSpec((None, CHUNK, d), lambda bi, r: (bi, jnp.maximum(r - 1, 0), 0)),
            pl.BlockSpec((N_META, d), lambda bi, r: (0, 0)),
            pl.BlockSpec((1, d), lambda bi, r: (0, 0)),
        ],
        out_specs=pl.BlockSpec((None, CHUNK, d), lambda bi, r: (bi, r, 0)),
        out_shape=jax.ShapeDtypeStruct((b, lp, d), BF16),
        compiler_params=_params("parallel", "arbitrary"),
        name="prenorm",
    )(x, meta_tokens, g)


def _inproj_kernel(x_ref, wa_ref, wb_ref, ws_ref, o_ref, os_ref, *, na):
    j = pl.program_id(1)

    @pl.when(j == 0)
    def _():
        os_ref[...] = _dot(x_ref[...], ws_ref[...])

    @pl.when(j < na)
    def _():
        o_ref[...] = _dot(x_ref[...], wa_ref[...].astype(BF16)).astype(o_ref.dtype)

    @pl.when(j >= na)
    def _():
        o_ref[...] = _dot(x_ref[...], wb_ref[...]).astype(o_ref.dtype)


def _inproj(hn, w_all, n_a, wb, n_b, ws, *, tm, tn):
    m, d = hn.shape
    na, nb = n_a // tn, n_b // tn
    return pl.pallas_call(
        functools.partial(_inproj_kernel, na=na),
        grid=(m // tm, na + nb),
        in_specs=[
            pl.BlockSpec((tm, d), lambda i, j: (i, 0)),
            pl.BlockSpec((None, d, tn), lambda i, j: (0, 0, jnp.minimum(j, na - 1))),
            pl.BlockSpec((d, tn), lambda i, j: (0, jnp.maximum(j - na, 0))),
            pl.BlockSpec((d, LANES), lambda i, j: (0, 0)),
        ],
        out_specs=[
            pl.BlockSpec((tm, tn), lambda i, j: (i, j)),
            pl.BlockSpec((tm, LANES), lambda i, j: (i, 0)),
        ],
        out_shape=[
            jax.ShapeDtypeStruct((m, (na + nb) * tn), BF16),
            jax.ShapeDtypeStruct((m, LANES), F32),
        ],
        compiler_params=_params("parallel", "arbitrary"),
        name="inproj",
    )(hn, w_all, wb, ws)


def _gate_prep_kernel(ps_ref, alog_ref, dtb_ref, fb_ref, ltri_ref, sq_ref, sk_ref, cq_ref, ck_ref,
                      yp_ref, eq_ref, ek_ref, *, nblk):
    lane = lax.broadcasted_iota(jnp.int32, (CHUNK, LANES), 1)
    rowi = lax.broadcasted_iota(jnp.int32, (CHUNK, LANES), 0)

    def body(i, carry):
        rows = pl.ds(pl.multiple_of(i * CHUNK, CHUNK), CHUNK)
        x = ps_ref[rows, :]
        valid = (i * CHUNK + rowi) >= FRONT_PAD

        beta = jnp.where(valid, _sigmoid(x), 0.0)
        gstep = jnp.where(valid, -jnp.exp(alog_ref[...]) * _softplus(x + dtb_ref[...]), 0.0)
        gcum = _dot(ltri_ref[...], jnp.concatenate(_split3(gstep), axis=0))
        eg = jnp.exp(gcum)
        ekd = jnp.exp(gcum[CHUNK - 1:CHUNK, :] - gcum)
        beg = pltpu.roll(beta, SM_A - SM_B, 1) * eg
        y = jnp.where(lane < 16 * (GQ_EKD + 1), pltpu.roll(ekd, 16 * GQ_EKD - SM_A, 1),
                      pltpu.roll(beg, 16 * GQ_BEG - SM_A, 1))
        y = jnp.where(lane < 16 * (GQ_EG + 1), pltpu.roll(eg, 16 * GQ_EG - SM_A, 1), y)
        y = jnp.where(lane < 16 * (GQ_BETA + 1), pltpu.roll(beta, 16 * GQ_BETA - SM_B, 1), y)
        y = jnp.where(lane < 16 * (GQ_G + 1), pltpu.roll(gcum, LANES + 16 * GQ_G - SM_A, 1), y)
        yp_ref[rows, :] = jnp.concatenate(_split3(y), axis=1)

        xf = x + fb_ref[...]
        ls = (jnp.minimum(xf, 0.0) - jnp.log1p(jnp.exp(-jnp.abs(xf)))) * LOG2E
        cum = _dot(ltri_ref[...], jnp.concatenate(_split3(ls), axis=0)) + carry
        cp = jnp.concatenate(_split3(cum), axis=1)
        eq_ref[rows, :] = (_dot(cp, sq_ref[...]) + cq_ref[...]).astype(BF16)
        ek_ref[rows, :] = (_dot(cp, sk_ref[...]) + ck_ref[...]).astype(BF16)
        return cum[CHUNK - 1:CHUNK, :]

    lax.fori_loop(0, nblk, body, jnp.zeros((1, LANES), F32))


def _gate_prep(ps, alog_row, dtb_row, fb_row, ltri3, sq, sk, cq, ck):
    b, lp, _ = ps.shape
    const = lambda bi: (0, 0)
    return pl.pallas_call(
        functools.partial(_gate_prep_kernel, nblk=lp // CHUNK),
        grid=(b,),
        in_specs=[
            pl.BlockSpec((None, lp, LANES), lambda bi: (bi, 0, 0)),
            pl.BlockSpec((1, LANES), const),
            pl.BlockSpec((1, LANES), const),
            pl.BlockSpec((1, LANES), const),
            pl.BlockSpec((CHUNK, 3 * CHUNK), const),
            pl.BlockSpec((3 * LANES, LANES), const),
            pl.BlockSpec((3 * LANES, LANES), const),
            pl.BlockSpec((1, LANES), const),
            pl.BlockSpec((1, LANES), const),
        ],
        out_specs=[
            pl.BlockSpec((None, lp, 3 * LANES), lambda bi: (bi, 0, 0)),
            pl.BlockSpec((None, lp, LANES), lambda bi: (bi, 0, 0)),
            pl.BlockSpec((None, lp, LANES), lambda bi: (bi, 0, 0)),
        ],
        out_shape=[
            jax.ShapeDtypeStruct((b, lp, 3 * LANES), BF16),
            jax.ShapeDtypeStruct((b, lp, LANES), BF16),
            jax.ShapeDtypeStruct((b, lp, LANES), BF16),
        ],
        compiler_params=_params("parallel"),
        name="gate_prep",
    )(ps, alog_row, dtb_row, fb_row, ltri3, sq, sk, cq, ck)


def _gdn_kernel(q_ref, k_ref, v_ref, z_ref, yp_ref, sel_ref, oneh_ref, wq_ref, wk_ref, wv_ref,
                gn_ref, shift_ref, o_ref, s_ref, prev_ref, *, hg):
    c = pl.program_id(2)

    @pl.when(c == 0)
    def _():
        s_ref[...] = jnp.zeros_like(s_ref)
        prev_ref[...] = jnp.zeros_like(prev_ref)

    row = lax.broadcasted_iota(jnp.int32, (CHUNK, CHUNK), 0)
    col = lax.broadcasted_iota(jnp.int32, (CHUNK, CHUNK), 1)
    tril = row >= col
    strict = row > col
    heads = range(hg)
    hsl = lambda a, u: a[:, u * HEAD_DIM:(u + 1) * HEAD_DIM]

    yp = yp_ref[...]
    gb = _dot(yp, sel_ref[...])
    g_rows = _dot_nt(oneh_ref[...], yp)
    gq = lambda u, q: gb[:, (u * N_GQ + q) * LANES:(u * N_GQ + q + 1) * LANES]

    def conv_act(x_ref, w_ref, t):
        x16 = x_ref[...]
        x2 = jnp.concatenate([prev_ref[t], x16], axis=0)
        prev_ref[t] = x16
        sh = _dot(shift_ref[...], x2)
        w = w_ref[...]
        acc = x16.astype(F32) * w[CONV_K - 1:CONV_K, :]
        for s in range(1, CONV_K):
            acc = acc + sh[(s - 1) * CHUNK:s * CHUNK, :] * w[CONV_K - 1 - s:CONV_K - s, :]
        return _silu(acc)

    qa = conv_act(q_ref, wq_ref, 0)
    ka = conv_act(k_ref, wk_ref, 1)
    va = conv_act(v_ref, wv_ref, 2)

    def l2n(x):
        return x * lax.rsqrt(jnp.sum(x * x, axis=-1, keepdims=True) + NORM_EPS)

    qn = [l2n(hsl(qa, u)) * (HEAD_DIM ** -0.5) for u in heads]
    kn = [l2n(hsl(ka, u)) for u in heads]
    k16 = [kn[u].astype(BF16) for u in heads]

    state = [s_ref[u] for u in heads]
    r1 = [_dot(jnp.concatenate([kn[u] * gq(u, GQ_BEG), qn[u] * gq(u, GQ_EG)], axis=0).astype(BF16),
               state[u].astype(BF16)) for u in heads]

    kq = [_dot_nt(jnp.concatenate([kn[u] * gq(u, GQ_BETA), qn[u]], axis=0).astype(BF16), k16[u])
          for u in heads]
    ediff = [jnp.exp(jnp.minimum(gq(u, GQ_G) - g_rows[u * CHUNK:(u + 1) * CHUNK, :], 0.0))
             for u in heads]
    a = [kq[u][:CHUNK] * jnp.where(strict, ediff[u], 0.0) for u in heads]
    qk_m = [(kq[u][CHUNK:] * jnp.where(tril, ediff[u], 0.0)).astype(BF16) for u in heads]

    a16 = [a[u].astype(BF16) for u in heads]
    yk = [_dot(a16[u], a16[u]) for u in heads]
    sk = [-a[u] for u in heads]
    nlev = int(np.log2(CHUNK)) - 1
    for lev in range(nlev):
        yk16 = [yk[u].astype(BF16) for u in heads]
        if lev < nlev - 1:
            zz = [_dot(jnp.concatenate([sk[u], yk[u]], axis=0).astype(BF16), yk16[u]) for u in heads]
            sk = [sk[u] + yk[u] + zz[u][:CHUNK] for u in heads]
            yk = [zz[u][CHUNK:] for u in heads]
        else:
            sk = [sk[u] + yk[u] + _dot(sk[u].astype(BF16), yk16[u]) for u in heads]

    rhs = [hsl(va, u) * gq(u, GQ_BETA) - r1[u][:CHUNK] for u in heads]
    v_new = [(rhs[u] + _dot(sk[u].astype(BF16), rhs[u].astype(BF16))).astype(BF16) for u in heads]
    o = [r1[u][CHUNK:] + _dot(qk_m[u], v_new[u]) for u in heads]
    for u in heads:
        eg_last = gq(u, GQ_EG)[CHUNK - 1:CHUNK, :]
        s_ref[u] = state[u] * eg_last + _dot_tn((kn[u] * gq(u, GQ_EKD)).astype(BF16), v_new[u])

    on = jnp.concatenate([_norm_rows(o[u], gn_ref[...]) for u in heads], axis=1)
    o_ref[...] = (on * _silu(z_ref[...].astype(F32))).astype(o_ref.dtype)


def _gdn(p1, yp, sel, oneh, conv_w, gnorm, shift3, *, seq, hg):
    b, lp, _ = p1.shape
    nc = lp // CHUNK
    gw = hg * HEAD_DIM
    ng = HEADS // hg
    col = lambda base: (lambda bi, gi, ci: (bi, ci, base * LANES // gw + gi))
    wcol = lambda base: (lambda bi, gi, ci: (0, base * LANES // gw + gi))
    const = lambda bi, gi, ci: (0, 0)
    return pl.pallas_call(
        functools.partial(_gdn_kernel, hg=hg),
        grid=(b, ng, nc),
        in_specs=[
            pl.BlockSpec((None, CHUNK, gw), col(COL_GDN_Q)),
            pl.BlockSpec((None, CHUNK, gw), col(COL_GDN_K)),
            pl.BlockSpec((None, CHUNK, gw), col(COL_GDN_V)),
            pl.BlockSpec((None, CHUNK, gw), col(COL_Z)),
            pl.BlockSpec((None, CHUNK, 3 * LANES), lambda bi, gi, ci: (bi, ci, 0)),
            pl.BlockSpec((3 * LANES, hg * N_GQ * LANES), lambda bi, gi, ci: (0, gi)),
            pl.BlockSpec((hg * CHUNK, 3 * LANES), lambda bi, gi, ci: (gi, 0)),
            pl.BlockSpec((CONV_K, gw), wcol(COL_GDN_Q)),
            pl.BlockSpec((CONV_K, gw), wcol(COL_GDN_K)),
            pl.BlockSpec((CONV_K, gw), wcol(COL_GDN_V)),
            pl.BlockSpec((1, HEAD_DIM), const),
            pl.BlockSpec((3 * CHUNK, 2 * CHUNK), const),
        ],
        out_specs=pl.BlockSpec((None, CHUNK, gw),
                               lambda bi, gi, ci: (bi, jnp.maximum(ci - 1, 0), gi)),
        out_shape=jax.ShapeDtypeStruct((b, seq, HEADS * HEAD_DIM), BF16),
        scratch_shapes=[
            pltpu.VMEM((hg, HEAD_DIM, HEAD_DIM), F32),
            pltpu.VMEM((3, CHUNK, gw), BF16),
        ],
        compiler_params=_params("parallel", "parallel", "arbitrary"),
        name="gdn",
    )(p1, p1, p1, p1, yp, sel, oneh, conv_w, conv_w, conv_w, gnorm, shift3)


def _fox_kernel(q_ref, k_ref, v_ref, eq_ref, ek_ref, gq_ref, gk_ref, o_ref,
                kaug_ref, vaug_ref, *, tq, lp):
    h = pl.program_id(1)
    i = pl.program_id(2)

    @pl.when(i == 0)
    def _():
        def build(rows, n):
            kn = _norm_rows(k_ref[rows, :].astype(F32), gk_ref[...])
            kaug_ref[rows, 0:HEAD_DIM] = kn.astype(BF16)
            kaug_ref[rows, HEAD_DIM:2 * HEAD_DIM] = ek_ref[rows, :]
            vaug_ref[rows, 0:HEAD_DIM] = v_ref[rows, :]
            vaug_ref[rows, HEAD_DIM:2 * HEAD_DIM] = jnp.ones((n, HEAD_DIM), BF16)

        def body(r, c):
            build(pl.ds(pl.multiple_of(r * tq, 128), tq), tq)
            return c
        lax.fori_loop(0, lp // tq, body, 0)
        if lp % tq:
            build(pl.ds(lp - lp % tq, lp % tq), lp % tq)

    q0 = pl.multiple_of(CHUNK + i * tq, 128)
    rows_q = pl.ds(q0, tq)
    qn = _norm_rows(q_ref[rows_q, :].astype(F32), gq_ref[...]) * (HEAD_DIM ** -0.5 * LOG2E)
    lane = lax.broadcasted_iota(jnp.int32, (tq, LANES), 1)
    mine = lax.shift_right_logical(lane, 3) == h
    eqm = jnp.where(mine, eq_ref[rows_q, :], jnp.zeros((), BF16))
    q16 = qn.astype(BF16)

    gmax = (jnp.max(jnp.abs(gq_ref[...]), axis=-1, keepdims=True)
            * jnp.max(jnp.abs(gk_ref[...]), axis=-1, keepdims=True))
    bound = (gmax * (HEAD_DIM ** 0.5 * LOG2E * 1.02)).astype(BF16)
    safe = bound.astype(F32)[0, 0] <= FOX_SAFE_BOUND

    rr = lax.broadcasted_iota(jnp.int32, (tq, tq), 0)
    cc = lax.broadcasted_iota(jnp.int32, (tq, tq), 1)
    causal = rr >= cc

    def blocks(j):
        return pl.ds(pl.multiple_of(CHUNK + j * tq, 128), tq)

    @pl.when(safe)
    def _():
        eqb = jnp.where(lane == FOX_EXTRA * h + FOX_M_LANE, -bound, eqm)
        qaug = jnp.concatenate([q16, eqb], axis=1)
        half = tq // 2

        def probs(qa, krows, mask=None):
            s = _dot_nt(qa, kaug_ref[krows, :])
            if mask is not None:
                s = jnp.where(mask, s, NEG_BIG)
            return jnp.exp2(s).astype(BF16)

        def finish(acc, rows):
            o_ref[rows, :] = (acc[:, :HEAD_DIM] / acc[:, HEAD_DIM:]).astype(o_ref.dtype)

        for ii in range(pl.cdiv(lp - CHUNK, tq)):
            @pl.when(i == ii)
            def _(ii=ii):
                acc = _dot(probs(qaug, slice(0, CHUNK), lane >= FRONT_PAD), vaug_ref[0:CHUNK, :])
                for j in range(ii):
                    kr = slice(CHUNK + j * tq, CHUNK + (j + 1) * tq)
                    acc = acc + _dot(probs(qaug, kr), vaug_ref[kr, :])
                k_lo = slice(CHUNK + ii * tq, CHUNK + ii * tq + half)
                k_all = slice(CHUNK + ii * tq, CHUNK + (ii + 1) * tq)
                p_lo = probs(qaug[:half], k_lo, causal[:half, :half])
                p_hi = probs(qaug[half:], k_all, causal[half:, :])
                finish(acc[:half] + _dot(p_lo, vaug_ref[k_lo, :]), slice(0, half))
                finish(acc[half:] + _dot(p_hi, vaug_ref[k_all, :]), slice(half, tq))

    @pl.when(jnp.logical_not(safe))
    def _():
        qaug = jnp.concatenate([q16, eqm], axis=1)
        s = _dot_nt(qaug, kaug_ref[0:CHUNK, :])
        s = jnp.where(lane >= FRONT_PAD, s, NEG_BIG)
        m = jnp.max(s, axis=-1, keepdims=True)
        acc = _dot(jnp.exp2(s - m).astype(BF16), vaug_ref[0:CHUNK, :])

        def update(s, vrows, m, acc):
            m_new = jnp.maximum(m, jnp.max(s, axis=-1, keepdims=True))
            p = jnp.exp2(s - m_new)
            return m_new, jnp.exp2(m - m_new) * acc + _dot(p.astype(BF16), vrows)

        def body(j, carry):
            s = _dot_nt(qaug, kaug_ref[blocks(j), :])
            return update(s, vaug_ref[blocks(j), :], *carry)
        m, acc = lax.fori_loop(0, i, body, (m, acc))

        s = _dot_nt(qaug, kaug_ref[rows_q, :])
        m, acc = update(jnp.where(causal, s, NEG_BIG), vaug_ref[rows_q, :], m, acc)
        o_ref[...] = (acc[:, :HEAD_DIM] / acc[:, HEAD_DIM:]).astype(o_ref.dtype)


def _fox(p1, eq, ek, gq, gk, *, seq, tq):
    b, lp, _ = p1.shape
    colmap = lambda base: (lambda bi, hi, qi: (bi, 0, base + hi))
    const = lambda bi, hi, qi: (0, 0)
    return pl.pallas_call(
        functools.partial(_fox_kernel, tq=tq, lp=lp),
        grid=(b, HEADS, seq // tq),
        in_specs=[
            pl.BlockSpec((None, lp, HEAD_DIM), colmap(COL_FOX_Q)),
            pl.BlockSpec((None, lp, HEAD_DIM), colmap(COL_FOX_K)),
            pl.BlockSpec((None, lp, HEAD_DIM), colmap(COL_FOX_V)),
            pl.BlockSpec((None, lp, LANES), lambda bi, hi, qi: (bi, 0, 0)),
            pl.BlockSpec((None, lp, LANES), lambda bi, hi, qi: (bi, 0, 0)),
            pl.BlockSpec((1, HEAD_DIM), const),
            pl.BlockSpec((1, HEAD_DIM), const),
        ],
        out_specs=pl.BlockSpec((None, tq, HEAD_DIM), lambda bi, hi, qi: (bi, qi, hi)),
        out_shape=jax.ShapeDtypeStruct((b, seq, HEADS * HEAD_DIM), BF16),
        scratch_shapes=[
            pltpu.VMEM((lp, 2 * HEAD_DIM), BF16),
            pltpu.VMEM((lp, 2 * HEAD_DIM), BF16),
        ],
        compiler_params=_params("parallel", "parallel", "arbitrary"),
        name="fox",
    )(p1, p1, p1, eq, ek, gq, gk)


def _merge_kernel(x_ref, g_ref, ya_ref, yb_ref, wga_ref, wgb_ref, wa_ref, wb_ref, wo_ref,
                  o_ref, u_ref):
    j = pl.program_id(1)

    @pl.when(j == 0)
    def _():
        x = x_ref[...]
        u_ref[...] = _norm_rows(x, g_ref[...]).astype(BF16)
        o_ref[...] = x

    u = u_ref[...]
    ga = _dot(u, wga_ref[...])
    gb = _dot(u, wgb_ref[...])
    pa = _dot(ya_ref[...], wa_ref[...])
    pb = _dot(yb_ref[...], wb_ref[...])
    mix = _sigmoid(ga) * pa + _sigmoid(gb) * pb
    o_ref[...] += _dot(mix.astype(BF16), wo_ref[...])


def _merge(x2d, g, ya, yb, wga, wgb, wa, wb, wo, *, tm, tj):
    m, d = x2d.shape
    rowblk = lambda i, j: (i, 0)
    colblk = lambda i, j: (0, j)
    return pl.pallas_call(
        _merge_kernel,
        grid=(m // tm, d // tj),
        in_specs=[
            pl.BlockSpec((tm, d), rowblk),
            pl.BlockSpec((1, d), lambda i, j: (0, 0)),
            pl.BlockSpec((tm, d), rowblk),
            pl.BlockSpec((tm, d), rowblk),
            pl.BlockSpec((d, tj), colblk),
            pl.BlockSpec((d, tj), colblk),
            pl.BlockSpec((d, tj), colblk),
            pl.BlockSpec((d, tj), colblk),
            pl.BlockSpec((tj, d), lambda i, j: (j, 0)),
        ],
        out_specs=pl.BlockSpec((tm, d), rowblk),
        out_shape=jax.ShapeDtypeStruct((m, d), F32),
        scratch_shapes=[pltpu.VMEM((tm, d), BF16)],
        compiler_params=_params("parallel", "arbitrary"),
        name="merge",
    )(x2d, g, ya, yb, wga, wgb, wa, wb, wo)


def _mlp_kernel(h_ref, g_ref, wu_ref, wd_ref, gf_ref, o_ref, u_ref):
    f = pl.program_id(1)

    @pl.when(f == 0)
    def _():
        x = h_ref[...]
        u_ref[...] = _norm_rows(x, g_ref[...]).astype(BF16)
        o_ref[...] = x

    a = jnp.maximum(_dot(u_ref[...], wu_ref[...]), 0.0)
    o_ref[...] += _dot((a * a).astype(BF16), wd_ref[...])

    @pl.when(f == pl.num_programs(1) - 1)
    def _():
        o_ref[...] = _norm_rows(o_ref[...], gf_ref[...])


def _mlp(h2d, g, wu, wd, gf, *, tm, tf):
    m, d = h2d.shape
    dff = wu.shape[1]
    rowblk = lambda i, f: (i, 0)
    return pl.pallas_call(
        _mlp_kernel,
        grid=(m // tm, dff // tf),
        in_specs=[
            pl.BlockSpec((tm, d), rowblk),
            pl.BlockSpec((1, d), lambda i, f: (0, 0)),
            pl.BlockSpec((d, tf), lambda i, f: (0, f)),
            pl.BlockSpec((tf, d), lambda i, f: (f, 0)),
            pl.BlockSpec((1, d), lambda i, f: (0, 0)),
        ],
        out_specs=pl.BlockSpec((tm, d), rowblk),
        out_shape=jax.ShapeDtypeStruct((m, d), F32),
        scratch_shapes=[pltpu.VMEM((tm, d), BF16)],
        compiler_params=_params("parallel", "arbitrary"),
        name="mlp",
    )(h2d, g, wu, wd, gf)


def _constants():
    r = np.arange(128)
    ltri = (r[:, None] >= r[None, :]).astype(np.float32)
    ltri3 = np.concatenate([ltri, ltri, ltri], axis=1)

    sq = np.zeros((3 * LANES, LANES), np.float32)
    sk = np.zeros((3 * LANES, LANES), np.float32)
    cq = np.zeros((1, LANES), np.float32)
    ck = np.zeros((1, LANES), np.float32)
    for h in range(HEADS):
        for p in range(3):
            sq[p * LANES + SM_F + h, FOX_EXTRA * h + p] = 1.0
            sk[p * LANES + SM_F + h, FOX_EXTRA * h + 3 + p] = -1.0
            cq[0, FOX_EXTRA * h + 3 + p] = 1.0
            ck[0, FOX_EXTRA * h + p] = 1.0
        ck[0, FOX_EXTRA * h + FOX_M_LANE] = 1.0

    shift3 = np.zeros((3 * CHUNK, 2 * CHUNK), np.float32)
    for s in range(1, CONV_K):
        shift3[(s - 1) * CHUNK + r, CHUNK + r - s] = 1.0

    src_lane = np.zeros((HEADS * N_GQ * LANES,), np.int64)
    for h in range(HEADS):
        for q in range(N_GQ):
            src_lane[(h * N_GQ + q) * LANES:(h * N_GQ + q + 1) * LANES] = 16 * q + h
    rows3 = np.arange(3 * LANES) % LANES
    sel = (rows3[:, None] == src_lane[None, :]).astype(np.float32)
    oneh = (np.repeat(16 * GQ_G + np.arange(HEADS), CHUNK)[:, None] == rows3[None, :]).astype(np.float32)

    as_bf = lambda a: jnp.asarray(a, BF16)
    return dict(ltri3=as_bf(ltri3), sq=as_bf(sq), sk=as_bf(sk), cq=jnp.asarray(cq),
                ck=jnp.asarray(ck), shift3=as_bf(shift3), sel=as_bf(sel), oneh=as_bf(oneh))


def _row128(vec, offset):
    return jnp.zeros((1, LANES), F32).at[0, offset:offset + HEADS].set(vec.astype(F32))


def _pick_rows_tile(total, target):
    best = 128
    for t in range(128, target + 1, 128):
        if total % t == 0:
            best = t
    return best


def _layer(x, meta_tokens, mix_norm_g, w_in, conv_w, a_log, dt_bias, gdn_norm_g, w_o_gdn,
           fox_q_norm_g, fox_k_norm_g, fox_f_bias, w_o_fox, w_out, mlp_norm_g, w_up, w_down,
           final_norm_g):
    b, seq, d = x.shape
    lp = CHUNK + seq
    qk = HEADS * HEAD_DIM

    o_z = 3 * qk
    o_b = o_z + qk
    o_a = o_b + HEADS
    o_fq = o_a + HEADS
    o_f = o_fq + 3 * qk
    o_ga = o_f + HEADS
    o_gb = o_ga + d

    w_rest = w_in[0, :, o_fq:].astype(BF16)
    rest = lambda lo, hi: w_rest[:, lo - o_fq:hi - o_fq]
    ws = jnp.concatenate([w_in[0, :, o_b:o_fq].astype(BF16), rest(o_f, o_ga),
                          jnp.zeros((d, LANES - 3 * HEADS), BF16)], axis=1)
    wga = rest(o_ga, o_gb)
    wgb = rest(o_gb, w_in.shape[2])

    row = lambda v: v.reshape(1, -1).astype(F32)
    cst = _constants()

    hn = _prenorm(x, meta_tokens.astype(x.dtype), row(mix_norm_g))
    p1, ps = _inproj(hn.reshape(b * lp, d), w_in, o_b, w_rest, o_f - o_fq, ws,
                     tm=_pick_rows_tile(lp, 1408), tn=1024)
    p1 = p1.reshape(b, lp, P1_COLS)
    ps = ps.reshape(b, lp, LANES)

    yp, eq, ek = _gate_prep(ps, _row128(a_log, SM_A), _row128(dt_bias, SM_A),
                            _row128(fox_f_bias, SM_F), cst["ltri3"], cst["sq"], cst["sk"],
                            cst["cq"], cst["ck"])
    ya = _gdn(p1, yp, cst["sel"], cst["oneh"], conv_w.astype(F32), row(gdn_norm_g),
              cst["shift3"], seq=seq, hg=8)
    yb = _fox(p1, eq, ek, row(fox_q_norm_g), row(fox_k_norm_g), seq=seq, tq=1024)

    x2d = x.reshape(b * seq, d)
    h1 = _merge(x2d, row(mix_norm_g), ya.reshape(b * seq, qk), yb.reshape(b * seq, qk),
                wga, wgb, w_o_gdn.astype(BF16), w_o_fox.astype(BF16), w_out.astype(BF16),
                tm=512, tj=512)
    out = _mlp(h1, row(mlp_norm_g), w_up.astype(BF16), w_down.astype(BF16),
               row(final_norm_g), tm=512, tf=1024)
    return out.reshape(b, seq, d)


def kernel(x, meta_tokens, mix_norm_g, w_in, conv_w, a_log, dt_bias, gdn_norm_g, w_o_gdn,
           fox_q_norm_g, fox_k_norm_g, fox_f_bias, w_o_fox, w_out, mlp_norm_g, w_up, w_down,
           final_norm_g):
    assert w_in.shape[0] == 1, "single-layer block"
    return _layer(x, meta_tokens, mix_norm_g[0], w_in, conv_w[0], a_log[0], dt_bias[0],
                  gdn_norm_g[0], w_o_gdn[0], fox_q_norm_g[0], fox_k_norm_g[0], fox_f_bias[0],
                  w_o_fox[0], w_out[0], mlp_norm_g[0], w_up[0], w_down[0], final_norm_g)
```

```python
import functools
import math

import numpy as np
import jax
import jax.numpy as jnp
from jax import lax
from jax.experimental import pallas as pl
from jax.experimental.pallas import tpu as pltpu

F32 = jnp.float32
BF16 = jnp.bfloat16

NORM_EPS = 1e-6
N_META = 16
HEADS = 16
HEAD_DIM = 128
CONV_K = 4
LANES = 128
CHUNK = 128
FRONT_PAD = CHUNK - N_META
NEG_BIG = -1e30
LOG2E = math.log2(math.e)
VMEM_LIMIT = 56 * 1024 * 1024

COL_GDN_Q, COL_GDN_K, COL_GDN_V, COL_Z = 0, 16, 32, 48
COL_FOX_Q, COL_FOX_K, COL_FOX_V = 64, 80, 96
P1_COLS = 112 * LANES
SM_B, SM_A, SM_F = 0, 16, 32
GQ_G, GQ_BETA, GQ_EG, GQ_EKD, GQ_BEG = range(5)
N_GQ = 5
FOX_EXTRA = 8
FOX_M_LANE = 6
FOX_SAFE_BOUND = 40.0


def _params(*sem):
    return pltpu.CompilerParams(dimension_semantics=sem, vmem_limit_bytes=VMEM_LIMIT)


def _norm_rows(x, g):
    ms = jnp.mean(x * x, axis=-1, keepdims=True)
    return x * lax.rsqrt(ms + NORM_EPS) * g


def _split3(x):
    hi = x.astype(BF16)
    r = x - hi.astype(F32)
    mid = r.astype(BF16)
    lo = (r - mid.astype(F32)).astype(BF16)
    return hi, mid, lo


def _dot(a, b):
    return jnp.dot(a, b, preferred_element_type=F32)


def _dot_nt(a, b):
    return lax.dot_general(a, b, (((1,), (1,)), ((), ())), preferred_element_type=F32)


def _dot_tn(a, b):
    return lax.dot_general(a, b, (((0,), (0,)), ((), ())), preferred_element_type=F32)


def _sigmoid(x):
    return 1.0 / (1.0 + jnp.exp(-x))


def _silu(x):
    return x * _sigmoid(x)


def _softplus(x):
    return jnp.maximum(x, 0.0) + jnp.log1p(jnp.exp(-jnp.abs(x)))


def _prenorm_kernel(x_ref, meta_ref, g_ref, o_ref):
    r = pl.program_id(1)

    @pl.when(r == 0)
    def _():
        o_ref[0:FRONT_PAD, :] = jnp.zeros((FRONT_PAD, o_ref.shape[1]), o_ref.dtype)
        o_ref[FRONT_PAD:CHUNK, :] = _norm_rows(meta_ref[...], g_ref[...]).astype(o_ref.dtype)

    @pl.when(r > 0)
    def _():
        o_ref[...] = _norm_rows(x_ref[...], g_ref[...]).astype(o_ref.dtype)


def _prenorm(x, meta_tokens, g):
    b, seq, d = x.shape
    lp = CHUNK + seq
    return pl.pallas_call(
        _prenorm_kernel,
        grid=(b, lp // CHUNK),
        in_specs=[
            pl.BlockSpec((None, CHUNK, d), lambda bi, r: (bi, jnp.maximum(r - 1, 0), 0)),
            pl.BlockSpec((N_META, d), lambda bi, r: (0, 0)),
            pl.BlockSpec((1, d), lambda bi, r: (0, 0)),
        ],
        out_specs=pl.BlockSpec((None, CHUNK, d), lambda bi, r: (bi, r, 0)),
        out_shape=jax.ShapeDtypeStruct((b, lp, d), BF16),
        compiler_params=_params("parallel", "arbitrary"),
        name="prenorm",
    )(x, meta_tokens, g)


def _inproj_kernel(x_ref, w_ref, wsa_ref, wsb_ref, o_ref, os_ref):
    @pl.when(pl.program_id(1) == 0)
    def _():
        pad = LANES - wsa_ref.shape[0] - wsb_ref.shape[0]
        ws = jnp.concatenate([wsa_ref[...], wsb_ref[...], jnp.zeros((pad, wsa_ref.shape[1]), F32)],
                             axis=0)
        os_ref[...] = _dot_nt(x_ref[...], ws.astype(BF16))

    o_ref[...] = _dot_nt(x_ref[...], w_ref[...].astype(BF16)).astype(o_ref.dtype)


F32_SUBLANES = 8


def _element_rows(tile, d, start_tiles):
    return pl.BlockSpec((None, pl.Element(tile), pl.Element(d)),
                        lambda *ids: (0, F32_SUBLANES * start_tiles(*ids), 0))


def _inproj(hn, wt, n_a, b_start, n_b, small, *, tm, tn):
    m, d = hn.shape
    na, nb = n_a // tn, n_b // tn
    (s0, c0), (s1, c1) = small
    sub = F32_SUBLANES
    assert all(v % sub == 0 for v in (tn, b_start - n_a, s0, s1))

    def w_rows(i, j):
        return j * (tn // sub) + jnp.where(j >= na, (b_start - n_a) // sub, 0)

    return pl.pallas_call(
        _inproj_kernel,
        grid=(m // tm, na + nb),
        in_specs=[
            pl.BlockSpec((tm, d), lambda i, j: (i, 0)),
            _element_rows(tn, d, w_rows),
            _element_rows(c0, d, lambda i, j: s0 // sub),
            _element_rows(c1, d, lambda i, j: s1 // sub),
        ],
        out_specs=[
            pl.BlockSpec((tm, tn), lambda i, j: (i, j)),
            pl.BlockSpec((tm, LANES), lambda i, j: (i, 0)),
        ],
        out_shape=[
            jax.ShapeDtypeStruct((m, (na + nb) * tn), BF16),
            jax.ShapeDtypeStruct((m, LANES), F32),
        ],
        compiler_params=_params("parallel", "arbitrary"),
        name="inproj",
    )(hn, wt, wt, wt)


def _cast_rows_kernel(w_ref, o_ref):
    o_ref[...] = w_ref[...].astype(o_ref.dtype)


def _cast_rows(wt, start, nrows, *, tr):
    d = wt.shape[2]
    return pl.pallas_call(
        _cast_rows_kernel,
        grid=(nrows // tr,),
        in_specs=[_element_rows(tr, d, lambda i: start // F32_SUBLANES + i * (tr // F32_SUBLANES))],
        out_specs=pl.BlockSpec((tr, d), lambda i: (i, 0)),
        out_shape=jax.ShapeDtypeStruct((nrows, d), BF16),
        compiler_params=_params("parallel"),
        name="cast_rows",
    )(wt)


def _gate_prep_kernel(ps_ref, alog_ref, dtb_ref, fb_ref, ltri_ref, sq_ref, sk_ref, cq_ref, ck_ref,
                      yp_ref, eq_ref, ek_ref, *, nblk):
    lane = lax.broadcasted_iota(jnp.int32, (CHUNK, LANES), 1)
    rowi = lax.broadcasted_iota(jnp.int32, (CHUNK, LANES), 0)

    def body(i, carry):
        rows = pl.ds(pl.multiple_of(i * CHUNK, CHUNK), CHUNK)
        x = ps_ref[rows, :]
        valid = (i * CHUNK + rowi) >= FRONT_PAD

        beta = jnp.where(valid, _sigmoid(x), 0.0)
        gstep = jnp.where(valid, -jnp.exp(alog_ref[...]) * _softplus(x + dtb_ref[...]), 0.0)
        gcum = _dot(ltri_ref[...], jnp.concatenate(_split3(gstep), axis=0))
        eg = jnp.exp(gcum)
        ekd = jnp.exp(gcum[CHUNK - 1:CHUNK, :] - gcum)
        beg = pltpu.roll(beta, SM_A - SM_B, 1) * eg
        y = jnp.where(lane < 16 * (GQ_EKD + 1), pltpu.roll(ekd, 16 * GQ_EKD - SM_A, 1),
                      pltpu.roll(beg, 16 * GQ_BEG - SM_A, 1))
        y = jnp.where(lane < 16 * (GQ_EG + 1), pltpu.roll(eg, 16 * GQ_EG - SM_A, 1), y)
        y = jnp.where(lane < 16 * (GQ_BETA + 1), pltpu.roll(beta, 16 * GQ_BETA - SM_B, 1), y)
        y = jnp.where(lane < 16 * (GQ_G + 1), pltpu.roll(gcum, LANES + 16 * GQ_G - SM_A, 1), y)
        yp_ref[rows, :] = jnp.concatenate(_split3(y), axis=1)

        xf = x + fb_ref[...]
        ls = (jnp.minimum(xf, 0.0) - jnp.log1p(jnp.exp(-jnp.abs(xf)))) * LOG2E
        cum = _dot(ltri_ref[...], jnp.concatenate(_split3(ls), axis=0)) + carry
        cp = jnp.concatenate(_split3(cum), axis=1)
        eq_ref[rows, :] = (_dot(cp, sq_ref[...]) + cq_ref[...]).astype(BF16)
        ek_ref[rows, :] = (_dot(cp, sk_ref[...]) + ck_ref[...]).astype(BF16)
        return cum[CHUNK - 1:CHUNK, :]

    lax.fori_loop(0, nblk, body, jnp.zeros((1, LANES), F32))


def _gate_prep(ps, alog_row, dtb_row, fb_row, ltri3, sq, sk, cq, ck):
    b, lp, _ = ps.shape
    const = lambda bi: (0, 0)
    return pl.pallas_call(
        functools.partial(_gate_prep_kernel, nblk=lp // CHUNK),
        grid=(b,),
        in_specs=[
            pl.BlockSpec((None, lp, LANES), lambda bi: (bi, 0, 0)),
            pl.BlockSpec((1, LANES), const),
            pl.BlockSpec((1, LANES), const),
            pl.BlockSpec((1, LANES), const),
            pl.BlockSpec((CHUNK, 3 * CHUNK), const),
            pl.BlockSpec((3 * LANES, LANES), const),
            pl.BlockSpec((3 * LANES, LANES), const),
            pl.BlockSpec((1, LANES), const),
            pl.BlockSpec((1, LANES), const),
        ],
        out_specs=[
            pl.BlockSpec((None, lp, 3 * LANES), lambda bi: (bi, 0, 0)),
            pl.BlockSpec((None, lp, LANES), lambda bi: (bi, 0, 0)),
            pl.BlockSpec((None, lp, LANES), lambda bi: (bi, 0, 0)),
        ],
        out_shape=[
            jax.ShapeDtypeStruct((b, lp, 3 * LANES), BF16),
            jax.ShapeDtypeStruct((b, lp, LANES), BF16),
            jax.ShapeDtypeStruct((b, lp, LANES), BF16),
        ],
        compiler_params=_params("parallel"),
        name="gate_prep",
    )(ps, alog_row, dtb_row, fb_row, ltri3, sq, sk, cq, ck)


def _gdn_kernel(q_ref, k_ref, v_ref, z_ref, yp_ref, sel_ref, oneh_ref, wq_ref, wk_ref, wv_ref,
                gn_ref, shift_ref, o_ref, s_ref, prev_ref, *, hg):
    c = pl.program_id(2)

    @pl.when(c == 0)
    def _():
        s_ref[...] = jnp.zeros_like(s_ref)
        prev_ref[...] = jnp.zeros_like(prev_ref)

    row = lax.broadcasted_iota(jnp.int32, (CHUNK, CHUNK), 0)
    col = lax.broadcasted_iota(jnp.int32, (CHUNK, CHUNK), 1)
    tril = row >= col
    strict = row > col
    heads = range(hg)
    hsl = lambda a, u: a[:, u * HEAD_DIM:(u + 1) * HEAD_DIM]

    yp = yp_ref[...]
    gb = _dot(yp, sel_ref[...])
    g_rows = _dot_nt(oneh_ref[...], yp)
    gq = lambda u, q: gb[:, (u * N_GQ + q) * LANES:(u * N_GQ + q + 1) * LANES]

    def conv_act(x_ref, w_ref, t):
        x16 = x_ref[...]
        x2 = jnp.concatenate([prev_ref[t], x16], axis=0)
        prev_ref[t] = x16
        sh = _dot(shift_ref[...], x2)
        w = w_ref[...]
        acc = x16.astype(F32) * w[CONV_K - 1:CONV_K, :]
        for s in range(1, CONV_K):
            acc = acc + sh[(s - 1) * CHUNK:s * CHUNK, :] * w[CONV_K - 1 - s:CONV_K - s, :]
        return _silu(acc)

    qa = conv_act(q_ref, wq_ref, 0)
    ka = conv_act(k_ref, wk_ref, 1)
    va = conv_act(v_ref, wv_ref, 2)

    def l2n(x):
        return x * lax.rsqrt(jnp.sum(x * x, axis=-1, keepdims=True) + NORM_EPS)

    qn = [l2n(hsl(qa, u)) * (HEAD_DIM ** -0.5) for u in heads]
    kn = [l2n(hsl(ka, u)) for u in heads]
    k16 = [kn[u].astype(BF16) for u in heads]

    state = [s_ref[u] for u in heads]
    r1 = [_dot(jnp.concatenate([kn[u] * gq(u, GQ_BEG), qn[u] * gq(u, GQ_EG)], axis=0).astype(BF16),
               state[u].astype(BF16)) for u in heads]

    kq = [_dot_nt(jnp.concatenate([kn[u] * gq(u, GQ_BETA), qn[u]], axis=0).astype(BF16), k16[u])
          for u in heads]
    ediff = [jnp.exp(jnp.minimum(gq(u, GQ_G) - g_rows[u * CHUNK:(u + 1) * CHUNK, :], 0.0))
             for u in heads]
    a = [kq[u][:CHUNK] * jnp.where(strict, ediff[u], 0.0) for u in heads]
    qk_m = [(kq[u][CHUNK:] * jnp.where(tril, ediff[u], 0.0)).astype(BF16) for u in heads]

    a16 = [a[u].astype(BF16) for u in heads]
    yk = [_dot(a16[u], a16[u]) for u in heads]
    sk = [-a[u] for u in heads]
    nlev = int(np.log2(CHUNK)) - 1
    for lev in range(nlev):
        yk16 = [yk[u].astype(BF16) for u in heads]
        if lev < nlev - 1:
            zz = [_dot(jnp.concatenate([sk[u], yk[u]], axis=0).astype(BF16), yk16[u]) for u in heads]
            sk = [sk[u] + yk[u] + zz[u][:CHUNK] for u in heads]
            yk = [zz[u][CHUNK:] for u in heads]
        else:
            sk = [sk[u] + yk[u] + _dot(sk[u].astype(BF16), yk16[u]) for u in heads]

    rhs = [hsl(va, u) * gq(u, GQ_BETA) - r1[u][:CHUNK] for u in heads]
    v_new = [(rhs[u] + _dot(sk[u].astype(BF16), rhs[u].astype(BF16))).astype(BF16) for u in heads]
    o = [r1[u][CHUNK:] + _dot(qk_m[u], v_new[u]) for u in heads]
    for u in heads:
        eg_last = gq(u, GQ_EG)[CHUNK - 1:CHUNK, :]
        s_ref[u] = state[u] * eg_last + _dot_tn((kn[u] * gq(u, GQ_EKD)).astype(BF16), v_new[u])

    on = jnp.concatenate([_norm_rows(o[u], gn_ref[...]) for u in heads], axis=1)
    o_ref[...] = (on * _silu(z_ref[...].astype(F32))).astype(o_ref.dtype)


def _gdn(p1, yp, sel, oneh, conv_w, gnorm, shift3, *, seq, hg):
    b, lp, _ = p1.shape
    nc = lp // CHUNK
    gw = hg * HEAD_DIM
    ng = HEADS // hg
    col = lambda base: (lambda bi, gi, ci: (bi, ci, base * LANES // gw + gi))
    wcol = lambda base: (lambda bi, gi, ci: (0, base * LANES // gw + gi))
    const = lambda bi, gi, ci: (0, 0)
    return pl.pallas_call(
        functools.partial(_gdn_kernel, hg=hg),
        grid=(b, ng, nc),
        in_specs=[
            pl.BlockSpec((None, CHUNK, gw), col(COL_GDN_Q)),
            pl.BlockSpec((None, CHUNK, gw), col(COL_GDN_K)),
            pl.BlockSpec((None, CHUNK, gw), col(COL_GDN_V)),
            pl.BlockSpec((None, CHUNK, gw), col(COL_Z)),
            pl.BlockSpec((None, CHUNK, 3 * LANES), lambda bi, gi, ci: (bi, ci, 0)),
            pl.BlockSpec((3 * LANES, hg * N_GQ * LANES), lambda bi, gi, ci: (0, gi)),
            pl.BlockSpec((hg * CHUNK, 3 * LANES), lambda bi, gi, ci: (gi, 0)),
            pl.BlockSpec((CONV_K, gw), wcol(COL_GDN_Q)),
            pl.BlockSpec((CONV_K, gw), wcol(COL_GDN_K)),
            pl.BlockSpec((CONV_K, gw), wcol(COL_GDN_V)),
            pl.BlockSpec((1, HEAD_DIM), const),
            pl.BlockSpec((3 * CHUNK, 2 * CHUNK), const),
        ],
        out_specs=pl.BlockSpec((None, CHUNK, gw),
                               lambda bi, gi, ci: (bi, jnp.maximum(ci - 1, 0), gi)),
        out_shape=jax.ShapeDtypeStruct((b, seq, HEADS * HEAD_DIM), BF16),
        scratch_shapes=[
            pltpu.VMEM((hg, HEAD_DIM, HEAD_DIM), F32),
            pltpu.VMEM((3, CHUNK, gw), BF16),
        ],
        compiler_params=_params("parallel", "parallel", "arbitrary"),
        name="gdn",
    )(p1, p1, p1, p1, yp, sel, oneh, conv_w, conv_w, conv_w, gnorm, shift3)


def _fox_kernel(q_ref, k_ref, v_ref, eq_ref, ek_ref, gq_ref, gk_ref, o_ref,
                kaug_ref, vaug_ref, *, tq, lp):
    h = pl.program_id(1)
    i = pl.program_id(2)

    @pl.when(i == 0)
    def _():
        def build(rows, n):
            kn = _norm_rows(k_ref[rows, :].astype(F32), gk_ref[...])
            kaug_ref[rows, 0:HEAD_DIM] = kn.astype(BF16)
            kaug_ref[rows, HEAD_DIM:2 * HEAD_DIM] = ek_ref[rows, :]
            vaug_ref[rows, 0:HEAD_DIM] = v_ref[rows, :]
            vaug_ref[rows, HEAD_DIM:2 * HEAD_DIM] = jnp.ones((n, HEAD_DIM), BF16)

        def body(r, c):
            build(pl.ds(pl.multiple_of(r * tq, 128), tq), tq)
            return c
        lax.fori_loop(0, lp // tq, body, 0)
        if lp % tq:
            build(pl.ds(lp - lp % tq, lp % tq), lp % tq)

    q0 = pl.multiple_of(CHUNK + i * tq, 128)
    rows_q = pl.ds(q0, tq)
    qn = _norm_rows(q_ref[rows_q, :].astype(F32), gq_ref[...]) * (HEAD_DIM ** -0.5 * LOG2E)
    lane = lax.broadcasted_iota(jnp.int32, (tq, LANES), 1)
    mine = lax.shift_right_logical(lane, 3) == h
    eqm = jnp.where(mine, eq_ref[rows_q, :], jnp.zeros((), BF16))
    q16 = qn.astype(BF16)

    gmax = (jnp.max(jnp.abs(gq_ref[...]), axis=-1, keepdims=True)
            * jnp.max(jnp.abs(gk_ref[...]), axis=-1, keepdims=True))
    bound = (gmax * (HEAD_DIM ** 0.5 * LOG2E * 1.02)).astype(BF16)
    safe = bound.astype(F32)[0, 0] <= FOX_SAFE_BOUND

    rr = lax.broadcasted_iota(jnp.int32, (tq, tq), 0)
    cc = lax.broadcasted_iota(jnp.int32, (tq, tq), 1)
    causal = rr >= cc

    def blocks(j):
        return pl.ds(pl.multiple_of(CHUNK + j * tq, 128), tq)

    @pl.when(safe)
    def _():
        eqb = jnp.where(lane == FOX_EXTRA * h + FOX_M_LANE, -bound, eqm)
        qaug = jnp.concatenate([q16, eqb], axis=1)
        half = tq // 2

        def probs(qa, krows, mask=None):
            s = _dot_nt(qa, kaug_ref[krows, :])
            if mask is not None:
                s = jnp.where(mask, s, NEG_BIG)
            return jnp.exp2(s).astype(BF16)

        def finish(acc, rows):
            o_ref[rows, :] = (acc[:, :HEAD_DIM] / acc[:, HEAD_DIM:]).astype(o_ref.dtype)

        for ii in range(pl.cdiv(lp - CHUNK, tq)):
            @pl.when(i == ii)
            def _(ii=ii):
                acc = _dot(probs(qaug, slice(0, CHUNK), lane >= FRONT_PAD), vaug_ref[0:CHUNK, :])
                for j in range(ii):
                    kr = slice(CHUNK + j * tq, CHUNK + (j + 1) * tq)
                    acc = acc + _dot(probs(qaug, kr), vaug_ref[kr, :])
                k_lo = slice(CHUNK + ii * tq, CHUNK + ii * tq + half)
                k_all = slice(CHUNK + ii * tq, CHUNK + (ii + 1) * tq)
                p_lo = probs(qaug[:half], k_lo, causal[:half, :half])
                p_hi = probs(qaug[half:], k_all, causal[half:, :])
                finish(acc[:half] + _dot(p_lo, vaug_ref[k_lo, :]), slice(0, half))
                finish(acc[half:] + _dot(p_hi, vaug_ref[k_all, :]), slice(half, tq))

    @pl.when(jnp.logical_not(safe))
    def _():
        qaug = jnp.concatenate([q16, eqm], axis=1)
        s = _dot_nt(qaug, kaug_ref[0:CHUNK, :])
        s = jnp.where(lane >= FRONT_PAD, s, NEG_BIG)
        m = jnp.max(s, axis=-1, keepdims=True)
        acc = _dot(jnp.exp2(s - m).astype(BF16), vaug_ref[0:CHUNK, :])

        def update(s, vrows, m, acc):
            m_new = jnp.maximum(m, jnp.max(s, axis=-1, keepdims=True))
            p = jnp.exp2(s - m_new)
            return m_new, jnp.exp2(m - m_new) * acc + _dot(p.astype(BF16), vrows)

        def body(j, carry):
            s = _dot_nt(qaug, kaug_ref[blocks(j), :])
            return update(s, vaug_ref[blocks(j), :], *carry)
        m, acc = lax.fori_loop(0, i, body, (m, acc))

        s = _dot_nt(qaug, kaug_ref[rows_q, :])
        m, acc = update(jnp.where(causal, s, NEG_BIG), vaug_ref[rows_q, :], m, acc)
        o_ref[...] = (acc[:, :HEAD_DIM] / acc[:, HEAD_DIM:]).astype(o_ref.dtype)


def _fox(p1, eq, ek, gq, gk, *, seq, tq):
    b, lp, _ = p1.shape
    colmap = lambda base: (lambda bi, hi, qi: (bi, 0, base + hi))
    const = lambda bi, hi, qi: (0, 0)
    return pl.pallas_call(
        functools.partial(_fox_kernel, tq=tq, lp=lp),
        grid=(b, HEADS, seq // tq),
        in_specs=[
            pl.BlockSpec((None, lp, HEAD_DIM), colmap(COL_FOX_Q)),
            pl.BlockSpec((None, lp, HEAD_DIM), colmap(COL_FOX_K)),
            pl.BlockSpec((None, lp, HEAD_DIM), colmap(COL_FOX_V)),
            pl.BlockSpec((None, lp, LANES), lambda bi, hi, qi: (bi, 0, 0)),
            pl.BlockSpec((None, lp, LANES), lambda bi, hi, qi: (bi, 0, 0)),
            pl.BlockSpec((1, HEAD_DIM), const),
            pl.BlockSpec((1, HEAD_DIM), const),
        ],
        out_specs=pl.BlockSpec((None, tq, HEAD_DIM), lambda bi, hi, qi: (bi, qi, hi)),
        out_shape=jax.ShapeDtypeStruct((b, seq, HEADS * HEAD_DIM), BF16),
        scratch_shapes=[
            pltpu.VMEM((lp, 2 * HEAD_DIM), BF16),
            pltpu.VMEM((lp, 2 * HEAD_DIM), BF16),
        ],
        compiler_params=_params("parallel", "parallel", "arbitrary"),
        name="fox",
    )(p1, p1, p1, eq, ek, gq, gk)


def _merge_kernel(x_ref, g_ref, ya_ref, yb_ref, wga_ref, wgb_ref, wa_ref, wb_ref, wo_ref,
                  o_ref, u_ref):
    j = pl.program_id(1)

    @pl.when(j == 0)
    def _():
        x = x_ref[...]
        u_ref[...] = _norm_rows(x, g_ref[...]).astype(BF16)
        o_ref[...] = x

    u = u_ref[...]
    ga = _dot_nt(u, wga_ref[...])
    gb = _dot_nt(u, wgb_ref[...])
    pa = _dot(ya_ref[...], wa_ref[...])
    pb = _dot(yb_ref[...], wb_ref[...])
    mix = _sigmoid(ga) * pa + _sigmoid(gb) * pb
    o_ref[...] += _dot(mix.astype(BF16), wo_ref[...])


def _merge(x2d, g, ya, yb, wg_t, wa, wb, wo, *, tm, tj):
    m, d = x2d.shape
    rowblk = lambda i, j: (i, 0)
    colblk = lambda i, j: (0, j)
    return pl.pallas_call(
        _merge_kernel,
        grid=(m // tm, d // tj),
        in_specs=[
            pl.BlockSpec((tm, d), rowblk),
            pl.BlockSpec((1, d), lambda i, j: (0, 0)),
            pl.BlockSpec((tm, d), rowblk),
            pl.BlockSpec((tm, d), rowblk),
            pl.BlockSpec((tj, d), lambda i, j: (j, 0)),
            pl.BlockSpec((tj, d), lambda i, j: (j + d // tj, 0)),
            pl.BlockSpec((d, tj), colblk),
            pl.BlockSpec((d, tj), colblk),
            pl.BlockSpec((tj, d), lambda i, j: (j, 0)),
        ],
        out_specs=pl.BlockSpec((tm, d), rowblk),
        out_shape=jax.ShapeDtypeStruct((m, d), F32),
        scratch_shapes=[pltpu.VMEM((tm, d), BF16)],
        compiler_params=_params("parallel", "arbitrary"),
        name="merge",
    )(x2d, g, ya, yb, wg_t, wg_t, wa, wb, wo)


def _mlp_kernel(h_ref, g_ref, wu_ref, wd_ref, gf_ref, o_ref, u_ref):
    f = pl.program_id(1)

    @pl.when(f == 0)
    def _():
        x = h_ref[...]
        u_ref[...] = _norm_rows(x, g_ref[...]).astype(BF16)
        o_ref[...] = x

    a = jnp.maximum(_dot(u_ref[...], wu_ref[...]), 0.0)
    o_ref[...] += _dot((a * a).astype(BF16), wd_ref[...])

    @pl.when(f == pl.num_programs(1) - 1)
    def _():
        o_ref[...] = _norm_rows(o_ref[...], gf_ref[...])


def _mlp(h2d, g, wu, wd, gf, *, tm, tf):
    m, d = h2d.shape
    dff = wu.shape[1]
    rowblk = lambda i, f: (i, 0)
    return pl.pallas_call(
        _mlp_kernel,
        grid=(m // tm, dff // tf),
        in_specs=[
            pl.BlockSpec((tm, d), rowblk),
            pl.BlockSpec((1, d), lambda i, f: (0, 0)),
            pl.BlockSpec((d, tf), lambda i, f: (0, f)),
            pl.BlockSpec((tf, d), lambda i, f: (f, 0)),
            pl.BlockSpec((1, d), lambda i, f: (0, 0)),
        ],
        out_specs=pl.BlockSpec((tm, d), rowblk),
        out_shape=jax.ShapeDtypeStruct((m, d), F32),
        scratch_shapes=[pltpu.VMEM((tm, d), BF16)],
        compiler_params=_params("parallel", "arbitrary"),
        name="mlp",
    )(h2d, g, wu, wd, gf)


def _constants():
    r = np.arange(128)
    ltri = (r[:, None] >= r[None, :]).astype(np.float32)
    ltri3 = np.concatenate([ltri, ltri, ltri], axis=1)

    sq = np.zeros((3 * LANES, LANES), np.float32)
    sk = np.zeros((3 * LANES, LANES), np.float32)
    cq = np.zeros((1, LANES), np.float32)
    ck = np.zeros((1, LANES), np.float32)
    for h in range(HEADS):
        for p in range(3):
            sq[p * LANES + SM_F + h, FOX_EXTRA * h + p] = 1.0
            sk[p * LANES + SM_F + h, FOX_EXTRA * h + 3 + p] = -1.0
            cq[0, FOX_EXTRA * h + 3 + p] = 1.0
            ck[0, FOX_EXTRA * h + p] = 1.0
        ck[0, FOX_EXTRA * h + FOX_M_LANE] = 1.0

    shift3 = np.zeros((3 * CHUNK, 2 * CHUNK), np.float32)
    for s in range(1, CONV_K):
        shift3[(s - 1) * CHUNK + r, CHUNK + r - s] = 1.0

    src_lane = np.zeros((HEADS * N_GQ * LANES,), np.int64)
    for h in range(HEADS):
        for q in range(N_GQ):
            src_lane[(h * N_GQ + q) * LANES:(h * N_GQ + q + 1) * LANES] = 16 * q + h
    rows3 = np.arange(3 * LANES) % LANES
    sel = (rows3[:, None] == src_lane[None, :]).astype(np.float32)
    oneh = (np.repeat(16 * GQ_G + np.arange(HEADS), CHUNK)[:, None] == rows3[None, :]).astype(np.float32)

    as_bf = lambda a: jnp.asarray(a, BF16)
    return dict(ltri3=as_bf(ltri3), sq=as_bf(sq), sk=as_bf(sk), cq=jnp.asarray(cq),
                ck=jnp.asarray(ck), shift3=as_bf(shift3), sel=as_bf(sel), oneh=as_bf(oneh))


def _row128(vec, offset):
    return jnp.zeros((1, LANES), F32).at[0, offset:offset + HEADS].set(vec.astype(F32))


def _pick_rows_tile(total, target):
    best = 128
    for t in range(128, target + 1, 128):
        if total % t == 0:
            best = t
    return best


def _layer(x, meta_tokens, mix_norm_g, w_in, conv_w, a_log, dt_bias, gdn_norm_g, w_o_gdn,
           fox_q_norm_g, fox_k_norm_g, fox_f_bias, w_o_fox, w_out, mlp_norm_g, w_up, w_down,
           final_norm_g):
    b, seq, d = x.shape
    lp = CHUNK + seq
    qk = HEADS * HEAD_DIM

    o_z = 3 * qk
    o_b = o_z + qk
    o_a = o_b + HEADS
    o_fq = o_a + HEADS
    o_f = o_fq + 3 * qk
    o_ga = o_f + HEADS
    o_gb = o_ga + d

    wt = jnp.swapaxes(w_in, 1, 2)
    row = lambda v: v.reshape(1, -1).astype(F32)
    cst = _constants()

    hn = _prenorm(x, meta_tokens.astype(x.dtype), row(mix_norm_g))
    p1, ps = _inproj(hn.reshape(b * lp, d), wt, o_b, o_fq, o_f - o_fq,
                     ((o_b, o_fq - o_b), (o_f, o_ga - o_f)),
                     tm=_pick_rows_tile(lp, 1408), tn=1024)
    p1 = p1.reshape(b, lp, P1_COLS)
    ps = ps.reshape(b, lp, LANES)

    yp, eq, ek = _gate_prep(ps, _row128(a_log, SM_A), _row128(dt_bias, SM_A),
                            _row128(fox_f_bias, SM_F), cst["ltri3"], cst["sq"], cst["sk"],
                            cst["cq"], cst["ck"])
    ya = _gdn(p1, yp, cst["sel"], cst["oneh"], conv_w.astype(F32), row(gdn_norm_g),
              cst["shift3"], seq=seq, hg=8)
    yb = _fox(p1, eq, ek, row(fox_q_norm_g), row(fox_k_norm_g), seq=seq, tq=1024)

    x2d = x.reshape(b * seq, d)
    h1 = _merge(x2d, row(mix_norm_g), ya.reshape(b * seq, qk), yb.reshape(b * seq, qk),
                _cast_rows(wt, o_ga, 2 * d, tr=512), w_o_gdn.astype(BF16),
                w_o_fox.astype(BF16), w_out.astype(BF16),
                tm=512, tj=512)
    out = _mlp(h1, row(mlp_norm_g), w_up.astype(BF16), w_down.astype(BF16),
               row(final_norm_g), tm=512, tf=1024)
    return out.reshape(b, seq, d)


def kernel(x, meta_tokens, mix_norm_g, w_in, conv_w, a_log, dt_bias, gdn_norm_g, w_o_gdn,
           fox_q_norm_g, fox_k_norm_g, fox_f_bias, w_o_fox, w_out, mlp_norm_g, w_up, w_down,
           final_norm_g):
    assert w_in.shape[0] == 1, "single-layer block"
    return _layer(x, meta_tokens, mix_norm_g[0], w_in, conv_w[0], a_log[0], dt_bias[0],
                  gdn_norm_g[0], w_o_gdn[0], fox_q_norm_g[0], fox_k_norm_g[0], fox_f_bias[0],
                  w_o_fox[0], w_out[0], mlp_norm_g[0], w_up[0], w_down[0], final_norm_g)
```

```python
import functools
import math

import numpy as np
import jax
import jax.numpy as jnp
from jax import lax
from jax.experimental import pallas as pl
from jax.experimental.pallas import tpu as pltpu

F32 = jnp.float32
BF16 = jnp.bfloat16

NORM_EPS = 1e-6
N_META = 16
HEADS = 16
HEAD_DIM = 128
CONV_K = 4
LANES = 128
CHUNK = 128
FRONT_PAD = CHUNK - N_META
NEG_BIG = -1e30
LOG2E = math.log2(math.e)
VMEM_LIMIT = 56 * 1024 * 1024

COL_GDN_Q, COL_GDN_K, COL_GDN_V, COL_Z = 0, 16, 32, 48
COL_FOX_Q, COL_FOX_K, COL_FOX_V = 64, 80, 96
P1_COLS = 112 * LANES
SM_B, SM_A, SM_F = 0, 16, 32
GQ_G, GQ_BETA = range(2)
N_GQ = 2
GQ_PIECES = 2
ROW_REP = 8
FOX_EXTRA = 8
FOX_M_LANE = 6
FOX_SAFE_BOUND = 40.0


def _params(*sem):
    return pltpu.CompilerParams(dimension_semantics=sem, vmem_limit_bytes=VMEM_LIMIT)


def _norm_rows(x, g):
    ms = jnp.mean(x * x, axis=-1, keepdims=True)
    return x * lax.rsqrt(ms + NORM_EPS) * g


def _split3(x):
    hi = x.astype(BF16)
    r = x - hi.astype(F32)
    mid = r.astype(BF16)
    lo = (r - mid.astype(F32)).astype(BF16)
    return hi, mid, lo


def _dot(a, b):
    return jnp.dot(a, b, preferred_element_type=F32)


def _dot_nt(a, b):
    return lax.dot_general(a, b, (((1,), (1,)), ((), ())), preferred_element_type=F32)


def _dot_tn(a, b):
    return lax.dot_general(a, b, (((0,), (0,)), ((), ())), preferred_element_type=F32)


def _sigmoid(x):
    return 1.0 / (1.0 + jnp.exp(-x))


def _silu(x):
    return x * _sigmoid(x)


def _softplus(x):
    return jnp.maximum(x, 0.0) + jnp.log1p(jnp.exp(-jnp.abs(x)))


def _prenorm_kernel(x_ref, meta_ref, g_ref, o_ref):
    r = pl.program_id(1)

    @pl.when(r == 0)
    def _():
        o_ref[0:FRONT_PAD, :] = jnp.zeros((FRONT_PAD, o_ref.shape[1]), o_ref.dtype)
        o_ref[FRONT_PAD:CHUNK, :] = _norm_rows(meta_ref[...], g_ref[...]).astype(o_ref.dtype)

    @pl.when(r > 0)
    def _():
        o_ref[...] = _norm_rows(x_ref[...], g_ref[...]).astype(o_ref.dtype)


def _prenorm(x, meta_tokens, g):
    b, seq, d = x.shape
    lp = CHUNK + seq
    return pl.pallas_call(
        _prenorm_kernel,
        grid=(b, lp // CHUNK),
        in_specs=[
            pl.BlockSpec((None, CHUNK, d), lambda bi, r: (bi, jnp.maximum(r - 1, 0), 0)),
            pl.BlockSpec((N_META, d), lambda bi, r: (0, 0)),
            pl.BlockSpec((1, d), lambda bi, r: (0, 0)),
        ],
        out_specs=pl.BlockSpec((None, CHUNK, d), lambda bi, r: (bi, r, 0)),
        out_shape=jax.ShapeDtypeStruct((b, lp, d), BF16),
        compiler_params=_params("parallel", "arbitrary"),
        name="prenorm",
    )(x, meta_tokens, g)


def _inproj_kernel(x_ref, w_ref, wsa_ref, wsb_ref, o_ref, os_ref):
    @pl.when(pl.program_id(1) == 0)
    def _():
        pad = LANES - wsa_ref.shape[0] - wsb_ref.shape[0]
        ws = jnp.concatenate([wsa_ref[...], wsb_ref[...], jnp.zeros((pad, wsa_ref.shape[1]), F32)],
                             axis=0)
        os_ref[...] = _dot_nt(x_ref[...], ws.astype(BF16))

    o_ref[...] = _dot_nt(x_ref[...], w_ref[...].astype(BF16)).astype(o_ref.dtype)


F32_SUBLANES = 8


def _element_rows(tile, d, start_tiles):
    return pl.BlockSpec((None, pl.Element(tile), pl.Element(d)),
                        lambda *ids: (0, F32_SUBLANES * start_tiles(*ids), 0))


def _inproj(hn, wt, n_a, b_start, n_b, small, *, tm, tn):
    m, d = hn.shape
    na, nb = n_a // tn, n_b // tn
    (s0, c0), (s1, c1) = small
    sub = F32_SUBLANES
    assert all(v % sub == 0 for v in (tn, b_start - n_a, s0, s1))

    def w_rows(i, j):
        return j * (tn // sub) + jnp.where(j >= na, (b_start - n_a) // sub, 0)

    return pl.pallas_call(
        _inproj_kernel,
        grid=(m // tm, na + nb),
        in_specs=[
            pl.BlockSpec((tm, d), lambda i, j: (i, 0)),
            _element_rows(tn, d, w_rows),
            _element_rows(c0, d, lambda i, j: s0 // sub),
            _element_rows(c1, d, lambda i, j: s1 // sub),
        ],
        out_specs=[
            pl.BlockSpec((tm, tn), lambda i, j: (i, j)),
            pl.BlockSpec((tm, LANES), lambda i, j: (i, 0)),
        ],
        out_shape=[
            jax.ShapeDtypeStruct((m, (na + nb) * tn), BF16),
            jax.ShapeDtypeStruct((m, LANES), F32),
        ],
        compiler_params=_params("parallel", "arbitrary"),
        name="inproj",
    )(hn, wt, wt, wt)


def _cast_rows_kernel(w_ref, o_ref):
    o_ref[...] = w_ref[...].astype(o_ref.dtype)


def _cast_rows(wt, start, nrows, *, tr):
    d = wt.shape[2]
    return pl.pallas_call(
        _cast_rows_kernel,
        grid=(nrows // tr,),
        in_specs=[_element_rows(tr, d, lambda i: start // F32_SUBLANES + i * (tr // F32_SUBLANES))],
        out_specs=pl.BlockSpec((tr, d), lambda i: (i, 0)),
        out_shape=jax.ShapeDtypeStruct((nrows, d), BF16),
        compiler_params=_params("parallel"),
        name="cast_rows",
    )(wt)


def _gate_prep_kernel(ps_ref, alog_ref, dtb_ref, fb_ref, ltri_ref, sq_ref, sk_ref, cq_ref, ck_ref,
                      yp_ref, eq_ref, ek_ref, *, nblk):
    lane = lax.broadcasted_iota(jnp.int32, (CHUNK, LANES), 1)
    rowi = lax.broadcasted_iota(jnp.int32, (CHUNK, LANES), 0)

    def body(i, carry):
        rows = pl.ds(pl.multiple_of(i * CHUNK, CHUNK), CHUNK)
        x = ps_ref[rows, :]
        valid = (i * CHUNK + rowi) >= FRONT_PAD

        beta = jnp.where(valid, _sigmoid(x), 0.0)
        gstep = jnp.where(valid, -jnp.exp(alog_ref[...]) * _softplus(x + dtb_ref[...]), 0.0)
        gcum = _dot(ltri_ref[...], jnp.concatenate(_split3(gstep), axis=0))
        y = jnp.where(lane < 16 * (GQ_G + 1), pltpu.roll(gcum, LANES + 16 * GQ_G - SM_A, 1),
                      pltpu.roll(beta, 16 * GQ_BETA - SM_B, 1))
        yp_ref[rows, :] = jnp.concatenate(_split3(y)[:GQ_PIECES], axis=1)

        xf = x + fb_ref[...]
        ls = (jnp.minimum(xf, 0.0) - jnp.log1p(jnp.exp(-jnp.abs(xf)))) * LOG2E
        cum = _dot(ltri_ref[...], jnp.concatenate(_split3(ls), axis=0)) + carry
        cp = jnp.concatenate(_split3(cum), axis=1)
        eq_ref[rows, :] = (_dot(cp, sq_ref[...]) + cq_ref[...]).astype(BF16)
        ek_ref[rows, :] = (_dot(cp, sk_ref[...]) + ck_ref[...]).astype(BF16)
        return cum[CHUNK - 1:CHUNK, :]

    lax.fori_loop(0, nblk, body, jnp.zeros((1, LANES), F32))


def _gate_prep(ps, alog_row, dtb_row, fb_row, ltri3, sq, sk, cq, ck):
    b, lp, _ = ps.shape
    const = lambda bi: (0, 0)
    return pl.pallas_call(
        functools.partial(_gate_prep_kernel, nblk=lp // CHUNK),
        grid=(b,),
        in_specs=[
            pl.BlockSpec((None, lp, LANES), lambda bi: (bi, 0, 0)),
            pl.BlockSpec((1, LANES), const),
            pl.BlockSpec((1, LANES), const),
            pl.BlockSpec((1, LANES), const),
            pl.BlockSpec((CHUNK, 3 * CHUNK), const),
            pl.BlockSpec((3 * LANES, LANES), const),
            pl.BlockSpec((3 * LANES, LANES), const),
            pl.BlockSpec((1, LANES), const),
            pl.BlockSpec((1, LANES), const),
        ],
        out_specs=[
            pl.BlockSpec((None, lp, GQ_PIECES * LANES), lambda bi: (bi, 0, 0)),
            pl.BlockSpec((None, lp, LANES), lambda bi: (bi, 0, 0)),
            pl.BlockSpec((None, lp, LANES), lambda bi: (bi, 0, 0)),
        ],
        out_shape=[
            jax.ShapeDtypeStruct((b, lp, GQ_PIECES * LANES), BF16),
            jax.ShapeDtypeStruct((b, lp, LANES), BF16),
            jax.ShapeDtypeStruct((b, lp, LANES), BF16),
        ],
        compiler_params=_params("parallel"),
        name="gate_prep",
    )(ps, alog_row, dtb_row, fb_row, ltri3, sq, sk, cq, ck)


def _gdn_kernel(q_ref, k_ref, v_ref, z_ref, yp_ref, sel_ref, oneh_ref, wq_ref, wk_ref, wv_ref,
                gn_ref, shift_ref, o_ref, s_ref, prev_ref, *, hg):
    c = pl.program_id(2)

    @pl.when(c == 0)
    def _():
        s_ref[...] = jnp.zeros_like(s_ref)
        prev_ref[...] = jnp.zeros_like(prev_ref)

    row = lax.broadcasted_iota(jnp.int32, (CHUNK, CHUNK), 0)
    col = lax.broadcasted_iota(jnp.int32, (CHUNK, CHUNK), 1)
    tril = row >= col
    strict = row > col
    heads = range(hg)
    hsl = lambda a, u: a[:, u * HEAD_DIM:(u + 1) * HEAD_DIM]

    yp = yp_ref[...]
    gb = _dot(yp, sel_ref[...])
    g_rows = _dot_nt(oneh_ref[...], yp)
    gq = lambda u, q: gb[:, (u * N_GQ + q) * LANES:(u * N_GQ + q + 1) * LANES]
    g_col = [gq(u, GQ_G) for u in heads]
    beta = [gq(u, GQ_BETA) for u in heads]
    eg = [jnp.exp(g_col[u]) for u in heads]

    def conv_act(x_ref, w_ref, t):
        x16 = x_ref[...]
        x2 = jnp.concatenate([prev_ref[t], x16], axis=0)
        prev_ref[t] = x16
        sh = _dot(shift_ref[...], x2)
        w = w_ref[...]
        acc = x16.astype(F32) * w[CONV_K - 1:CONV_K, :]
        for s in range(1, CONV_K):
            acc = acc + sh[(s - 1) * CHUNK:s * CHUNK, :] * w[CONV_K - 1 - s:CONV_K - s, :]
        return _silu(acc)

    qa = conv_act(q_ref, wq_ref, 0)
    ka = conv_act(k_ref, wk_ref, 1)
    va = conv_act(v_ref, wv_ref, 2)

    def l2n(x):
        return x * lax.rsqrt(jnp.sum(x * x, axis=-1, keepdims=True) + NORM_EPS)

    qn = [l2n(hsl(qa, u)) * (HEAD_DIM ** -0.5) for u in heads]
    kn = [l2n(hsl(ka, u)) for u in heads]
    k16 = [kn[u].astype(BF16) for u in heads]

    state = [s_ref[u] for u in heads]
    kb = [kn[u] * beta[u] for u in heads]
    r1 = [_dot(jnp.concatenate([kb[u] * eg[u], qn[u] * eg[u]], axis=0).astype(BF16),
               state[u].astype(BF16)) for u in heads]

    kq = [_dot_nt(jnp.concatenate([kb[u], qn[u]], axis=0).astype(BF16), k16[u])
          for u in heads]
    ediff = [jnp.exp(jnp.minimum(g_col[u] - g_rows[u * ROW_REP:u * ROW_REP + 1, :], 0.0))
             for u in heads]
    a = [kq[u][:CHUNK] * jnp.where(strict, ediff[u], 0.0) for u in heads]
    qk_m = [(kq[u][CHUNK:] * jnp.where(tril, ediff[u], 0.0)).astype(BF16) for u in heads]

    a16 = [a[u].astype(BF16) for u in heads]
    yk = [_dot(a16[u], a16[u]) for u in heads]
    sk = [-a[u] for u in heads]
    nlev = int(np.log2(CHUNK)) - 1
    for lev in range(nlev):
        yk16 = [yk[u].astype(BF16) for u in heads]
        if lev < nlev - 1:
            zz = [_dot(jnp.concatenate([sk[u], yk[u]], axis=0).astype(BF16), yk16[u]) for u in heads]
            sk = [sk[u] + yk[u] + zz[u][:CHUNK] for u in heads]
            yk = [zz[u][CHUNK:] for u in heads]
        else:
            sk = [sk[u] + yk[u] + _dot(sk[u].astype(BF16), yk16[u]) for u in heads]

    rhs = [hsl(va, u) * beta[u] - r1[u][:CHUNK] for u in heads]
    v_new = [(rhs[u] + _dot(sk[u].astype(BF16), rhs[u].astype(BF16))).astype(BF16) for u in heads]
    o = [r1[u][CHUNK:] + _dot(qk_m[u], v_new[u]) for u in heads]
    for u in heads:
        k_dec = kn[u] * jnp.exp(g_col[u][CHUNK - 1:CHUNK, :] - g_col[u])
        s_ref[u] = state[u] * eg[u][CHUNK - 1:CHUNK, :] + _dot_tn(k_dec.astype(BF16), v_new[u])

    on = jnp.concatenate([_norm_rows(o[u], gn_ref[...]) for u in heads], axis=1)
    o_ref[...] = (on * _silu(z_ref[...].astype(F32))).astype(o_ref.dtype)


def _gdn(p1, yp, sel, oneh, conv_w, gnorm, shift3, *, seq, hg):
    b, lp, _ = p1.shape
    nc = lp // CHUNK
    gw = hg * HEAD_DIM
    ng = HEADS // hg
    col = lambda base: (lambda bi, gi, ci: (bi, ci, base * LANES // gw + gi))
    wcol = lambda base: (lambda bi, gi, ci: (0, base * LANES // gw + gi))
    const = lambda bi, gi, ci: (0, 0)
    return pl.pallas_call(
        functools.partial(_gdn_kernel, hg=hg),
        grid=(b, ng, nc),
        in_specs=[
            pl.BlockSpec((None, CHUNK, gw), col(COL_GDN_Q)),
            pl.BlockSpec((None, CHUNK, gw), col(COL_GDN_K)),
            pl.BlockSpec((None, CHUNK, gw), col(COL_GDN_V)),
            pl.BlockSpec((None, CHUNK, gw), col(COL_Z)),
            pl.BlockSpec((None, CHUNK, GQ_PIECES * LANES), lambda bi, gi, ci: (bi, ci, 0)),
            pl.BlockSpec((GQ_PIECES * LANES, hg * N_GQ * LANES), lambda bi, gi, ci: (0, gi)),
            pl.BlockSpec((hg * ROW_REP, GQ_PIECES * LANES), lambda bi, gi, ci: (gi, 0)),
            pl.BlockSpec((CONV_K, gw), wcol(COL_GDN_Q)),
            pl.BlockSpec((CONV_K, gw), wcol(COL_GDN_K)),
            pl.BlockSpec((CONV_K, gw), wcol(COL_GDN_V)),
            pl.BlockSpec((1, HEAD_DIM), const),
            pl.BlockSpec((3 * CHUNK, 2 * CHUNK), const),
        ],
        out_specs=pl.BlockSpec((None, CHUNK, gw),
                               lambda bi, gi, ci: (bi, jnp.maximum(ci - 1, 0), gi)),
        out_shape=jax.ShapeDtypeStruct((b, seq, HEADS * HEAD_DIM), BF16),
        scratch_shapes=[
            pltpu.VMEM((hg, HEAD_DIM, HEAD_DIM), F32),
            pltpu.VMEM((3, CHUNK, gw), BF16),
        ],
        compiler_params=_params("parallel", "parallel", "arbitrary"),
        name="gdn",
    )(p1, p1, p1, p1, yp, sel, oneh, conv_w, conv_w, conv_w, gnorm, shift3)


def _fox_kernel(q_ref, k_ref, v_ref, eq_ref, ek_ref, gq_ref, gk_ref, o_ref,
                kaug_ref, vaug_ref, *, tq, lp):
    h = pl.program_id(1)
    i = pl.program_id(2)

    @pl.when(i == 0)
    def _():
        def build(rows, n):
            kn = _norm_rows(k_ref[rows, :].astype(F32), gk_ref[...])
            kaug_ref[rows, 0:HEAD_DIM] = kn.astype(BF16)
            kaug_ref[rows, HEAD_DIM:2 * HEAD_DIM] = ek_ref[rows, :]
            vaug_ref[rows, 0:HEAD_DIM] = v_ref[rows, :]
            vaug_ref[rows, HEAD_DIM:2 * HEAD_DIM] = jnp.ones((n, HEAD_DIM), BF16)

        def body(r, c):
            build(pl.ds(pl.multiple_of(r * tq, 128), tq), tq)
            return c
        lax.fori_loop(0, lp // tq, body, 0)
        if lp % tq:
            build(pl.ds(lp - lp % tq, lp % tq), lp % tq)

    q0 = pl.multiple_of(CHUNK + i * tq, 128)
    rows_q = pl.ds(q0, tq)
    qn = _norm_rows(q_ref[rows_q, :].astype(F32), gq_ref[...]) * (HEAD_DIM ** -0.5 * LOG2E)
    lane = lax.broadcasted_iota(jnp.int32, (tq, LANES), 1)
    mine = lax.shift_right_logical(lane, 3) == h
    eqm = jnp.where(mine, eq_ref[rows_q, :], jnp.zeros((), BF16))
    q16 = qn.astype(BF16)

    gmax = (jnp.max(jnp.abs(gq_ref[...]), axis=-1, keepdims=True)
            * jnp.max(jnp.abs(gk_ref[...]), axis=-1, keepdims=True))
    bound = (gmax * (HEAD_DIM ** 0.5 * LOG2E * 1.02)).astype(BF16)
    safe = bound.astype(F32)[0, 0] <= FOX_SAFE_BOUND

    rr = lax.broadcasted_iota(jnp.int32, (tq, tq), 0)
    cc = lax.broadcasted_iota(jnp.int32, (tq, tq), 1)
    causal = rr >= cc

    def blocks(j):
        return pl.ds(pl.multiple_of(CHUNK + j * tq, 128), tq)

    @pl.when(safe)
    def _():
        eqb = jnp.where(lane == FOX_EXTRA * h + FOX_M_LANE, -bound, eqm)
        qaug = jnp.concatenate([q16, eqb], axis=1)
        half = tq // 2

        def probs(qa, krows, mask=None):
            s = _dot_nt(qa, kaug_ref[krows, :])
            if mask is not None:
                s = jnp.where(mask, s, NEG_BIG)
            return jnp.exp2(s).astype(BF16)

        def finish(acc, rows):
            o_ref[rows, :] = (acc[:, :HEAD_DIM] / acc[:, HEAD_DIM:]).astype(o_ref.dtype)

        for ii in range(pl.cdiv(lp - CHUNK, tq)):
            @pl.when(i == ii)
            def _(ii=ii):
                acc = _dot(probs(qaug, slice(0, CHUNK), lane >= FRONT_PAD), vaug_ref[0:CHUNK, :])
                for j in range(ii):
                    kr = slice(CHUNK + j * tq, CHUNK + (j + 1) * tq)
                    acc = acc + _dot(probs(qaug, kr), vaug_ref[kr, :])
                k_lo = slice(CHUNK + ii * tq, CHUNK + ii * tq + half)
                k_all = slice(CHUNK + ii * tq, CHUNK + (ii + 1) * tq)
                p_lo = probs(qaug[:half], k_lo, causal[:half, :half])
                p_hi = probs(qaug[half:], k_all, causal[half:, :])
                finish(acc[:half] + _dot(p_lo, vaug_ref[k_lo, :]), slice(0, half))
                finish(acc[half:] + _dot(p_hi, vaug_ref[k_all, :]), slice(half, tq))

    @pl.when(jnp.logical_not(safe))
    def _():
        qaug = jnp.concatenate([q16, eqm], axis=1)
        s = _dot_nt(qaug, kaug_ref[0:CHUNK, :])
        s = jnp.where(lane >= FRONT_PAD, s, NEG_BIG)
        m = jnp.max(s, axis=-1, keepdims=True)
        acc = _dot(jnp.exp2(s - m).astype(BF16), vaug_ref[0:CHUNK, :])

        def update(s, vrows, m, acc):
            m_new = jnp.maximum(m, jnp.max(s, axis=-1, keepdims=True))
            p = jnp.exp2(s - m_new)
            return m_new, jnp.exp2(m - m_new) * acc + _dot(p.astype(BF16), vrows)

        def body(j, carry):
            s = _dot_nt(qaug, kaug_ref[blocks(j), :])
            return update(s, vaug_ref[blocks(j), :], *carry)
        m, acc = lax.fori_loop(0, i, body, (m, acc))

        s = _dot_nt(qaug, kaug_ref[rows_q, :])
        m, acc = update(jnp.where(causal, s, NEG_BIG), vaug_ref[rows_q, :], m, acc)
        o_ref[...] = (acc[:, :HEAD_DIM] / acc[:, HEAD_DIM:]).astype(o_ref.dtype)


def _fox(p1, eq, ek, gq, gk, *, seq, tq):
    b, lp, _ = p1.shape
    colmap = lambda base: (lambda bi, hi, qi: (bi, 0, base + hi))
    const = lambda bi, hi, qi: (0, 0)
    return pl.pallas_call(
        functools.partial(_fox_kernel, tq=tq, lp=lp),
        grid=(b, HEADS, seq // tq),
        in_specs=[
            pl.BlockSpec((None, lp, HEAD_DIM), colmap(COL_FOX_Q)),
            pl.BlockSpec((None, lp, HEAD_DIM), colmap(COL_FOX_K)),
            pl.BlockSpec((None, lp, HEAD_DIM), colmap(COL_FOX_V)),
            pl.BlockSpec((None, lp, LANES), lambda bi, hi, qi: (bi, 0, 0)),
            pl.BlockSpec((None, lp, LANES), lambda bi, hi, qi: (bi, 0, 0)),
            pl.BlockSpec((1, HEAD_DIM), const),
            pl.BlockSpec((1, HEAD_DIM), const),
        ],
        out_specs=pl.BlockSpec((None, tq, HEAD_DIM), lambda bi, hi, qi: (bi, qi, hi)),
        out_shape=jax.ShapeDtypeStruct((b, seq, HEADS * HEAD_DIM), BF16),
        scratch_shapes=[
            pltpu.VMEM((lp, 2 * HEAD_DIM), BF16),
            pltpu.VMEM((lp, 2 * HEAD_DIM), BF16),
        ],
        compiler_params=_params("parallel", "parallel", "arbitrary"),
        name="fox",
    )(p1, p1, p1, eq, ek, gq, gk)


def _merge_kernel(x_ref, g_ref, ya_ref, yb_ref, wga_ref, wgb_ref, wa_ref, wb_ref, wo_ref,
                  o_ref, u_ref):
    j = pl.program_id(1)

    @pl.when(j == 0)
    def _():
        x = x_ref[...]
        u_ref[...] = _norm_rows(x, g_ref[...]).astype(BF16)
        o_ref[...] = x

    u = u_ref[...]
    ga = _dot_nt(u, wga_ref[...])
    gb = _dot_nt(u, wgb_ref[...])
    pa = _dot(ya_ref[...], wa_ref[...])
    pb = _dot(yb_ref[...], wb_ref[...])
    mix = _sigmoid(ga) * pa + _sigmoid(gb) * pb
    o_ref[...] += _dot(mix.astype(BF16), wo_ref[...])


def _merge(x2d, g, ya, yb, wg_t, wa, wb, wo, *, tm, tj):
    m, d = x2d.shape
    rowblk = lambda i, j: (i, 0)
    colblk = lambda i, j: (0, j)
    return pl.pallas_call(
        _merge_kernel,
        grid=(m // tm, d // tj),
        in_specs=[
            pl.BlockSpec((tm, d), rowblk),
            pl.BlockSpec((1, d), lambda i, j: (0, 0)),
            pl.BlockSpec((tm, d), rowblk),
            pl.BlockSpec((tm, d), rowblk),
            pl.BlockSpec((tj, d), lambda i, j: (j, 0)),
            pl.BlockSpec((tj, d), lambda i, j: (j + d // tj, 0)),
            pl.BlockSpec((d, tj), colblk),
            pl.BlockSpec((d, tj), colblk),
            pl.BlockSpec((tj, d), lambda i, j: (j, 0)),
        ],
        out_specs=pl.BlockSpec((tm, d), rowblk),
        out_shape=jax.ShapeDtypeStruct((m, d), F32),
        scratch_shapes=[pltpu.VMEM((tm, d), BF16)],
        compiler_params=_params("parallel", "arbitrary"),
        name="merge",
    )(x2d, g, ya, yb, wg_t, wg_t, wa, wb, wo)


def _mlp_kernel(h_ref, g_ref, wu_ref, wd_ref, gf_ref, o_ref, u_ref):
    f = pl.program_id(1)

    @pl.when(f == 0)
    def _():
        x = h_ref[...]
        u_ref[...] = _norm_rows(x, g_ref[...]).astype(BF16)
        o_ref[...] = x

    a = jnp.maximum(_dot(u_ref[...], wu_ref[...]), 0.0)
    o_ref[...] += _dot((a * a).astype(BF16), wd_ref[...])

    @pl.when(f == pl.num_programs(1) - 1)
    def _():
        o_ref[...] = _norm_rows(o_ref[...], gf_ref[...])


def _mlp(h2d, g, wu, wd, gf, *, tm, tf):
    m, d = h2d.shape
    dff = wu.shape[1]
    rowblk = lambda i, f: (i, 0)
    return pl.pallas_call(
        _mlp_kernel,
        grid=(m // tm, dff // tf),
        in_specs=[
            pl.BlockSpec((tm, d), rowblk),
            pl.BlockSpec((1, d), lambda i, f: (0, 0)),
            pl.BlockSpec((d, tf), lambda i, f: (0, f)),
            pl.BlockSpec((tf, d), lambda i, f: (f, 0)),
            pl.BlockSpec((1, d), lambda i, f: (0, 0)),
        ],
        out_specs=pl.BlockSpec((tm, d), rowblk),
        out_shape=jax.ShapeDtypeStruct((m, d), F32),
        scratch_shapes=[pltpu.VMEM((tm, d), BF16)],
        compiler_params=_params("parallel", "arbitrary"),
        name="mlp",
    )(h2d, g, wu, wd, gf)


def _constants():
    r = np.arange(128)
    ltri = (r[:, None] >= r[None, :]).astype(np.float32)
    ltri3 = np.concatenate([ltri, ltri, ltri], axis=1)

    sq = np.zeros((3 * LANES, LANES), np.float32)
    sk = np.zeros((3 * LANES, LANES), np.float32)
    cq = np.zeros((1, LANES), np.float32)
    ck = np.zeros((1, LANES), np.float32)
    for h in range(HEADS):
        for p in range(3):
            sq[p * LANES + SM_F + h, FOX_EXTRA * h + p] = 1.0
            sk[p * LANES + SM_F + h, FOX_EXTRA * h + 3 + p] = -1.0
            cq[0, FOX_EXTRA * h + 3 + p] = 1.0
            ck[0, FOX_EXTRA * h + p] = 1.0
        ck[0, FOX_EXTRA * h + FOX_M_LANE] = 1.0

    shift3 = np.zeros((3 * CHUNK, 2 * CHUNK), np.float32)
    for s in range(1, CONV_K):
        shift3[(s - 1) * CHUNK + r, CHUNK + r - s] = 1.0

    src_lane = np.zeros((HEADS * N_GQ * LANES,), np.int64)
    for h in range(HEADS):
        for q in range(N_GQ):
            src_lane[(h * N_GQ + q) * LANES:(h * N_GQ + q + 1) * LANES] = 16 * q + h
    piece_lane = np.arange(GQ_PIECES * LANES) % LANES
    sel = (piece_lane[:, None] == src_lane[None, :]).astype(np.float32)
    oneh = (np.repeat(16 * GQ_G + np.arange(HEADS), ROW_REP)[:, None]
            == piece_lane[None, :]).astype(np.float32)

    as_bf = lambda a: jnp.asarray(a, BF16)
    return dict(ltri3=as_bf(ltri3), sq=as_bf(sq), sk=as_bf(sk), cq=jnp.asarray(cq),
                ck=jnp.asarray(ck), shift3=as_bf(shift3), sel=as_bf(sel), oneh=as_bf(oneh))


def _row128(vec, offset):
    return jnp.zeros((1, LANES), F32).at[0, offset:offset + HEADS].set(vec.astype(F32))


def _pick_rows_tile(total, target):
    best = 128
    for t in range(128, target + 1, 128):
        if total % t == 0:
            best = t
    return best


def _layer(x, meta_tokens, mix_norm_g, w_in, conv_w, a_log, dt_bias, gdn_norm_g, w_o_gdn,
           fox_q_norm_g, fox_k_norm_g, fox_f_bias, w_o_fox, w_out, mlp_norm_g, w_up, w_down,
           final_norm_g):
    b, seq, d = x.shape
    lp = CHUNK + seq
    qk = HEADS * HEAD_DIM

    o_z = 3 * qk
    o_b = o_z + qk
    o_a = o_b + HEADS
    o_fq = o_a + HEADS
    o_f = o_fq + 3 * qk
    o_ga = o_f + HEADS
    o_gb = o_ga + d

    wt = jnp.swapaxes(w_in, 1, 2)
    row = lambda v: v.reshape(1, -1).astype(F32)
    cst = _constants()

    hn = _prenorm(x, meta_tokens.astype(x.dtype), row(mix_norm_g))
    p1, ps = _inproj(hn.reshape(b * lp, d), wt, o_b, o_fq, o_f - o_fq,
                     ((o_b, o_fq - o_b), (o_f, o_ga - o_f)),
                     tm=_pick_rows_tile(lp, 1408), tn=1024)
    p1 = p1.reshape(b, lp, P1_COLS)
    ps = ps.reshape(b, lp, LANES)

    yp, eq, ek = _gate_prep(ps, _row128(a_log, SM_A), _row128(dt_bias, SM_A),
                            _row128(fox_f_bias, SM_F), cst["ltri3"], cst["sq"], cst["sk"],
                            cst["cq"], cst["ck"])
    ya = _gdn(p1, yp, cst["sel"], cst["oneh"], conv_w.astype(F32), row(gdn_norm_g),
              cst["shift3"], seq=seq, hg=16)
    yb = _fox(p1, eq, ek, row(fox_q_norm_g), row(fox_k_norm_g), seq=seq, tq=1024)

    x2d = x.reshape(b * seq, d)
    h1 = _merge(x2d, row(mix_norm_g), ya.reshape(b * seq, qk), yb.reshape(b * seq, qk),
                _cast_rows(wt, o_ga, 2 * d, tr=512), w_o_gdn.astype(BF16),
                w_o_fox.astype(BF16), w_out.astype(BF16),
                tm=512, tj=512)
    out = _mlp(h1, row(mlp_norm_g), w_up.astype(BF16), w_down.astype(BF16),
               row(final_norm_g), tm=512, tf=1024)
    return out.reshape(b, seq, d)


def kernel(x, meta_tokens, mix_norm_g, w_in, conv_w, a_log, dt_bias, gdn_norm_g, w_o_gdn,
           fox_q_norm_g, fox_k_norm_g, fox_f_bias, w_o_fox, w_out, mlp_norm_g, w_up, w_down,
           final_norm_g):
    assert w_in.shape[0] == 1, "single-layer block"
    return _layer(x, meta_tokens, mix_norm_g[0], w_in, conv_w[0], a_log[0], dt_bias[0],
                  gdn_norm_g[0], w_o_gdn[0], fox_q_norm_g[0], fox_k_norm_g[0], fox_f_bias[0],
                  w_o_fox[0], w_out[0], mlp_norm_g[0], w_up[0], w_down[0], final_norm_g)
```

```python
import functools
import math

import numpy as np
import jax
import jax.numpy as jnp
from jax import lax
from jax.experimental import pallas as pl
from jax.experimental.pallas import tpu as pltpu

F32 = jnp.float32
BF16 = jnp.bfloat16

NORM_EPS = 1e-6
N_META = 16
HEADS = 16
HEAD_DIM = 128
CONV_K = 4
LANES = 128
CHUNK = 128
FRONT_PAD = CHUNK - N_META
NEG_BIG = -1e30
LOG2E = math.log2(math.e)
VMEM_LIMIT = 56 * 1024 * 1024

COL_GDN_Q, COL_GDN_K, COL_GDN_V, COL_Z = 0, 16, 32, 48
COL_FOX_Q, COL_FOX_K, COL_FOX_V = 64, 80, 96
P1_COLS = 112 * LANES
SM_B, SM_A, SM_F = 0, 16, 32
GQ_G, GQ_BETA = range(2)
N_GQ = 2
GQ_PIECES = 2
ROW_REP = 8
FOX_EXTRA = 8
FOX_M_LANE = 6
FOX_SAFE_BOUND = 40.0


def _params(*sem):
    return pltpu.CompilerParams(dimension_semantics=sem, vmem_limit_bytes=VMEM_LIMIT)


def _norm_rows(x, g):
    ms = jnp.mean(x * x, axis=-1, keepdims=True)
    return x * lax.rsqrt(ms + NORM_EPS) * g


def _split3(x):
    hi = x.astype(BF16)
    r = x - hi.astype(F32)
    mid = r.astype(BF16)
    lo = (r - mid.astype(F32)).astype(BF16)
    return hi, mid, lo


def _dot(a, b):
    return jnp.dot(a, b, preferred_element_type=F32)


def _dot_nt(a, b):
    return lax.dot_general(a, b, (((1,), (1,)), ((), ())), preferred_element_type=F32)


def _dot_tn(a, b):
    return lax.dot_general(a, b, (((0,), (0,)), ((), ())), preferred_element_type=F32)


def _sigmoid(x):
    return 1.0 / (1.0 + jnp.exp(-x))


def _silu(x):
    return x * _sigmoid(x)


def _softplus(x):
    return jnp.maximum(x, 0.0) + jnp.log1p(jnp.exp(-jnp.abs(x)))


def _prenorm_kernel(x_ref, meta_ref, g_ref, o_ref):
    r = pl.program_id(1)

    @pl.when(r == 0)
    def _():
        o_ref[0:FRONT_PAD, :] = jnp.zeros((FRONT_PAD, o_ref.shape[1]), o_ref.dtype)
        o_ref[FRONT_PAD:CHUNK, :] = _norm_rows(meta_ref[...], g_ref[...]).astype(o_ref.dtype)

    @pl.when(r > 0)
    def _():
        o_ref[...] = _norm_rows(x_ref[...], g_ref[...]).astype(o_ref.dtype)


def _prenorm(x, meta_tokens, g):
    b, seq, d = x.shape
    lp = CHUNK + seq
    return pl.pallas_call(
        _prenorm_kernel,
        grid=(b, lp // CHUNK),
        in_specs=[
            pl.BlockSpec((None, CHUNK, d), lambda bi, r: (bi, jnp.maximum(r - 1, 0), 0)),
            pl.BlockSpec((N_META, d), lambda bi, r: (0, 0)),
            pl.BlockSpec((1, d), lambda bi, r: (0, 0)),
        ],
        out_specs=pl.BlockSpec((None, CHUNK, d), lambda bi, r: (bi, r, 0)),
        out_shape=jax.ShapeDtypeStruct((b, lp, d), BF16),
        compiler_params=_params("parallel", "arbitrary"),
        name="prenorm",
    )(x, meta_tokens, g)


def _inproj_kernel(x_ref, w_ref, wsa_ref, wsb_ref, o_ref, os_ref):
    @pl.when(pl.program_id(1) == 0)
    def _():
        pad = LANES - wsa_ref.shape[0] - wsb_ref.shape[0]
        ws = jnp.concatenate([wsa_ref[...], wsb_ref[...], jnp.zeros((pad, wsa_ref.shape[1]), F32)],
                             axis=0)
        os_ref[...] = _dot_nt(x_ref[...], ws.astype(BF16))

    o_ref[...] = _dot_nt(x_ref[...], w_ref[...].astype(BF16)).astype(o_ref.dtype)


F32_SUBLANES = 8


def _element_rows(tile, d, start_tiles):
    return pl.BlockSpec((None, pl.Element(tile), pl.Element(d)),
                        lambda *ids: (0, F32_SUBLANES * start_tiles(*ids), 0))


def _inproj(hn, wt, n_a, b_start, n_b, small, *, tm, tn):
    m, d = hn.shape
    na, nb = n_a // tn, n_b // tn
    (s0, c0), (s1, c1) = small
    sub = F32_SUBLANES
    assert all(v % sub == 0 for v in (tn, b_start - n_a, s0, s1))

    def w_rows(i, j):
        return j * (tn // sub) + jnp.where(j >= na, (b_start - n_a) // sub, 0)

    return pl.pallas_call(
        _inproj_kernel,
        grid=(m // tm, na + nb),
        in_specs=[
            pl.BlockSpec((tm, d), lambda i, j: (i, 0)),
            _element_rows(tn, d, w_rows),
            _element_rows(c0, d, lambda i, j: s0 // sub),
            _element_rows(c1, d, lambda i, j: s1 // sub),
        ],
        out_specs=[
            pl.BlockSpec((tm, tn), lambda i, j: (i, j)),
            pl.BlockSpec((tm, LANES), lambda i, j: (i, 0)),
        ],
        out_shape=[
            jax.ShapeDtypeStruct((m, (na + nb) * tn), BF16),
            jax.ShapeDtypeStruct((m, LANES), F32),
        ],
        compiler_params=_params("parallel", "arbitrary"),
        name="inproj",
    )(hn, wt, wt, wt)


def _gate_prep_kernel(ps_ref, alog_ref, dtb_ref, fb_ref, ltri_ref, sq_ref, sk_ref, cq_ref, ck_ref,
                      yp_ref, eq_ref, ek_ref, *, nblk):
    lane = lax.broadcasted_iota(jnp.int32, (CHUNK, LANES), 1)
    rowi = lax.broadcasted_iota(jnp.int32, (CHUNK, LANES), 0)

    def body(i, carry):
        rows = pl.ds(pl.multiple_of(i * CHUNK, CHUNK), CHUNK)
        x = ps_ref[rows, :]
        valid = (i * CHUNK + rowi) >= FRONT_PAD

        beta = jnp.where(valid, _sigmoid(x), 0.0)
        gstep = jnp.where(valid, -jnp.exp(alog_ref[...]) * _softplus(x + dtb_ref[...]), 0.0)
        gcum = _dot(ltri_ref[...], jnp.concatenate(_split3(gstep), axis=0))
        y = jnp.where(lane < 16 * (GQ_G + 1), pltpu.roll(gcum, LANES + 16 * GQ_G - SM_A, 1),
                      pltpu.roll(beta, 16 * GQ_BETA - SM_B, 1))
        yp_ref[rows, :] = jnp.concatenate(_split3(y)[:GQ_PIECES], axis=1)

        xf = x + fb_ref[...]
        ls = (jnp.minimum(xf, 0.0) - jnp.log1p(jnp.exp(-jnp.abs(xf)))) * LOG2E
        cum = _dot(ltri_ref[...], jnp.concatenate(_split3(ls), axis=0)) + carry
        cp = jnp.concatenate(_split3(cum), axis=1)
        eq_ref[rows, :] = (_dot(cp, sq_ref[...]) + cq_ref[...]).astype(BF16)
        ek_ref[rows, :] = (_dot(cp, sk_ref[...]) + ck_ref[...]).astype(BF16)
        return cum[CHUNK - 1:CHUNK, :]

    lax.fori_loop(0, nblk, body, jnp.zeros((1, LANES), F32))


def _gate_prep(ps, alog_row, dtb_row, fb_row, ltri3, sq, sk, cq, ck):
    b, lp, _ = ps.shape
    const = lambda bi: (0, 0)
    return pl.pallas_call(
        functools.partial(_gate_prep_kernel, nblk=lp // CHUNK),
        grid=(b,),
        in_specs=[
            pl.BlockSpec((None, lp, LANES), lambda bi: (bi, 0, 0)),
            pl.BlockSpec((1, LANES), const),
            pl.BlockSpec((1, LANES), const),
            pl.BlockSpec((1, LANES), const),
            pl.BlockSpec((CHUNK, 3 * CHUNK), const),
            pl.BlockSpec((3 * LANES, LANES), const),
            pl.BlockSpec((3 * LANES, LANES), const),
            pl.BlockSpec((1, LANES), const),
            pl.BlockSpec((1, LANES), const),
        ],
        out_specs=[
            pl.BlockSpec((None, lp, GQ_PIECES * LANES), lambda bi: (bi, 0, 0)),
            pl.BlockSpec((None, lp, LANES), lambda bi: (bi, 0, 0)),
            pl.BlockSpec((None, lp, LANES), lambda bi: (bi, 0, 0)),
        ],
        out_shape=[
            jax.ShapeDtypeStruct((b, lp, GQ_PIECES * LANES), BF16),
            jax.ShapeDtypeStruct((b, lp, LANES), BF16),
            jax.ShapeDtypeStruct((b, lp, LANES), BF16),
        ],
        compiler_params=_params("parallel"),
        name="gate_prep",
    )(ps, alog_row, dtb_row, fb_row, ltri3, sq, sk, cq, ck)


def _gdn_kernel(q_ref, k_ref, v_ref, z_ref, yp_ref, sel_ref, oneh_ref, wq_ref, wk_ref, wv_ref,
                gn_ref, shift_ref, o_ref, s_ref, prev_ref, *, hg):
    c = pl.program_id(2)

    @pl.when(c == 0)
    def _():
        s_ref[...] = jnp.zeros_like(s_ref)
        prev_ref[...] = jnp.zeros_like(prev_ref)

    row = lax.broadcasted_iota(jnp.int32, (CHUNK, CHUNK), 0)
    col = lax.broadcasted_iota(jnp.int32, (CHUNK, CHUNK), 1)
    tril = row >= col
    strict = row > col
    heads = range(hg)
    hsl = lambda a, u: a[:, u * HEAD_DIM:(u + 1) * HEAD_DIM]

    yp = yp_ref[...]
    gb = _dot(yp, sel_ref[...])
    g_rows = _dot_nt(oneh_ref[...], yp)
    gq = lambda u, q: gb[:, (u * N_GQ + q) * LANES:(u * N_GQ + q + 1) * LANES]
    g_col = [gq(u, GQ_G) for u in heads]
    beta = [gq(u, GQ_BETA) for u in heads]
    eg = [jnp.exp(g_col[u]) for u in heads]

    def conv_act(x_ref, w_ref, t):
        x16 = x_ref[...]
        x2 = jnp.concatenate([prev_ref[t], x16], axis=0)
        prev_ref[t] = x16
        sh = _dot(shift_ref[...], x2)
        w = w_ref[...]
        acc = x16.astype(F32) * w[CONV_K - 1:CONV_K, :]
        for s in range(1, CONV_K):
            acc = acc + sh[(s - 1) * CHUNK:s * CHUNK, :] * w[CONV_K - 1 - s:CONV_K - s, :]
        return _silu(acc)

    qa = conv_act(q_ref, wq_ref, 0)
    ka = conv_act(k_ref, wk_ref, 1)
    va = conv_act(v_ref, wv_ref, 2)

    def l2n(x):
        return x * lax.rsqrt(jnp.sum(x * x, axis=-1, keepdims=True) + NORM_EPS)

    qn = [l2n(hsl(qa, u)) * (HEAD_DIM ** -0.5) for u in heads]
    kn = [l2n(hsl(ka, u)) for u in heads]
    k16 = [kn[u].astype(BF16) for u in heads]

    state = [s_ref[u] for u in heads]
    kb = [kn[u] * beta[u] for u in heads]
    r1 = [_dot(jnp.concatenate([kb[u] * eg[u], qn[u] * eg[u]], axis=0).astype(BF16),
               state[u].astype(BF16)) for u in heads]

    kq = [_dot_nt(jnp.concatenate([kb[u], qn[u]], axis=0).astype(BF16), k16[u])
          for u in heads]
    ediff = [jnp.exp(jnp.minimum(g_col[u] - g_rows[u * ROW_REP:u * ROW_REP + 1, :], 0.0))
             for u in heads]
    a = [kq[u][:CHUNK] * jnp.where(strict, ediff[u], 0.0) for u in heads]
    qk_m = [(kq[u][CHUNK:] * jnp.where(tril, ediff[u], 0.0)).astype(BF16) for u in heads]

    a16 = [a[u].astype(BF16) for u in heads]
    yk = [_dot(a16[u], a16[u]) for u in heads]
    sk = [-a[u] for u in heads]
    nlev = int(np.log2(CHUNK)) - 1
    for lev in range(nlev):
        yk16 = [yk[u].astype(BF16) for u in heads]
        if lev < nlev - 1:
            zz = [_dot(jnp.concatenate([sk[u], yk[u]], axis=0).astype(BF16), yk16[u]) for u in heads]
            sk = [sk[u] + yk[u] + zz[u][:CHUNK] for u in heads]
            yk = [zz[u][CHUNK:] for u in heads]
        else:
            sk = [sk[u] + yk[u] + _dot(sk[u].astype(BF16), yk16[u]) for u in heads]

    rhs = [hsl(va, u) * beta[u] - r1[u][:CHUNK] for u in heads]
    v_new = [(rhs[u] + _dot(sk[u].astype(BF16), rhs[u].astype(BF16))).astype(BF16) for u in heads]
    o = [r1[u][CHUNK:] + _dot(qk_m[u], v_new[u]) for u in heads]
    for u in heads:
        k_dec = kn[u] * jnp.exp(g_col[u][CHUNK - 1:CHUNK, :] - g_col[u])
        s_ref[u] = state[u] * eg[u][CHUNK - 1:CHUNK, :] + _dot_tn(k_dec.astype(BF16), v_new[u])

    on = jnp.concatenate([_norm_rows(o[u], gn_ref[...]) for u in heads], axis=1)
    o_ref[...] = (on * _silu(z_ref[...].astype(F32))).astype(o_ref.dtype)


def _gdn(p1, yp, sel, oneh, conv_w, gnorm, shift3, *, seq, hg):
    b, lp, _ = p1.shape
    nc = lp // CHUNK
    gw = hg * HEAD_DIM
    ng = HEADS // hg
    col = lambda base: (lambda bi, gi, ci: (bi, ci, base * LANES // gw + gi))
    wcol = lambda base: (lambda bi, gi, ci: (0, base * LANES // gw + gi))
    const = lambda bi, gi, ci: (0, 0)
    return pl.pallas_call(
        functools.partial(_gdn_kernel, hg=hg),
        grid=(b, ng, nc),
        in_specs=[
            pl.BlockSpec((None, CHUNK, gw), col(COL_GDN_Q)),
            pl.BlockSpec((None, CHUNK, gw), col(COL_GDN_K)),
            pl.BlockSpec((None, CHUNK, gw), col(COL_GDN_V)),
            pl.BlockSpec((None, CHUNK, gw), col(COL_Z)),
            pl.BlockSpec((None, CHUNK, GQ_PIECES * LANES), lambda bi, gi, ci: (bi, ci, 0)),
            pl.BlockSpec((GQ_PIECES * LANES, hg * N_GQ * LANES), lambda bi, gi, ci: (0, gi)),
            pl.BlockSpec((hg * ROW_REP, GQ_PIECES * LANES), lambda bi, gi, ci: (gi, 0)),
            pl.BlockSpec((CONV_K, gw), wcol(COL_GDN_Q)),
            pl.BlockSpec((CONV_K, gw), wcol(COL_GDN_K)),
            pl.BlockSpec((CONV_K, gw), wcol(COL_GDN_V)),
            pl.BlockSpec((1, HEAD_DIM), const),
            pl.BlockSpec((3 * CHUNK, 2 * CHUNK), const),
        ],
        out_specs=pl.BlockSpec((None, CHUNK, gw),
                               lambda bi, gi, ci: (bi, jnp.maximum(ci - 1, 0), gi)),
        out_shape=jax.ShapeDtypeStruct((b, seq, HEADS * HEAD_DIM), BF16),
        scratch_shapes=[
            pltpu.VMEM((hg, HEAD_DIM, HEAD_DIM), F32),
            pltpu.VMEM((3, CHUNK, gw), BF16),
        ],
        compiler_params=_params("parallel", "parallel", "arbitrary"),
        name="gdn",
    )(p1, p1, p1, p1, yp, sel, oneh, conv_w, conv_w, conv_w, gnorm, shift3)


def _fox_kernel(*refs, tq, lp, n_cast):
    q_ref, k_ref, v_ref, eq_ref, ek_ref, gq_ref, gk_ref = refs[:7]
    o_ref = refs[7 + n_cast]
    kaug_ref, vaug_ref = refs[-2:]
    h = pl.program_id(1)
    i = pl.program_id(2)

    for w_ref, w16_ref in zip(refs[7:7 + n_cast], refs[8 + n_cast:8 + 2 * n_cast]):
        w16_ref[...] = w_ref[...].astype(w16_ref.dtype)

    @pl.when(i == 0)
    def _():
        def build(rows, n):
            kn = _norm_rows(k_ref[rows, :].astype(F32), gk_ref[...])
            kaug_ref[rows, 0:HEAD_DIM] = kn.astype(BF16)
            kaug_ref[rows, HEAD_DIM:2 * HEAD_DIM] = ek_ref[rows, :]
            vaug_ref[rows, 0:HEAD_DIM] = v_ref[rows, :]
            vaug_ref[rows, HEAD_DIM:2 * HEAD_DIM] = jnp.ones((n, HEAD_DIM), BF16)

        def body(r, c):
            build(pl.ds(pl.multiple_of(r * tq, 128), tq), tq)
            return c
        lax.fori_loop(0, lp // tq, body, 0)
        if lp % tq:
            build(pl.ds(lp - lp % tq, lp % tq), lp % tq)

    q0 = pl.multiple_of(CHUNK + i * tq, 128)
    rows_q = pl.ds(q0, tq)
    qn = _norm_rows(q_ref[rows_q, :].astype(F32), gq_ref[...]) * (HEAD_DIM ** -0.5 * LOG2E)
    lane = lax.broadcasted_iota(jnp.int32, (tq, LANES), 1)
    mine = lax.shift_right_logical(lane, 3) == h
    eqm = jnp.where(mine, eq_ref[rows_q, :], jnp.zeros((), BF16))
    q16 = qn.astype(BF16)

    gmax = (jnp.max(jnp.abs(gq_ref[...]), axis=-1, keepdims=True)
            * jnp.max(jnp.abs(gk_ref[...]), axis=-1, keepdims=True))
    bound = (gmax * (HEAD_DIM ** 0.5 * LOG2E * 1.02)).astype(BF16)
    safe = bound.astype(F32)[0, 0] <= FOX_SAFE_BOUND

    rr = lax.broadcasted_iota(jnp.int32, (tq, tq), 0)
    cc = lax.broadcasted_iota(jnp.int32, (tq, tq), 1)
    causal = rr >= cc

    def blocks(j):
        return pl.ds(pl.multiple_of(CHUNK + j * tq, 128), tq)

    @pl.when(safe)
    def _():
        eqb = jnp.where(lane == FOX_EXTRA * h + FOX_M_LANE, -bound, eqm)
        qaug = jnp.concatenate([q16, eqb], axis=1)
        half = tq // 2

        def probs(qa, krows, mask=None):
            s = _dot_nt(qa, kaug_ref[krows, :])
            if mask is not None:
                s = jnp.where(mask, s, NEG_BIG)
            return jnp.exp2(s).astype(BF16)

        def finish(acc, rows):
            o_ref[rows, :] = (acc[:, :HEAD_DIM] / acc[:, HEAD_DIM:]).astype(o_ref.dtype)

        for ii in range(pl.cdiv(lp - CHUNK, tq)):
            @pl.when(i == ii)
            def _(ii=ii):
                acc = _dot(probs(qaug, slice(0, CHUNK), lane >= FRONT_PAD), vaug_ref[0:CHUNK, :])
                for j in range(ii):
                    kr = slice(CHUNK + j * tq, CHUNK + (j + 1) * tq)
                    acc = acc + _dot(probs(qaug, kr), vaug_ref[kr, :])
                k_lo = slice(CHUNK + ii * tq, CHUNK + ii * tq + half)
                k_all = slice(CHUNK + ii * tq, CHUNK + (ii + 1) * tq)
                p_lo = probs(qaug[:half], k_lo, causal[:half, :half])
                p_hi = probs(qaug[half:], k_all, causal[half:, :])
                finish(acc[:half] + _dot(p_lo, vaug_ref[k_lo, :]), slice(0, half))
                finish(acc[half:] + _dot(p_hi, vaug_ref[k_all, :]), slice(half, tq))

    @pl.when(jnp.logical_not(safe))
    def _():
        qaug = jnp.concatenate([q16, eqm], axis=1)
        s = _dot_nt(qaug, kaug_ref[0:CHUNK, :])
        s = jnp.where(lane >= FRONT_PAD, s, NEG_BIG)
        m = jnp.max(s, axis=-1, keepdims=True)
        acc = _dot(jnp.exp2(s - m).astype(BF16), vaug_ref[0:CHUNK, :])

        def update(s, vrows, m, acc):
            m_new = jnp.maximum(m, jnp.max(s, axis=-1, keepdims=True))
            p = jnp.exp2(s - m_new)
            return m_new, jnp.exp2(m - m_new) * acc + _dot(p.astype(BF16), vrows)

        def body(j, carry):
            s = _dot_nt(qaug, kaug_ref[blocks(j), :])
            return update(s, vaug_ref[blocks(j), :], *carry)
        m, acc = lax.fori_loop(0, i, body, (m, acc))

        s = _dot_nt(qaug, kaug_ref[rows_q, :])
        m, acc = update(jnp.where(causal, s, NEG_BIG), vaug_ref[rows_q, :], m, acc)
        o_ref[...] = (acc[:, :HEAD_DIM] / acc[:, HEAD_DIM:]).astype(o_ref.dtype)


def _fox(p1, eq, ek, gq, gk, weights, wt, wt_rows, *, seq, tq):
    b, lp, _ = p1.shape
    nq = seq // tq
    nsteps = b * HEADS * nq
    colmap = lambda base: (lambda bi, hi, qi: (bi, 0, base + hi))
    const = lambda bi, hi, qi: (0, 0)
    step = lambda bi, hi, qi: (bi * HEADS + hi) * nq + qi

    cast_in, cast_out, cast_shapes = [], [], []
    for w in weights:
        _, r, c = w.shape
        rp = r // nsteps
        assert r % nsteps == 0 and rp % 16 == 0
        cast_in.append(pl.BlockSpec((None, rp, c), lambda bi, hi, qi: (0, step(bi, hi, qi), 0)))
        cast_out.append(pl.BlockSpec((rp, c), lambda bi, hi, qi: (step(bi, hi, qi), 0)))
        cast_shapes.append(jax.ShapeDtypeStruct((r, c), BF16))
    start, n = wt_rows
    rp = n // nsteps
    assert n % nsteps == 0 and rp % 16 == 0 and start % F32_SUBLANES == 0
    cast_in.append(_element_rows(rp, wt.shape[2], lambda bi, hi, qi: (
        start // F32_SUBLANES + step(bi, hi, qi) * (rp // F32_SUBLANES))))
    cast_out.append(pl.BlockSpec((rp, wt.shape[2]), lambda bi, hi, qi: (step(bi, hi, qi), 0)))
    cast_shapes.append(jax.ShapeDtypeStruct((n, wt.shape[2]), BF16))

    outs = pl.pallas_call(
        functools.partial(_fox_kernel, tq=tq, lp=lp, n_cast=len(cast_in)),
        grid=(b, HEADS, nq),
        in_specs=[
            pl.BlockSpec((None, lp, HEAD_DIM), colmap(COL_FOX_Q)),
            pl.BlockSpec((None, lp, HEAD_DIM), colmap(COL_FOX_K)),
            pl.BlockSpec((None, lp, HEAD_DIM), colmap(COL_FOX_V)),
            pl.BlockSpec((None, lp, LANES), lambda bi, hi, qi: (bi, 0, 0)),
            pl.BlockSpec((None, lp, LANES), lambda bi, hi, qi: (bi, 0, 0)),
            pl.BlockSpec((1, HEAD_DIM), const),
            pl.BlockSpec((1, HEAD_DIM), const),
        ] + cast_in,
        out_specs=[pl.BlockSpec((None, tq, HEAD_DIM), lambda bi, hi, qi: (bi, qi, hi))] + cast_out,
        out_shape=[jax.ShapeDtypeStruct((b, seq, HEADS * HEAD_DIM), BF16)] + cast_shapes,
        scratch_shapes=[
            pltpu.VMEM((lp, 2 * HEAD_DIM), BF16),
            pltpu.VMEM((lp, 2 * HEAD_DIM), BF16),
        ],
        compiler_params=_params("parallel", "parallel", "arbitrary"),
        name="fox",
    )(p1, p1, p1, eq, ek, gq, gk, *weights, wt)
    return outs[0], outs[1:]


def _merge_kernel(x_ref, g_ref, ya_ref, yb_ref, wga_ref, wgb_ref, wa_ref, wb_ref, wo_ref,
                  o_ref, u_ref):
    j = pl.program_id(1)

    @pl.when(j == 0)
    def _():
        x = x_ref[...]
        u_ref[...] = _norm_rows(x, g_ref[...]).astype(BF16)
        o_ref[...] = x

    u = u_ref[...]
    ga = _dot_nt(u, wga_ref[...])
    gb = _dot_nt(u, wgb_ref[...])
    pa = _dot(ya_ref[...], wa_ref[...])
    pb = _dot(yb_ref[...], wb_ref[...])
    mix = _sigmoid(ga) * pa + _sigmoid(gb) * pb
    o_ref[...] += _dot(mix.astype(BF16), wo_ref[...])


def _merge(x2d, g, ya, yb, wg_t, wa, wb, wo, *, tm, tj):
    m, d = x2d.shape
    rowblk = lambda i, j: (i, 0)
    colblk = lambda i, j: (0, j)
    return pl.pallas_call(
        _merge_kernel,
        grid=(m // tm, d // tj),
        in_specs=[
            pl.BlockSpec((tm, d), rowblk),
            pl.BlockSpec((1, d), lambda i, j: (0, 0)),
            pl.BlockSpec((tm, d), rowblk),
            pl.BlockSpec((tm, d), rowblk),
            pl.BlockSpec((tj, d), lambda i, j: (j, 0)),
            pl.BlockSpec((tj, d), lambda i, j: (j + d // tj, 0)),
            pl.BlockSpec((d, tj), colblk),
            pl.BlockSpec((d, tj), colblk),
            pl.BlockSpec((tj, d), lambda i, j: (j, 0)),
        ],
        out_specs=pl.BlockSpec((tm, d), rowblk),
        out_shape=jax.ShapeDtypeStruct((m, d), F32),
        scratch_shapes=[pltpu.VMEM((tm, d), BF16)],
        compiler_params=_params("parallel", "arbitrary"),
        name="merge",
    )(x2d, g, ya, yb, wg_t, wg_t, wa, wb, wo)


def _mlp_kernel(h_ref, g_ref, wu_ref, wd_ref, gf_ref, o_ref, u_ref):
    f = pl.program_id(1)

    @pl.when(f == 0)
    def _():
        x = h_ref[...]
        u_ref[...] = _norm_rows(x, g_ref[...]).astype(BF16)
        o_ref[...] = x

    a = jnp.maximum(_dot(u_ref[...], wu_ref[...]), 0.0)
    o_ref[...] += _dot((a * a).astype(BF16), wd_ref[...])

    @pl.when(f == pl.num_programs(1) - 1)
    def _():
        o_ref[...] = _norm_rows(o_ref[...], gf_ref[...])


def _mlp(h2d, g, wu, wd, gf, *, tm, tf):
    m, d = h2d.shape
    dff = wu.shape[1]
    rowblk = lambda i, f: (i, 0)
    return pl.pallas_call(
        _mlp_kernel,
        grid=(m // tm, dff // tf),
        in_specs=[
            pl.BlockSpec((tm, d), rowblk),
            pl.BlockSpec((1, d), lambda i, f: (0, 0)),
            pl.BlockSpec((d, tf), lambda i, f: (0, f)),
            pl.BlockSpec((tf, d), lambda i, f: (f, 0)),
            pl.BlockSpec((1, d), lambda i, f: (0, 0)),
        ],
        out_specs=pl.BlockSpec((tm, d), rowblk),
        out_shape=jax.ShapeDtypeStruct((m, d), F32),
        scratch_shapes=[pltpu.VMEM((tm, d), BF16)],
        compiler_params=_params("parallel", "arbitrary"),
        name="mlp",
    )(h2d, g, wu, wd, gf)


def _constants():
    r = np.arange(128)
    ltri = (r[:, None] >= r[None, :]).astype(np.float32)
    ltri3 = np.concatenate([ltri, ltri, ltri], axis=1)

    sq = np.zeros((3 * LANES, LANES), np.float32)
    sk = np.zeros((3 * LANES, LANES), np.float32)
    cq = np.zeros((1, LANES), np.float32)
    ck = np.zeros((1, LANES), np.float32)
    for h in range(HEADS):
        for p in range(3):
            sq[p * LANES + SM_F + h, FOX_EXTRA * h + p] = 1.0
            sk[p * LANES + SM_F + h, FOX_EXTRA * h + 3 + p] = -1.0
            cq[0, FOX_EXTRA * h + 3 + p] = 1.0
            ck[0, FOX_EXTRA * h + p] = 1.0
        ck[0, FOX_EXTRA * h + FOX_M_LANE] = 1.0

    shift3 = np.zeros((3 * CHUNK, 2 * CHUNK), np.float32)
    for s in range(1, CONV_K):
        shift3[(s - 1) * CHUNK + r, CHUNK + r - s] = 1.0

    src_lane = np.zeros((HEADS * N_GQ * LANES,), np.int64)
    for h in range(HEADS):
        for q in range(N_GQ):
            src_lane[(h * N_GQ + q) * LANES:(h * N_GQ + q + 1) * LANES] = 16 * q + h
    piece_lane = np.arange(GQ_PIECES * LANES) % LANES
    sel = (piece_lane[:, None] == src_lane[None, :]).astype(np.float32)
    oneh = (np.repeat(16 * GQ_G + np.arange(HEADS), ROW_REP)[:, None]
            == piece_lane[None, :]).astype(np.float32)

    as_bf = lambda a: jnp.asarray(a, BF16)
    return dict(ltri3=as_bf(ltri3), sq=as_bf(sq), sk=as_bf(sk), cq=jnp.asarray(cq),
                ck=jnp.asarray(ck), shift3=as_bf(shift3), sel=as_bf(sel), oneh=as_bf(oneh))


def _row128(vec, offset):
    return jnp.zeros((1, LANES), F32).at[0, offset:offset + HEADS].set(vec.astype(F32))


def _pick_rows_tile(total, target):
    best = 128
    for t in range(128, target + 1, 128):
        if total % t == 0:
            best = t
    return best


def _layer(x, meta_tokens, mix_norm_g, w_in, conv_w, a_log, dt_bias, gdn_norm_g, w_o_gdn,
           fox_q_norm_g, fox_k_norm_g, fox_f_bias, w_o_fox, w_out, mlp_norm_g, w_up, w_down,
           final_norm_g):
    b, seq, d = x.shape
    lp = CHUNK + seq
    qk = HEADS * HEAD_DIM

    o_z = 3 * qk
    o_b = o_z + qk
    o_a = o_b + HEADS
    o_fq = o_a + HEADS
    o_f = o_fq + 3 * qk
    o_ga = o_f + HEADS
    o_gb = o_ga + d

    wt = jnp.swapaxes(w_in, 1, 2)
    row = lambda v: v.reshape(1, -1).astype(F32)
    cst = _constants()

    hn = _prenorm(x, meta_tokens.astype(x.dtype), row(mix_norm_g))
    p1, ps = _inproj(hn.reshape(b * lp, d), wt, o_b, o_fq, o_f - o_fq,
                     ((o_b, o_fq - o_b), (o_f, o_ga - o_f)),
                     tm=_pick_rows_tile(lp, 1408), tn=1024)
    p1 = p1.reshape(b, lp, P1_COLS)
    ps = ps.reshape(b, lp, LANES)

    yp, eq, ek = _gate_prep(ps, _row128(a_log, SM_A), _row128(dt_bias, SM_A),
                            _row128(fox_f_bias, SM_F), cst["ltri3"], cst["sq"], cst["sk"],
                            cst["cq"], cst["ck"])
    ya = _gdn(p1, yp, cst["sel"], cst["oneh"], conv_w.astype(F32), row(gdn_norm_g),
              cst["shift3"], seq=seq, hg=16)
    yb, (wog, wof, wout, wup, wdown, wg_t) = _fox(
        p1, eq, ek, row(fox_q_norm_g), row(fox_k_norm_g),
        [w_o_gdn, w_o_fox, w_out, w_up, w_down], wt, (o_ga, 2 * d), seq=seq, tq=1024)

    x2d = x.reshape(b * seq, d)
    h1 = _merge(x2d, row(mix_norm_g), ya.reshape(b * seq, qk), yb.reshape(b * seq, qk),
                wg_t, wog, wof, wout, tm=512, tj=512)
    out = _mlp(h1, row(mlp_norm_g), wup, wdown, row(final_norm_g), tm=512, tf=1024)
    return out.reshape(b, seq, d)


def kernel(x, meta_tokens, mix_norm_g, w_in, conv_w, a_log, dt_bias, gdn_norm_g, w_o_gdn,
           fox_q_norm_g, fox_k_norm_g, fox_f_bias, w_o_fox, w_out, mlp_norm_g, w_up, w_down,
           final_norm_g):
    assert w_in.shape[0] == 1, "single-layer block"
    return _layer(x, meta_tokens, mix_norm_g[0], w_in, conv_w[0], a_log[0], dt_bias[0],
                  gdn_norm_g[0], w_o_gdn, fox_q_norm_g[0], fox_k_norm_g[0], fox_f_bias[0],
                  w_o_fox, w_out, mlp_norm_g[0], w_up, w_down, final_norm_g)
```

```python
import functools
import math

import numpy as np
import jax
import jax.numpy as jnp
from jax import lax
from jax.experimental import pallas as pl
from jax.experimental.pallas import tpu as pltpu

F32 = jnp.float32
BF16 = jnp.bfloat16

NORM_EPS = 1e-6
N_META = 16
HEADS = 16
HEAD_DIM = 128
CONV_K = 4
LANES = 128
CHUNK = 128
FRONT_PAD = CHUNK - N_META
NEG_BIG = -1e30
LOG2E = math.log2(math.e)
VMEM_LIMIT = 56 * 1024 * 1024

COL_GDN_Q, COL_GDN_K, COL_GDN_V, COL_Z = 0, 16, 32, 48
COL_FOX_Q, COL_FOX_K, COL_FOX_V = 64, 80, 96
P1_COLS = 112 * LANES
SM_B, SM_A, SM_F = 0, 16, 32
GQ_G, GQ_BETA = range(2)
N_GQ = 2
GQ_PIECES = 2
ROW_REP = 8
FOX_EXTRA = 8
FOX_M_LANE = 6
FOX_SAFE_BOUND = 40.0


def _params(*sem):
    return pltpu.CompilerParams(dimension_semantics=sem, vmem_limit_bytes=VMEM_LIMIT)


def _norm_rows(x, g):
    ms = jnp.mean(x * x, axis=-1, keepdims=True)
    return x * lax.rsqrt(ms + NORM_EPS) * g


def _split3(x):
    hi = x.astype(BF16)
    r = x - hi.astype(F32)
    mid = r.astype(BF16)
    lo = (r - mid.astype(F32)).astype(BF16)
    return hi, mid, lo


def _dot(a, b):
    return jnp.dot(a, b, preferred_element_type=F32)


def _dot_nt(a, b):
    return lax.dot_general(a, b, (((1,), (1,)), ((), ())), preferred_element_type=F32)


def _dot_tn(a, b):
    return lax.dot_general(a, b, (((0,), (0,)), ((), ())), preferred_element_type=F32)


def _sigmoid(x):
    return 1.0 / (1.0 + jnp.exp(-x))


def _silu(x):
    return x * _sigmoid(x)


def _softplus(x):
    return jnp.maximum(x, 0.0) + jnp.log1p(jnp.exp(-jnp.abs(x)))


def _prenorm_kernel(x_ref, meta_ref, g_ref, o_ref):
    r = pl.program_id(1)

    @pl.when(r == 0)
    def _():
        o_ref[0:FRONT_PAD, :] = jnp.zeros((FRONT_PAD, o_ref.shape[1]), o_ref.dtype)
        o_ref[FRONT_PAD:CHUNK, :] = _norm_rows(meta_ref[...], g_ref[...]).astype(o_ref.dtype)

    @pl.when(r > 0)
    def _():
        o_ref[...] = _norm_rows(x_ref[...], g_ref[...]).astype(o_ref.dtype)


def _prenorm(x, meta_tokens, g):
    b, seq, d = x.shape
    lp = CHUNK + seq
    return pl.pallas_call(
        _prenorm_kernel,
        grid=(b, lp // CHUNK),
        in_specs=[
            pl.BlockSpec((None, CHUNK, d), lambda bi, r: (bi, jnp.maximum(r - 1, 0), 0)),
            pl.BlockSpec((N_META, d), lambda bi, r: (0, 0)),
            pl.BlockSpec((1, d), lambda bi, r: (0, 0)),
        ],
        out_specs=pl.BlockSpec((None, CHUNK, d), lambda bi, r: (bi, r, 0)),
        out_shape=jax.ShapeDtypeStruct((b, lp, d), BF16),
        compiler_params=_params("parallel", "arbitrary"),
        name="prenorm",
    )(x, meta_tokens, g)


def _inproj_kernel(x_ref, w_ref, wsa_ref, wsb_ref, gq_ref, gk_ref, o_ref, os_ref, *,
                   q_tiles, k_tiles, q_scale):
    j = pl.program_id(1)

    @pl.when(j == 0)
    def _():
        pad = LANES - wsa_ref.shape[0] - wsb_ref.shape[0]
        ws = jnp.concatenate([wsa_ref[...], wsb_ref[...], jnp.zeros((pad, wsa_ref.shape[1]), F32)],
                             axis=0)
        os_ref[...] = _dot_nt(x_ref[...], ws.astype(BF16))

    def project():
        return _dot_nt(x_ref[...], w_ref[...].astype(BF16))

    def project_head_norm(g_ref, scale):
        acc = project()
        for h in range(o_ref.shape[1] // HEAD_DIM):
            cols = slice(h * HEAD_DIM, (h + 1) * HEAD_DIM)
            o_ref[:, cols] = (_norm_rows(acc[:, cols], g_ref[...]) * scale).astype(o_ref.dtype)

    is_q = jnp.logical_and(j >= q_tiles[0], j < q_tiles[1])
    is_k = jnp.logical_and(j >= k_tiles[0], j < k_tiles[1])

    @pl.when(is_q)
    def _():
        project_head_norm(gq_ref, q_scale)

    @pl.when(is_k)
    def _():
        project_head_norm(gk_ref, 1.0)

    @pl.when(jnp.logical_not(jnp.logical_or(is_q, is_k)))
    def _():
        o_ref[...] = project().astype(o_ref.dtype)


F32_SUBLANES = 8


def _element_rows(tile, d, start_tiles):
    return pl.BlockSpec((None, pl.Element(tile), pl.Element(d)),
                        lambda *ids: (0, F32_SUBLANES * start_tiles(*ids), 0))


def _inproj(hn, wt, n_a, b_start, n_b, small, gq, gk, q_scale, *, tm, tn):
    m, d = hn.shape
    na, nb = n_a // tn, n_b // tn
    assert nb % 3 == 0
    q_tiles = (na, na + nb // 3)
    k_tiles = (na + nb // 3, na + 2 * nb // 3)
    (s0, c0), (s1, c1) = small
    sub = F32_SUBLANES
    assert all(v % sub == 0 for v in (tn, b_start - n_a, s0, s1))

    def w_rows(i, j):
        return j * (tn // sub) + jnp.where(j >= na, (b_start - n_a) // sub, 0)

    return pl.pallas_call(
        functools.partial(_inproj_kernel, q_tiles=q_tiles, k_tiles=k_tiles, q_scale=q_scale),
        grid=(m // tm, na + nb),
        in_specs=[
            pl.BlockSpec((tm, d), lambda i, j: (i, 0)),
            _element_rows(tn, d, w_rows),
            _element_rows(c0, d, lambda i, j: s0 // sub),
            _element_rows(c1, d, lambda i, j: s1 // sub),
            pl.BlockSpec((1, HEAD_DIM), lambda i, j: (0, 0)),
            pl.BlockSpec((1, HEAD_DIM), lambda i, j: (0, 0)),
        ],
        out_specs=[
            pl.BlockSpec((tm, tn), lambda i, j: (i, j)),
            pl.BlockSpec((tm, LANES), lambda i, j: (i, 0)),
        ],
        out_shape=[
            jax.ShapeDtypeStruct((m, (na + nb) * tn), BF16),
            jax.ShapeDtypeStruct((m, LANES), F32),
        ],
        compiler_params=_params("parallel", "arbitrary"),
        name="inproj",
    )(hn, wt, wt, wt, gq, gk)


def _gate_prep_kernel(ps_ref, alog_ref, dtb_ref, fb_ref, ltri_ref, sq_ref, sk_ref, cq_ref, ck_ref,
                      yp_ref, eq_ref, ek_ref, *, nblk):
    lane = lax.broadcasted_iota(jnp.int32, (CHUNK, LANES), 1)
    rowi = lax.broadcasted_iota(jnp.int32, (CHUNK, LANES), 0)

    def body(i, carry):
        rows = pl.ds(pl.multiple_of(i * CHUNK, CHUNK), CHUNK)
        x = ps_ref[rows, :]
        valid = (i * CHUNK + rowi) >= FRONT_PAD

        beta = jnp.where(valid, _sigmoid(x), 0.0)
        gstep = jnp.where(valid, -jnp.exp(alog_ref[...]) * _softplus(x + dtb_ref[...]), 0.0)
        gcum = _dot(ltri_ref[...], jnp.concatenate(_split3(gstep), axis=0))
        y = jnp.where(lane < 16 * (GQ_G + 1), pltpu.roll(gcum, LANES + 16 * GQ_G - SM_A, 1),
                      pltpu.roll(beta, 16 * GQ_BETA - SM_B, 1))
        yp_ref[rows, :] = jnp.concatenate(_split3(y)[:GQ_PIECES], axis=1)

        xf = x + fb_ref[...]
        ls = (jnp.minimum(xf, 0.0) - jnp.log1p(jnp.exp(-jnp.abs(xf)))) * LOG2E
        cum = _dot(ltri_ref[...], jnp.concatenate(_split3(ls), axis=0)) + carry
        cp = jnp.concatenate(_split3(cum), axis=1)
        eq_ref[rows, :] = (_dot(cp, sq_ref[...]) + cq_ref[...]).astype(BF16)
        ek_ref[rows, :] = (_dot(cp, sk_ref[...]) + ck_ref[...]).astype(BF16)
        return cum[CHUNK - 1:CHUNK, :]

    lax.fori_loop(0, nblk, body, jnp.zeros((1, LANES), F32))


def _gate_prep(ps, alog_row, dtb_row, fb_row, ltri3, sq, sk, cq, ck):
    b, lp, _ = ps.shape
    const = lambda bi: (0, 0)
    return pl.pallas_call(
        functools.partial(_gate_prep_kernel, nblk=lp // CHUNK),
        grid=(b,),
        in_specs=[
            pl.BlockSpec((None, lp, LANES), lambda bi: (bi, 0, 0)),
            pl.BlockSpec((1, LANES), const),
            pl.BlockSpec((1, LANES), const),
            pl.BlockSpec((1, LANES), const),
            pl.BlockSpec((CHUNK, 3 * CHUNK), const),
            pl.BlockSpec((3 * LANES, LANES), const),
            pl.BlockSpec((3 * LANES, LANES), const),
            pl.BlockSpec((1, LANES), const),
            pl.BlockSpec((1, LANES), const),
        ],
        out_specs=[
            pl.BlockSpec((None, lp, GQ_PIECES * LANES), lambda bi: (bi, 0, 0)),
            pl.BlockSpec((None, lp, LANES), lambda bi: (bi, 0, 0)),
            pl.BlockSpec((None, lp, LANES), lambda bi: (bi, 0, 0)),
        ],
        out_shape=[
            jax.ShapeDtypeStruct((b, lp, GQ_PIECES * LANES), BF16),
            jax.ShapeDtypeStruct((b, lp, LANES), BF16),
            jax.ShapeDtypeStruct((b, lp, LANES), BF16),
        ],
        compiler_params=_params("parallel"),
        name="gate_prep",
    )(ps, alog_row, dtb_row, fb_row, ltri3, sq, sk, cq, ck)


def _gdn_kernel(q_ref, k_ref, v_ref, z_ref, yp_ref, sel_ref, oneh_ref, wq_ref, wk_ref, wv_ref,
                gn_ref, shift_ref, o_ref, s_ref, prev_ref, *, hg):
    c = pl.program_id(2)

    @pl.when(c == 0)
    def _():
        s_ref[...] = jnp.zeros_like(s_ref)
        prev_ref[...] = jnp.zeros_like(prev_ref)

    row = lax.broadcasted_iota(jnp.int32, (CHUNK, CHUNK), 0)
    col = lax.broadcasted_iota(jnp.int32, (CHUNK, CHUNK), 1)
    tril = row >= col
    strict = row > col
    heads = range(hg)
    hsl = lambda a, u: a[:, u * HEAD_DIM:(u + 1) * HEAD_DIM]

    yp = yp_ref[...]
    gb = _dot(yp, sel_ref[...])
    g_rows = _dot_nt(oneh_ref[...], yp)
    gq = lambda u, q: gb[:, (u * N_GQ + q) * LANES:(u * N_GQ + q + 1) * LANES]
    g_col = [gq(u, GQ_G) for u in heads]
    beta = [gq(u, GQ_BETA) for u in heads]
    eg = [jnp.exp(g_col[u]) for u in heads]

    def conv_act(x_ref, w_ref, t):
        x16 = x_ref[...]
        x2 = jnp.concatenate([prev_ref[t], x16], axis=0)
        prev_ref[t] = x16
        sh = _dot(shift_ref[...], x2)
        w = w_ref[...]
        acc = x16.astype(F32) * w[CONV_K - 1:CONV_K, :]
        for s in range(1, CONV_K):
            acc = acc + sh[(s - 1) * CHUNK:s * CHUNK, :] * w[CONV_K - 1 - s:CONV_K - s, :]
        return _silu(acc)

    qa = conv_act(q_ref, wq_ref, 0)
    ka = conv_act(k_ref, wk_ref, 1)
    va = conv_act(v_ref, wv_ref, 2)

    def l2n(x):
        return x * lax.rsqrt(jnp.sum(x * x, axis=-1, keepdims=True) + NORM_EPS)

    qn = [l2n(hsl(qa, u)) * (HEAD_DIM ** -0.5) for u in heads]
    kn = [l2n(hsl(ka, u)) for u in heads]
    k16 = [kn[u].astype(BF16) for u in heads]

    state = [s_ref[u] for u in heads]
    kb = [kn[u] * beta[u] for u in heads]
    r1 = [_dot(jnp.concatenate([kb[u] * eg[u], qn[u] * eg[u]], axis=0).astype(BF16),
               state[u].astype(BF16)) for u in heads]

    kq = [_dot_nt(jnp.concatenate([kb[u], qn[u]], axis=0).astype(BF16), k16[u])
          for u in heads]
    ediff = [jnp.exp(jnp.minimum(g_col[u] - g_rows[u * ROW_REP:u * ROW_REP + 1, :], 0.0))
             for u in heads]
    a = [kq[u][:CHUNK] * jnp.where(strict, ediff[u], 0.0) for u in heads]
    qk_m = [(kq[u][CHUNK:] * jnp.where(tril, ediff[u], 0.0)).astype(BF16) for u in heads]

    a16 = [a[u].astype(BF16) for u in heads]
    yk = [_dot(a16[u], a16[u]) for u in heads]
    sk = [-a[u] for u in heads]
    nlev = int(np.log2(CHUNK)) - 1
    for lev in range(nlev):
        yk16 = [yk[u].astype(BF16) for u in heads]
        if lev < nlev - 1:
            zz = [_dot(jnp.concatenate([sk[u], yk[u]], axis=0).astype(BF16), yk16[u]) for u in heads]
            sk = [sk[u] + yk[u] + zz[u][:CHUNK] for u in heads]
            yk = [zz[u][CHUNK:] for u in heads]
        else:
            sk = [sk[u] + yk[u] + _dot(sk[u].astype(BF16), yk16[u]) for u in heads]

    rhs = [hsl(va, u) * beta[u] - r1[u][:CHUNK] for u in heads]
    v_new = [(rhs[u] + _dot(sk[u].astype(BF16), rhs[u].astype(BF16))).astype(BF16) for u in heads]
    o = [r1[u][CHUNK:] + _dot(qk_m[u], v_new[u]) for u in heads]
    for u in heads:
        k_dec = kn[u] * jnp.exp(g_col[u][CHUNK - 1:CHUNK, :] - g_col[u])
        s_ref[u] = state[u] * eg[u][CHUNK - 1:CHUNK, :] + _dot_tn(k_dec.astype(BF16), v_new[u])

    on = jnp.concatenate([_norm_rows(o[u], gn_ref[...]) for u in heads], axis=1)
    o_ref[...] = (on * _silu(z_ref[...].astype(F32))).astype(o_ref.dtype)


def _gdn(p1, yp, sel, oneh, conv_w, gnorm, shift3, *, seq, hg):
    b, lp, _ = p1.shape
    nc = lp // CHUNK
    gw = hg * HEAD_DIM
    ng = HEADS // hg
    col = lambda base: (lambda bi, gi, ci: (bi, ci, base * LANES // gw + gi))
    wcol = lambda base: (lambda bi, gi, ci: (0, base * LANES // gw + gi))
    const = lambda bi, gi, ci: (0, 0)
    return pl.pallas_call(
        functools.partial(_gdn_kernel, hg=hg),
        grid=(b, ng, nc),
        in_specs=[
            pl.BlockSpec((None, CHUNK, gw), col(COL_GDN_Q)),
            pl.BlockSpec((None, CHUNK, gw), col(COL_GDN_K)),
            pl.BlockSpec((None, CHUNK, gw), col(COL_GDN_V)),
            pl.BlockSpec((None, CHUNK, gw), col(COL_Z)),
            pl.BlockSpec((None, CHUNK, GQ_PIECES * LANES), lambda bi, gi, ci: (bi, ci, 0)),
            pl.BlockSpec((GQ_PIECES * LANES, hg * N_GQ * LANES), lambda bi, gi, ci: (0, gi)),
            pl.BlockSpec((hg * ROW_REP, GQ_PIECES * LANES), lambda bi, gi, ci: (gi, 0)),
            pl.BlockSpec((CONV_K, gw), wcol(COL_GDN_Q)),
            pl.BlockSpec((CONV_K, gw), wcol(COL_GDN_K)),
            pl.BlockSpec((CONV_K, gw), wcol(COL_GDN_V)),
            pl.BlockSpec((1, HEAD_DIM), const),
            pl.BlockSpec((3 * CHUNK, 2 * CHUNK), const),
        ],
        out_specs=pl.BlockSpec((None, CHUNK, gw),
                               lambda bi, gi, ci: (bi, jnp.maximum(ci - 1, 0), gi)),
        out_shape=jax.ShapeDtypeStruct((b, seq, HEADS * HEAD_DIM), BF16),
        scratch_shapes=[
            pltpu.VMEM((hg, HEAD_DIM, HEAD_DIM), F32),
            pltpu.VMEM((3, CHUNK, gw), BF16),
        ],
        compiler_params=_params("parallel", "parallel", "arbitrary"),
        name="gdn",
    )(p1, p1, p1, p1, yp, sel, oneh, conv_w, conv_w, conv_w, gnorm, shift3)


def _fox_kernel(*refs, tq, lp, n_cast):
    q_ref, k_ref, v_ref, eq_ref, ek_ref, gq_ref, gk_ref = refs[:7]
    o_ref = refs[7 + n_cast]
    kaug_ref, vaug_ref = refs[-2:]
    h = pl.program_id(1)
    i = pl.program_id(2)

    for w_ref, w16_ref in zip(refs[7:7 + n_cast], refs[8 + n_cast:8 + 2 * n_cast]):
        w16_ref[...] = w_ref[...].astype(w16_ref.dtype)

    @pl.when(i == 0)
    def _():
        def build(rows, n):
            kaug_ref[rows, 0:HEAD_DIM] = k_ref[rows, :]
            kaug_ref[rows, HEAD_DIM:2 * HEAD_DIM] = ek_ref[rows, :]
            vaug_ref[rows, 0:HEAD_DIM] = v_ref[rows, :]
            vaug_ref[rows, HEAD_DIM:2 * HEAD_DIM] = jnp.ones((n, HEAD_DIM), BF16)

        def body(r, c):
            build(pl.ds(pl.multiple_of(r * tq, 128), tq), tq)
            return c
        lax.fori_loop(0, lp // tq, body, 0)
        if lp % tq:
            build(pl.ds(lp - lp % tq, lp % tq), lp % tq)

    q0 = pl.multiple_of(CHUNK + i * tq, 128)
    rows_q = pl.ds(q0, tq)
    lane = lax.broadcasted_iota(jnp.int32, (tq, LANES), 1)
    mine = lax.shift_right_logical(lane, 3) == h
    eqm = jnp.where(mine, eq_ref[rows_q, :], jnp.zeros((), BF16))
    q16 = q_ref[rows_q, :]

    gmax = (jnp.max(jnp.abs(gq_ref[...]), axis=-1, keepdims=True)
            * jnp.max(jnp.abs(gk_ref[...]), axis=-1, keepdims=True))
    bound = (gmax * (HEAD_DIM ** 0.5 * LOG2E * 1.02)).astype(BF16)
    safe = bound.astype(F32)[0, 0] <= FOX_SAFE_BOUND

    rr = lax.broadcasted_iota(jnp.int32, (tq, tq), 0)
    cc = lax.broadcasted_iota(jnp.int32, (tq, tq), 1)
    causal = rr >= cc

    def blocks(j):
        return pl.ds(pl.multiple_of(CHUNK + j * tq, 128), tq)

    @pl.when(safe)
    def _():
        eqb = jnp.where(lane == FOX_EXTRA * h + FOX_M_LANE, -bound, eqm)
        qaug = jnp.concatenate([q16, eqb], axis=1)
        half = tq // 2

        def probs(qa, krows, mask=None):
            s = _dot_nt(qa, kaug_ref[krows, :])
            if mask is not None:
                s = jnp.where(mask, s, NEG_BIG)
            return jnp.exp2(s).astype(BF16)

        def finish(acc, rows):
            o_ref[rows, :] = (acc[:, :HEAD_DIM] / acc[:, HEAD_DIM:]).astype(o_ref.dtype)

        for ii in range(pl.cdiv(lp - CHUNK, tq)):
            @pl.when(i == ii)
            def _(ii=ii):
                acc = _dot(probs(qaug, slice(0, CHUNK), lane >= FRONT_PAD), vaug_ref[0:CHUNK, :])
                for j in range(ii):
                    kr = slice(CHUNK + j * tq, CHUNK + (j + 1) * tq)
                    acc = acc + _dot(probs(qaug, kr), vaug_ref[kr, :])
                k_lo = slice(CHUNK + ii * tq, CHUNK + ii * tq + half)
                k_all = slice(CHUNK + ii * tq, CHUNK + (ii + 1) * tq)
                p_lo = probs(qaug[:half], k_lo, causal[:half, :half])
                p_hi = probs(qaug[half:], k_all, causal[half:, :])
                finish(acc[:half] + _dot(p_lo, vaug_ref[k_lo, :]), slice(0, half))
                finish(acc[half:] + _dot(p_hi, vaug_ref[k_all, :]), slice(half, tq))

    @pl.when(jnp.logical_not(safe))
    def _():
        qaug = jnp.concatenate([q16, eqm], axis=1)
        s = _dot_nt(qaug, kaug_ref[0:CHUNK, :])
        s = jnp.where(lane >= FRONT_PAD, s, NEG_BIG)
        m = jnp.max(s, axis=-1, keepdims=True)
        acc = _dot(jnp.exp2(s - m).astype(BF16), vaug_ref[0:CHUNK, :])

        def update(s, vrows, m, acc):
            m_new = jnp.maximum(m, jnp.max(s, axis=-1, keepdims=True))
            p = jnp.exp2(s - m_new)
            return m_new, jnp.exp2(m - m_new) * acc + _dot(p.astype(BF16), vrows)

        def body(j, carry):
            s = _dot_nt(qaug, kaug_ref[blocks(j), :])
            return update(s, vaug_ref[blocks(j), :], *carry)
        m, acc = lax.fori_loop(0, i, body, (m, acc))

        s = _dot_nt(qaug, kaug_ref[rows_q, :])
        m, acc = update(jnp.where(causal, s, NEG_BIG), vaug_ref[rows_q, :], m, acc)
        o_ref[...] = (acc[:, :HEAD_DIM] / acc[:, HEAD_DIM:]).astype(o_ref.dtype)


def _fox(p1, eq, ek, gq, gk, weights, wt, wt_rows, *, seq, tq):
    b, lp, _ = p1.shape
    nq = seq // tq
    nsteps = b * HEADS * nq
    colmap = lambda base: (lambda bi, hi, qi: (bi, 0, base + hi))
    const = lambda bi, hi, qi: (0, 0)
    step = lambda bi, hi, qi: (bi * HEADS + hi) * nq + qi

    cast_in, cast_out, cast_shapes = [], [], []
    for w in weights:
        _, r, c = w.shape
        rp = r // nsteps
        assert r % nsteps == 0 and rp % 16 == 0
        cast_in.append(pl.BlockSpec((None, rp, c), lambda bi, hi, qi: (0, step(bi, hi, qi), 0)))
        cast_out.append(pl.BlockSpec((rp, c), lambda bi, hi, qi: (step(bi, hi, qi), 0)))
        cast_shapes.append(jax.ShapeDtypeStruct((r, c), BF16))
    start, n = wt_rows
    rp = n // nsteps
    assert n % nsteps == 0 and rp % 16 == 0 and start % F32_SUBLANES == 0
    cast_in.append(_element_rows(rp, wt.shape[2], lambda bi, hi, qi: (
        start // F32_SUBLANES + step(bi, hi, qi) * (rp // F32_SUBLANES))))
    cast_out.append(pl.BlockSpec((rp, wt.shape[2]), lambda bi, hi, qi: (step(bi, hi, qi), 0)))
    cast_shapes.append(jax.ShapeDtypeStruct((n, wt.shape[2]), BF16))

    outs = pl.pallas_call(
        functools.partial(_fox_kernel, tq=tq, lp=lp, n_cast=len(cast_in)),
        grid=(b, HEADS, nq),
        in_specs=[
            pl.BlockSpec((None, lp, HEAD_DIM), colmap(COL_FOX_Q)),
            pl.BlockSpec((None, lp, HEAD_DIM), colmap(COL_FOX_K)),
            pl.BlockSpec((None, lp, HEAD_DIM), colmap(COL_FOX_V)),
            pl.BlockSpec((None, lp, LANES), lambda bi, hi, qi: (bi, 0, 0)),
            pl.BlockSpec((None, lp, LANES), lambda bi, hi, qi: (bi, 0, 0)),
            pl.BlockSpec((1, HEAD_DIM), const),
            pl.BlockSpec((1, HEAD_DIM), const),
        ] + cast_in,
        out_specs=[pl.BlockSpec((None, tq, HEAD_DIM), lambda bi, hi, qi: (bi, qi, hi))] + cast_out,
        out_shape=[jax.ShapeDtypeStruct((b, seq, HEADS * HEAD_DIM), BF16)] + cast_shapes,
        scratch_shapes=[
            pltpu.VMEM((lp, 2 * HEAD_DIM), BF16),
            pltpu.VMEM((lp, 2 * HEAD_DIM), BF16),
        ],
        compiler_params=_params("parallel", "parallel", "arbitrary"),
        name="fox",
    )(p1, p1, p1, eq, ek, gq, gk, *weights, wt)
    return outs[0], outs[1:]


def _merge_kernel(x_ref, g_ref, ya_ref, yb_ref, wga_ref, wgb_ref, wa_ref, wb_ref, wo_ref,
                  o_ref, u_ref):
    j = pl.program_id(1)

    @pl.when(j == 0)
    def _():
        x = x_ref[...]
        u_ref[...] = _norm_rows(x, g_ref[...]).astype(BF16)
        o_ref[...] = x

    u = u_ref[...]
    ga = _dot_nt(u, wga_ref[...])
    gb = _dot_nt(u, wgb_ref[...])
    pa = _dot(ya_ref[...], wa_ref[...])
    pb = _dot(yb_ref[...], wb_ref[...])
    mix = _sigmoid(ga) * pa + _sigmoid(gb) * pb
    o_ref[...] += _dot(mix.astype(BF16), wo_ref[...])


def _merge(x2d, g, ya, yb, wg_t, wa, wb, wo, *, tm, tj):
    m, d = x2d.shape
    rowblk = lambda i, j: (i, 0)
    colblk = lambda i, j: (0, j)
    return pl.pallas_call(
        _merge_kernel,
        grid=(m // tm, d // tj),
        in_specs=[
            pl.BlockSpec((tm, d), rowblk),
            pl.BlockSpec((1, d), lambda i, j: (0, 0)),
            pl.BlockSpec((tm, d), rowblk),
            pl.BlockSpec((tm, d), rowblk),
            pl.BlockSpec((tj, d), lambda i, j: (j, 0)),
            pl.BlockSpec((tj, d), lambda i, j: (j + d // tj, 0)),
            pl.BlockSpec((d, tj), colblk),
            pl.BlockSpec((d, tj), colblk),
            pl.BlockSpec((tj, d), lambda i, j: (j, 0)),
        ],
        out_specs=pl.BlockSpec((tm, d), rowblk),
        out_shape=jax.ShapeDtypeStruct((m, d), F32),
        scratch_shapes=[pltpu.VMEM((tm, d), BF16)],
        compiler_params=_params("parallel", "arbitrary"),
        name="merge",
    )(x2d, g, ya, yb, wg_t, wg_t, wa, wb, wo)


def _mlp_kernel(h_ref, g_ref, wu_ref, wd_ref, gf_ref, o_ref, u_ref):
    f = pl.program_id(1)

    @pl.when(f == 0)
    def _():
        x = h_ref[...]
        u_ref[...] = _norm_rows(x, g_ref[...]).astype(BF16)
        o_ref[...] = x

    a = jnp.maximum(_dot(u_ref[...], wu_ref[...]), 0.0)
    o_ref[...] += _dot((a * a).astype(BF16), wd_ref[...])

    @pl.when(f == pl.num_programs(1) - 1)
    def _():
        o_ref[...] = _norm_rows(o_ref[...], gf_ref[...])


def _mlp(h2d, g, wu, wd, gf, *, tm, tf):
    m, d = h2d.shape
    dff = wu.shape[1]
    rowblk = lambda i, f: (i, 0)
    return pl.pallas_call(
        _mlp_kernel,
        grid=(m // tm, dff // tf),
        in_specs=[
            pl.BlockSpec((tm, d), rowblk),
            pl.BlockSpec((1, d), lambda i, f: (0, 0)),
            pl.BlockSpec((d, tf), lambda i, f: (0, f)),
            pl.BlockSpec((tf, d), lambda i, f: (f, 0)),
            pl.BlockSpec((1, d), lambda i, f: (0, 0)),
        ],
        out_specs=pl.BlockSpec((tm, d), rowblk),
        out_shape=jax.ShapeDtypeStruct((m, d), F32),
        scratch_shapes=[pltpu.VMEM((tm, d), BF16)],
        compiler_params=_params("parallel", "arbitrary"),
        name="mlp",
    )(h2d, g, wu, wd, gf)


def _constants():
    r = np.arange(128)
    ltri = (r[:, None] >= r[None, :]).astype(np.float32)
    ltri3 = np.concatenate([ltri, ltri, ltri], axis=1)

    sq = np.zeros((3 * LANES, LANES), np.float32)
    sk = np.zeros((3 * LANES, LANES), np.float32)
    cq = np.zeros((1, LANES), np.float32)
    ck = np.zeros((1, LANES), np.float32)
    for h in range(HEADS):
        for p in range(3):
            sq[p * LANES + SM_F + h, FOX_EXTRA * h + p] = 1.0
            sk[p * LANES + SM_F + h, FOX_EXTRA * h + 3 + p] = -1.0
            cq[0, FOX_EXTRA * h + 3 + p] = 1.0
            ck[0, FOX_EXTRA * h + p] = 1.0
        ck[0, FOX_EXTRA * h + FOX_M_LANE] = 1.0

    shift3 = np.zeros((3 * CHUNK, 2 * CHUNK), np.float32)
    for s in range(1, CONV_K):
        shift3[(s - 1) * CHUNK + r, CHUNK + r - s] = 1.0

    src_lane = np.zeros((HEADS * N_GQ * LANES,), np.int64)
    for h in range(HEADS):
        for q in range(N_GQ):
            src_lane[(h * N_GQ + q) * LANES:(h * N_GQ + q + 1) * LANES] = 16 * q + h
    piece_lane = np.arange(GQ_PIECES * LANES) % LANES
    sel = (piece_lane[:, None] == src_lane[None, :]).astype(np.float32)
    oneh = (np.repeat(16 * GQ_G + np.arange(HEADS), ROW_REP)[:, None]
            == piece_lane[None, :]).astype(np.float32)

    as_bf = lambda a: jnp.asarray(a, BF16)
    return dict(ltri3=as_bf(ltri3), sq=as_bf(sq), sk=as_bf(sk), cq=jnp.asarray(cq),
                ck=jnp.asarray(ck), shift3=as_bf(shift3), sel=as_bf(sel), oneh=as_bf(oneh))


def _row128(vec, offset):
    return jnp.zeros((1, LANES), F32).at[0, offset:offset + HEADS].set(vec.astype(F32))


def _pick_rows_tile(total, target):
    best = 128
    for t in range(128, target + 1, 128):
        if total % t == 0:
            best = t
    return best


def _layer(x, meta_tokens, mix_norm_g, w_in, conv_w, a_log, dt_bias, gdn_norm_g, w_o_gdn,
           fox_q_norm_g, fox_k_norm_g, fox_f_bias, w_o_fox, w_out, mlp_norm_g, w_up, w_down,
           final_norm_g):
    b, seq, d = x.shape
    lp = CHUNK + seq
    qk = HEADS * HEAD_DIM

    o_z = 3 * qk
    o_b = o_z + qk
    o_a = o_b + HEADS
    o_fq = o_a + HEADS
    o_f = o_fq + 3 * qk
    o_ga = o_f + HEADS
    o_gb = o_ga + d

    wt = jnp.swapaxes(w_in, 1, 2)
    row = lambda v: v.reshape(1, -1).astype(F32)
    cst = _constants()

    hn = _prenorm(x, meta_tokens.astype(x.dtype), row(mix_norm_g))
    p1, ps = _inproj(hn.reshape(b * lp, d), wt, o_b, o_fq, o_f - o_fq,
                     ((o_b, o_fq - o_b), (o_f, o_ga - o_f)),
                     row(fox_q_norm_g), row(fox_k_norm_g), HEAD_DIM ** -0.5 * LOG2E,
                     tm=_pick_rows_tile(lp, 1408), tn=1024)
    p1 = p1.reshape(b, lp, P1_COLS)
    ps = ps.reshape(b, lp, LANES)

    yp, eq, ek = _gate_prep(ps, _row128(a_log, SM_A), _row128(dt_bias, SM_A),
                            _row128(fox_f_bias, SM_F), cst["ltri3"], cst["sq"], cst["sk"],
                            cst["cq"], cst["ck"])
    ya = _gdn(p1, yp, cst["sel"], cst["oneh"], conv_w.astype(F32), row(gdn_norm_g),
              cst["shift3"], seq=seq, hg=16)
    yb, (wog, wof, wout, wup, wdown, wg_t) = _fox(
        p1, eq, ek, row(fox_q_norm_g), row(fox_k_norm_g),
        [w_o_gdn, w_o_fox, w_out, w_up, w_down], wt, (o_ga, 2 * d), seq=seq, tq=1024)

    x2d = x.reshape(b * seq, d)
    h1 = _merge(x2d, row(mix_norm_g), ya.reshape(b * seq, qk), yb.reshape(b * seq, qk),
                wg_t, wog, wof, wout, tm=512, tj=512)
    out = _mlp(h1, row(mlp_norm_g), wup, wdown, row(final_norm_g), tm=512, tf=1024)
    return out.reshape(b, seq, d)


def kernel(x, meta_tokens, mix_norm_g, w_in, conv_w, a_log, dt_bias, gdn_norm_g, w_o_gdn,
           fox_q_norm_g, fox_k_norm_g, fox_f_bias, w_o_fox, w_out, mlp_norm_g, w_up, w_down,
           final_norm_g):
    assert w_in.shape[0] == 1, "single-layer block"
    return _layer(x, meta_tokens, mix_norm_g[0], w_in, conv_w[0], a_log[0], dt_bias[0],
                  gdn_norm_g[0], w_o_gdn, fox_q_norm_g[0], fox_k_norm_g[0], fox_f_bias[0],
                  w_o_fox, w_out, mlp_norm_g[0], w_up, w_down, final_norm_g)
```

```python
import functools
import math

import numpy as np
import jax
import jax.numpy as jnp
from jax import lax
from jax.experimental import pallas as pl
from jax.experimental.pallas import tpu as pltpu

F32 = jnp.float32
BF16 = jnp.bfloat16

NORM_EPS = 1e-6
N_META = 16
HEADS = 16
HEAD_DIM = 128
CONV_K = 4
LANES = 128
F32_SUBLANES = 8
CHUNK = 128
FRONT_PAD = CHUNK - N_META
NEG_BIG = -1e30
LOG2E = math.log2(math.e)
VMEM_LIMIT = 56 * 1024 * 1024

COL_GDN_Q, COL_GDN_K, COL_GDN_V, COL_Z = 0, 16, 32, 48
COL_FOX_Q, COL_FOX_K, COL_FOX_V = 64, 80, 96
P1_COLS = 112 * LANES
SM_B, SM_A, SM_F = 0, 16, 32
GQ_G, GQ_BETA = range(2)
N_GQ = 2
GQ_PIECES = 2
ROW_REP = 8
FOX_EXTRA = 8
FOX_M_LANE = 6
FOX_SAFE_BOUND = 40.0


def _params(*sem):
    return pltpu.CompilerParams(dimension_semantics=sem, vmem_limit_bytes=VMEM_LIMIT)


def _norm_rows(x, g):
    ms = jnp.mean(x * x, axis=-1, keepdims=True)
    return x * lax.rsqrt(ms + NORM_EPS) * g


def _split3(x):
    hi = x.astype(BF16)
    r = x - hi.astype(F32)
    mid = r.astype(BF16)
    lo = (r - mid.astype(F32)).astype(BF16)
    return hi, mid, lo


def _dot(a, b):
    return jnp.dot(a, b, preferred_element_type=F32)


def _dot_nt(a, b):
    return lax.dot_general(a, b, (((1,), (1,)), ((), ())), preferred_element_type=F32)


def _dot_tn(a, b):
    return lax.dot_general(a, b, (((0,), (0,)), ((), ())), preferred_element_type=F32)


def _sigmoid(x):
    return 1.0 / (1.0 + jnp.exp(-x))


def _silu(x):
    return x * _sigmoid(x)


def _softplus(x):
    return jnp.maximum(x, 0.0) + jnp.log1p(jnp.exp(-jnp.abs(x)))


def _prenorm_kernel(x_ref, meta_ref, g_ref, o_ref, *, tile):
    t = pl.program_id(1)

    def norm_chunks(first_out, n, shift):
        def body(c, carry):
            dst = pl.ds(pl.multiple_of(first_out + c * CHUNK, CHUNK), CHUNK)
            src = pl.ds(pl.multiple_of(first_out - shift + c * CHUNK, CHUNK), CHUNK)
            o_ref[dst, :] = _norm_rows(x_ref[src, :], g_ref[...]).astype(o_ref.dtype)
            return carry
        lax.fori_loop(0, n, body, 0)

    @pl.when(t == 0)
    def _():
        o_ref[0:FRONT_PAD, :] = jnp.zeros((FRONT_PAD, o_ref.shape[1]), o_ref.dtype)
        o_ref[FRONT_PAD:CHUNK, :] = _norm_rows(meta_ref[...], g_ref[...]).astype(o_ref.dtype)
        norm_chunks(CHUNK, tile // CHUNK - 1, CHUNK)

    @pl.when(t > 0)
    def _():
        norm_chunks(0, tile // CHUNK, 0)


def _prenorm(x, meta_tokens, g, *, tile):
    b, seq, d = x.shape
    lp = CHUNK + seq
    sub = F32_SUBLANES
    assert lp % tile == 0 and tile % CHUNK == 0 and CHUNK % sub == 0
    x_rows = lambda bi, t: sub * jnp.maximum(t * (tile // sub) - CHUNK // sub, 0)
    return pl.pallas_call(
        functools.partial(_prenorm_kernel, tile=tile),
        grid=(b, lp // tile),
        in_specs=[
            pl.BlockSpec((None, pl.Element(tile), pl.Element(d)),
                         lambda bi, t: (bi, x_rows(bi, t), 0)),
            pl.BlockSpec((N_META, d), lambda bi, t: (0, 0)),
            pl.BlockSpec((1, d), lambda bi, t: (0, 0)),
        ],
        out_specs=pl.BlockSpec((None, tile, d), lambda bi, t: (bi, t, 0)),
        out_shape=jax.ShapeDtypeStruct((b, lp, d), BF16),
        compiler_params=_params("parallel", "arbitrary"),
        name="prenorm",
    )(x, meta_tokens, g)


def _inproj_kernel(x_ref, w_ref, wsa_ref, wsb_ref, gq_ref, gk_ref, o_ref, os_ref, *,
                   q_tiles, k_tiles, q_scale):
    j = pl.program_id(1)

    @pl.when(j == 0)
    def _():
        pad = LANES - wsa_ref.shape[0] - wsb_ref.shape[0]
        ws = jnp.concatenate([wsa_ref[...], wsb_ref[...], jnp.zeros((pad, wsa_ref.shape[1]), F32)],
                             axis=0)
        os_ref[...] = _dot_nt(x_ref[...], ws.astype(BF16))

    def project():
        return _dot_nt(x_ref[...], w_ref[...].astype(BF16))

    def project_head_norm(g_ref, scale):
        acc = project()
        for h in range(o_ref.shape[1] // HEAD_DIM):
            cols = slice(h * HEAD_DIM, (h + 1) * HEAD_DIM)
            o_ref[:, cols] = (_norm_rows(acc[:, cols], g_ref[...]) * scale).astype(o_ref.dtype)

    is_q = jnp.logical_and(j >= q_tiles[0], j < q_tiles[1])
    is_k = jnp.logical_and(j >= k_tiles[0], j < k_tiles[1])

    @pl.when(is_q)
    def _():
        project_head_norm(gq_ref, q_scale)

    @pl.when(is_k)
    def _():
        project_head_norm(gk_ref, 1.0)

    @pl.when(jnp.logical_not(jnp.logical_or(is_q, is_k)))
    def _():
        o_ref[...] = project().astype(o_ref.dtype)


def _element_rows(tile, d, start_tiles):
    return pl.BlockSpec((None, pl.Element(tile), pl.Element(d)),
                        lambda *ids: (0, F32_SUBLANES * start_tiles(*ids), 0))


def _inproj(hn, wt, n_a, b_start, n_b, small, gq, gk, q_scale, *, tm, tn):
    m, d = hn.shape
    na, nb = n_a // tn, n_b // tn
    assert nb % 3 == 0
    q_tiles = (na, na + nb // 3)
    k_tiles = (na + nb // 3, na + 2 * nb // 3)
    (s0, c0), (s1, c1) = small
    sub = F32_SUBLANES
    assert all(v % sub == 0 for v in (tn, b_start - n_a, s0, s1))

    def w_rows(i, j):
        return j * (tn // sub) + jnp.where(j >= na, (b_start - n_a) // sub, 0)

    return pl.pallas_call(
        functools.partial(_inproj_kernel, q_tiles=q_tiles, k_tiles=k_tiles, q_scale=q_scale),
        grid=(m // tm, na + nb),
        in_specs=[
            pl.BlockSpec((tm, d), lambda i, j: (i, 0)),
            _element_rows(tn, d, w_rows),
            _element_rows(c0, d, lambda i, j: s0 // sub),
            _element_rows(c1, d, lambda i, j: s1 // sub),
            pl.BlockSpec((1, HEAD_DIM), lambda i, j: (0, 0)),
            pl.BlockSpec((1, HEAD_DIM), lambda i, j: (0, 0)),
        ],
        out_specs=[
            pl.BlockSpec((tm, tn), lambda i, j: (i, j)),
            pl.BlockSpec((tm, LANES), lambda i, j: (i, 0)),
        ],
        out_shape=[
            jax.ShapeDtypeStruct((m, (na + nb) * tn), BF16),
            jax.ShapeDtypeStruct((m, LANES), F32),
        ],
        compiler_params=_params("parallel", "arbitrary"),
        name="inproj",
    )(hn, wt, wt, wt, gq, gk)


def _gate_prep_kernel(ps_ref, alog_ref, dtb_ref, fb_ref, ltri_ref, sq_ref, sk_ref, cq_ref, ck_ref,
                      yp_ref, eq_ref, ek_ref, *, nblk):
    lane = lax.broadcasted_iota(jnp.int32, (CHUNK, LANES), 1)
    rowi = lax.broadcasted_iota(jnp.int32, (CHUNK, LANES), 0)

    def body(i, carry):
        rows = pl.ds(pl.multiple_of(i * CHUNK, CHUNK), CHUNK)
        x = ps_ref[rows, :]
        valid = (i * CHUNK + rowi) >= FRONT_PAD

        beta = jnp.where(valid, _sigmoid(x), 0.0)
        gstep = jnp.where(valid, -jnp.exp(alog_ref[...]) * _softplus(x + dtb_ref[...]), 0.0)
        gcum = _dot(ltri_ref[...], jnp.concatenate(_split3(gstep), axis=0))
        y = jnp.where(lane < 16 * (GQ_G + 1), pltpu.roll(gcum, LANES + 16 * GQ_G - SM_A, 1),
                      pltpu.roll(beta, 16 * GQ_BETA - SM_B, 1))
        yp_ref[rows, :] = jnp.concatenate(_split3(y)[:GQ_PIECES], axis=1)

        xf = x + fb_ref[...]
        ls = (jnp.minimum(xf, 0.0) - jnp.log1p(jnp.exp(-jnp.abs(xf)))) * LOG2E
        cum = _dot(ltri_ref[...], jnp.concatenate(_split3(ls), axis=0)) + carry
        cp = jnp.concatenate(_split3(cum), axis=1)
        eq_ref[rows, :] = (_dot(cp, sq_ref[...]) + cq_ref[...]).astype(BF16)
        ek_ref[rows, :] = (_dot(cp, sk_ref[...]) + ck_ref[...]).astype(BF16)
        return cum[CHUNK - 1:CHUNK, :]

    lax.fori_loop(0, nblk, body, jnp.zeros((1, LANES), F32))


def _gate_prep(ps, alog_row, dtb_row, fb_row, ltri3, sq, sk, cq, ck):
    b, lp, _ = ps.shape
    const = lambda bi: (0, 0)
    return pl.pallas_call(
        functools.partial(_gate_prep_kernel, nblk=lp // CHUNK),
        grid=(b,),
        in_specs=[
            pl.BlockSpec((None, lp, LANES), lambda bi: (bi, 0, 0)),
            pl.BlockSpec((1, LANES), const),
            pl.BlockSpec((1, LANES), const),
            pl.BlockSpec((1, LANES), const),
            pl.BlockSpec((CHUNK, 3 * CHUNK), const),
            pl.BlockSpec((3 * LANES, LANES), const),
            pl.BlockSpec((3 * LANES, LANES), const),
            pl.BlockSpec((1, LANES), const),
            pl.BlockSpec((1, LANES), const),
        ],
        out_specs=[
            pl.BlockSpec((None, lp, GQ_PIECES * LANES), lambda bi: (bi, 0, 0)),
            pl.BlockSpec((None, lp, LANES), lambda bi: (bi, 0, 0)),
            pl.BlockSpec((None, lp, LANES), lambda bi: (bi, 0, 0)),
        ],
        out_shape=[
            jax.ShapeDtypeStruct((b, lp, GQ_PIECES * LANES), BF16),
            jax.ShapeDtypeStruct((b, lp, LANES), BF16),
            jax.ShapeDtypeStruct((b, lp, LANES), BF16),
        ],
        compiler_params=_params("parallel"),
        name="gate_prep",
    )(ps, alog_row, dtb_row, fb_row, ltri3, sq, sk, cq, ck)


def _gdn_kernel(q_ref, k_ref, v_ref, z_ref, yp_ref, sel_ref, oneh_ref, wq_ref, wk_ref, wv_ref,
                gn_ref, shift_ref, o_ref, s_ref, prev_ref, *, hg):
    c = pl.program_id(2)

    @pl.when(c == 0)
    def _():
        s_ref[...] = jnp.zeros_like(s_ref)
        prev_ref[...] = jnp.zeros_like(prev_ref)

    row = lax.broadcasted_iota(jnp.int32, (CHUNK, CHUNK), 0)
    col = lax.broadcasted_iota(jnp.int32, (CHUNK, CHUNK), 1)
    tril = row >= col
    strict = row > col
    heads = range(hg)
    hsl = lambda a, u: a[:, u * HEAD_DIM:(u + 1) * HEAD_DIM]

    yp = yp_ref[...]
    gb = _dot(yp, sel_ref[...])
    g_rows = _dot_nt(oneh_ref[...], yp)
    gq = lambda u, q: gb[:, (u * N_GQ + q) * LANES:(u * N_GQ + q + 1) * LANES]
    g_col = [gq(u, GQ_G) for u in heads]
    beta = [gq(u, GQ_BETA) for u in heads]
    eg = [jnp.exp(g_col[u]) for u in heads]

    def conv_act(x_ref, w_ref, t):
        x16 = x_ref[...]
        x2 = jnp.concatenate([prev_ref[t], x16], axis=0)
        prev_ref[t] = x16
        sh = _dot(shift_ref[...], x2)
        w = w_ref[...]
        acc = x16.astype(F32) * w[CONV_K - 1:CONV_K, :]
        for s in range(1, CONV_K):
            acc = acc + sh[(s - 1) * CHUNK:s * CHUNK, :] * w[CONV_K - 1 - s:CONV_K - s, :]
        return _silu(acc)

    qa = conv_act(q_ref, wq_ref, 0)
    ka = conv_act(k_ref, wk_ref, 1)
    va = conv_act(v_ref, wv_ref, 2)

    def l2n(x):
        return x * lax.rsqrt(jnp.sum(x * x, axis=-1, keepdims=True) + NORM_EPS)

    qn = [l2n(hsl(qa, u)) * (HEAD_DIM ** -0.5) for u in heads]
    kn = [l2n(hsl(ka, u)) for u in heads]
    k16 = [kn[u].astype(BF16) for u in heads]

    state = [s_ref[u] for u in heads]
    kb = [kn[u] * beta[u] for u in heads]
    r1 = [_dot(jnp.concatenate([kb[u] * eg[u], qn[u] * eg[u]], axis=0).astype(BF16),
               state[u].astype(BF16)) for u in heads]

    kq = [_dot_nt(jnp.concatenate([kb[u], qn[u]], axis=0).astype(BF16), k16[u])
          for u in heads]
    ediff = [jnp.exp(jnp.minimum(g_col[u] - g_rows[u * ROW_REP:u * ROW_REP + 1, :], 0.0))
             for u in heads]
    a = [kq[u][:CHUNK] * jnp.where(strict, ediff[u], 0.0) for u in heads]
    qk_m = [(kq[u][CHUNK:] * jnp.where(tril, ediff[u], 0.0)).astype(BF16) for u in heads]

    a16 = [a[u].astype(BF16) for u in heads]
    yk = [_dot(a16[u], a16[u]) for u in heads]
    sk = [-a[u] for u in heads]
    nlev = int(np.log2(CHUNK)) - 1
    for lev in range(nlev):
        yk16 = [yk[u].astype(BF16) for u in heads]
        if lev < nlev - 1:
            zz = [_dot(jnp.concatenate([sk[u], yk[u]], axis=0).astype(BF16), yk16[u]) for u in heads]
            sk = [sk[u] + yk[u] + zz[u][:CHUNK] for u in heads]
            yk = [zz[u][CHUNK:] for u in heads]
        else:
            sk = [sk[u] + yk[u] + _dot(sk[u].astype(BF16), yk16[u]) for u in heads]

    rhs = [hsl(va, u) * beta[u] - r1[u][:CHUNK] for u in heads]
    v_new = [(rhs[u] + _dot(sk[u].astype(BF16), rhs[u].astype(BF16))).astype(BF16) for u in heads]
    o = [r1[u][CHUNK:] + _dot(qk_m[u], v_new[u]) for u in heads]
    for u in heads:
        k_dec = kn[u] * jnp.exp(g_col[u][CHUNK - 1:CHUNK, :] - g_col[u])
        s_ref[u] = state[u] * eg[u][CHUNK - 1:CHUNK, :] + _dot_tn(k_dec.astype(BF16), v_new[u])

    on = jnp.concatenate([_norm_rows(o[u], gn_ref[...]) for u in heads], axis=1)
    o_ref[...] = (on * _silu(z_ref[...].astype(F32))).astype(o_ref.dtype)


def _gdn(p1, yp, sel, oneh, conv_w, gnorm, shift3, *, seq, hg):
    b, lp, _ = p1.shape
    nc = lp // CHUNK
    gw = hg * HEAD_DIM
    ng = HEADS // hg
    col = lambda base: (lambda bi, gi, ci: (bi, ci, base * LANES // gw + gi))
    wcol = lambda base: (lambda bi, gi, ci: (0, base * LANES // gw + gi))
    const = lambda bi, gi, ci: (0, 0)
    return pl.pallas_call(
        functools.partial(_gdn_kernel, hg=hg),
        grid=(b, ng, nc),
        in_specs=[
            pl.BlockSpec((None, CHUNK, gw), col(COL_GDN_Q)),
            pl.BlockSpec((None, CHUNK, gw), col(COL_GDN_K)),
            pl.BlockSpec((None, CHUNK, gw), col(COL_GDN_V)),
            pl.BlockSpec((None, CHUNK, gw), col(COL_Z)),
            pl.BlockSpec((None, CHUNK, GQ_PIECES * LANES), lambda bi, gi, ci: (bi, ci, 0)),
            pl.BlockSpec((GQ_PIECES * LANES, hg * N_GQ * LANES), lambda bi, gi, ci: (0, gi)),
            pl.BlockSpec((hg * ROW_REP, GQ_PIECES * LANES), lambda bi, gi, ci: (gi, 0)),
            pl.BlockSpec((CONV_K, gw), wcol(COL_GDN_Q)),
            pl.BlockSpec((CONV_K, gw), wcol(COL_GDN_K)),
            pl.BlockSpec((CONV_K, gw), wcol(COL_GDN_V)),
            pl.BlockSpec((1, HEAD_DIM), const),
            pl.BlockSpec((3 * CHUNK, 2 * CHUNK), const),
        ],
        out_specs=pl.BlockSpec((None, CHUNK, gw),
                               lambda bi, gi, ci: (bi, jnp.maximum(ci - 1, 0), gi)),
        out_shape=jax.ShapeDtypeStruct((b, seq, HEADS * HEAD_DIM), BF16),
        scratch_shapes=[
            pltpu.VMEM((hg, HEAD_DIM, HEAD_DIM), F32),
            pltpu.VMEM((3, CHUNK, gw), BF16),
        ],
        compiler_params=_params("parallel", "parallel", "arbitrary"),
        name="gdn",
    )(p1, p1, p1, p1, yp, sel, oneh, conv_w, conv_w, conv_w, gnorm, shift3)


def _fox_kernel(*refs, tq, lp, n_cast):
    q_ref, k_ref, v_ref, eq_ref, ek_ref, bound_ref = refs[:6]
    o_ref = refs[6 + n_cast]
    kaug_ref, vaug_ref = refs[-2:]
    h = pl.program_id(1)
    i = pl.program_id(2)

    for w_ref, w16_ref in zip(refs[6:6 + n_cast], refs[7 + n_cast:7 + 2 * n_cast]):
        w16_ref[...] = w_ref[...].astype(w16_ref.dtype)

    @pl.when(i == 0)
    def _():
        def build(rows, n):
            kaug_ref[rows, 0:HEAD_DIM] = k_ref[rows, :]
            kaug_ref[rows, HEAD_DIM:2 * HEAD_DIM] = ek_ref[rows, :]
            vaug_ref[rows, 0:HEAD_DIM] = v_ref[rows, :]
            vaug_ref[rows, HEAD_DIM:2 * HEAD_DIM] = jnp.ones((n, HEAD_DIM), BF16)

        def body(r, c):
            build(pl.ds(pl.multiple_of(r * tq, 128), tq), tq)
            return c
        lax.fori_loop(0, lp // tq, body, 0)
        if lp % tq:
            build(pl.ds(lp - lp % tq, lp % tq), lp % tq)

    q0 = pl.multiple_of(CHUNK + i * tq, 128)
    rows_q = pl.ds(q0, tq)
    lane = lax.broadcasted_iota(jnp.int32, (tq, LANES), 1)
    mine = lax.shift_right_logical(lane, 3) == h
    eqm = jnp.where(mine, eq_ref[rows_q, :], jnp.zeros((), BF16))
    q16 = q_ref[rows_q, :]

    bound = bound_ref[0, 0]
    safe = bound <= FOX_SAFE_BOUND

    rr = lax.broadcasted_iota(jnp.int32, (tq, tq), 0)
    cc = lax.broadcasted_iota(jnp.int32, (tq, tq), 1)
    causal = rr >= cc

    def blocks(j):
        return pl.ds(pl.multiple_of(CHUNK + j * tq, 128), tq)

    @pl.when(safe)
    def _():
        eqb = jnp.where(lane == FOX_EXTRA * h + FOX_M_LANE, jnp.full((tq, LANES), -bound, BF16), eqm)
        qaug = jnp.concatenate([q16, eqb], axis=1)
        half = tq // 2

        def probs(qa, krows, mask=None):
            s = _dot_nt(qa, kaug_ref[krows, :])
            if mask is not None:
                s = jnp.where(mask, s, NEG_BIG)
            return jnp.exp2(s).astype(BF16)

        def finish(acc, rows):
            o_ref[rows, :] = (acc[:, :HEAD_DIM] / acc[:, HEAD_DIM:]).astype(o_ref.dtype)

        for ii in range(pl.cdiv(lp - CHUNK, tq)):
            @pl.when(i == ii)
            def _(ii=ii):
                acc = _dot(probs(qaug, slice(0, CHUNK), lane >= FRONT_PAD), vaug_ref[0:CHUNK, :])
                for j in range(ii):
                    kr = slice(CHUNK + j * tq, CHUNK + (j + 1) * tq)
                    acc = acc + _dot(probs(qaug, kr), vaug_ref[kr, :])
                k_lo = slice(CHUNK + ii * tq, CHUNK + ii * tq + half)
                k_all = slice(CHUNK + ii * tq, CHUNK + (ii + 1) * tq)
                p_lo = probs(qaug[:half], k_lo, causal[:half, :half])
                p_hi = probs(qaug[half:], k_all, causal[half:, :])
                finish(acc[:half] + _dot(p_lo, vaug_ref[k_lo, :]), slice(0, half))
                finish(acc[half:] + _dot(p_hi, vaug_ref[k_all, :]), slice(half, tq))

    @pl.when(jnp.logical_not(safe))
    def _():
        qaug = jnp.concatenate([q16, eqm], axis=1)
        s = _dot_nt(qaug, kaug_ref[0:CHUNK, :])
        s = jnp.where(lane >= FRONT_PAD, s, NEG_BIG)
        m = jnp.max(s, axis=-1, keepdims=True)
        acc = _dot(jnp.exp2(s - m).astype(BF16), vaug_ref[0:CHUNK, :])

        def update(s, vrows, m, acc):
            m_new = jnp.maximum(m, jnp.max(s, axis=-1, keepdims=True))
            p = jnp.exp2(s - m_new)
            return m_new, jnp.exp2(m - m_new) * acc + _dot(p.astype(BF16), vrows)

        def body(j, carry):
            s = _dot_nt(qaug, kaug_ref[blocks(j), :])
            return update(s, vaug_ref[blocks(j), :], *carry)
        m, acc = lax.fori_loop(0, i, body, (m, acc))

        s = _dot_nt(qaug, kaug_ref[rows_q, :])
        m, acc = update(jnp.where(causal, s, NEG_BIG), vaug_ref[rows_q, :], m, acc)
        o_ref[...] = (acc[:, :HEAD_DIM] / acc[:, HEAD_DIM:]).astype(o_ref.dtype)


def _fox(p1, eq, ek, bound, weights, wt, wt_rows, *, seq, tq):
    b, lp, _ = p1.shape
    nq = seq // tq
    nsteps = b * HEADS * nq
    colmap = lambda base: (lambda bi, hi, qi: (bi, 0, base + hi))
    step = lambda bi, hi, qi: (bi * HEADS + hi) * nq + qi

    cast_in, cast_out, cast_shapes = [], [], []
    for w in weights:
        _, r, c = w.shape
        rp = r // nsteps
        assert r % nsteps == 0 and rp % 16 == 0
        cast_in.append(pl.BlockSpec((None, rp, c), lambda bi, hi, qi: (0, step(bi, hi, qi), 0)))
        cast_out.append(pl.BlockSpec((rp, c), lambda bi, hi, qi: (step(bi, hi, qi), 0)))
        cast_shapes.append(jax.ShapeDtypeStruct((r, c), BF16))
    start, n = wt_rows
    rp = n // nsteps
    assert n % nsteps == 0 and rp % 16 == 0 and start % F32_SUBLANES == 0
    cast_in.append(_element_rows(rp, wt.shape[2], lambda bi, hi, qi: (
        start // F32_SUBLANES + step(bi, hi, qi) * (rp // F32_SUBLANES))))
    cast_out.append(pl.BlockSpec((rp, wt.shape[2]), lambda bi, hi, qi: (step(bi, hi, qi), 0)))
    cast_shapes.append(jax.ShapeDtypeStruct((n, wt.shape[2]), BF16))

    outs = pl.pallas_call(
        functools.partial(_fox_kernel, tq=tq, lp=lp, n_cast=len(cast_in)),
        grid=(b, HEADS, nq),
        in_specs=[
            pl.BlockSpec((None, lp, HEAD_DIM), colmap(COL_FOX_Q)),
            pl.BlockSpec((None, lp, HEAD_DIM), colmap(COL_FOX_K)),
            pl.BlockSpec((None, lp, HEAD_DIM), colmap(COL_FOX_V)),
            pl.BlockSpec((None, lp, LANES), lambda bi, hi, qi: (bi, 0, 0)),
            pl.BlockSpec((None, lp, LANES), lambda bi, hi, qi: (bi, 0, 0)),
            pl.BlockSpec(memory_space=pltpu.SMEM),
        ] + cast_in,
        out_specs=[pl.BlockSpec((None, tq, HEAD_DIM), lambda bi, hi, qi: (bi, qi, hi))] + cast_out,
        out_shape=[jax.ShapeDtypeStruct((b, seq, HEADS * HEAD_DIM), BF16)] + cast_shapes,
        scratch_shapes=[
            pltpu.VMEM((lp, 2 * HEAD_DIM), BF16),
            pltpu.VMEM((lp, 2 * HEAD_DIM), BF16),
        ],
        compiler_params=_params("parallel", "parallel", "arbitrary"),
        name="fox",
    )(p1, p1, p1, eq, ek, bound, *weights, wt)
    return outs[0], outs[1:]


def _merge_kernel(x_ref, g_ref, ya_ref, yb_ref, wga_ref, wgb_ref, wa_ref, wb_ref, wo_ref,
                  o_ref, u_ref):
    j = pl.program_id(1)

    @pl.when(j == 0)
    def _():
        x = x_ref[...]
        u_ref[...] = _norm_rows(x, g_ref[...]).astype(BF16)
        o_ref[...] = x

    u = u_ref[...]
    ga = _dot_nt(u, wga_ref[...])
    gb = _dot_nt(u, wgb_ref[...])
    pa = _dot(ya_ref[...], wa_ref[...])
    pb = _dot(yb_ref[...], wb_ref[...])
    mix = _sigmoid(ga) * pa + _sigmoid(gb) * pb
    o_ref[...] += _dot(mix.astype(BF16), wo_ref[...])


def _merge(x2d, g, ya, yb, wg_t, wa, wb, wo, *, tm, tj):
    m, d = x2d.shape
    rowblk = lambda i, j: (i, 0)
    colblk = lambda i, j: (0, j)
    return pl.pallas_call(
        _merge_kernel,
        grid=(m // tm, d // tj),
        in_specs=[
            pl.BlockSpec((tm, d), rowblk),
            pl.BlockSpec((1, d), lambda i, j: (0, 0)),
            pl.BlockSpec((tm, d), rowblk),
            pl.BlockSpec((tm, d), rowblk),
            pl.BlockSpec((tj, d), lambda i, j: (j, 0)),
            pl.BlockSpec((tj, d), lambda i, j: (j + d // tj, 0)),
            pl.BlockSpec((d, tj), colblk),
            pl.BlockSpec((d, tj), colblk),
            pl.BlockSpec((tj, d), lambda i, j: (j, 0)),
        ],
        out_specs=pl.BlockSpec((tm, d), rowblk),
        out_shape=jax.ShapeDtypeStruct((m, d), F32),
        scratch_shapes=[pltpu.VMEM((tm, d), BF16)],
        compiler_params=_params("parallel", "arbitrary"),
        name="merge",
    )(x2d, g, ya, yb, wg_t, wg_t, wa, wb, wo)


def _mlp_kernel(h_ref, g_ref, wu_ref, wd_ref, gf_ref, o_ref, u_ref):
    f = pl.program_id(1)

    @pl.when(f == 0)
    def _():
        x = h_ref[...]
        u_ref[...] = _norm_rows(x, g_ref[...]).astype(BF16)
        o_ref[...] = x

    a = jnp.maximum(_dot(u_ref[...], wu_ref[...]), 0.0)
    o_ref[...] += _dot((a * a).astype(BF16), wd_ref[...])

    @pl.when(f == pl.num_programs(1) - 1)
    def _():
        o_ref[...] = _norm_rows(o_ref[...], gf_ref[...])


def _mlp(h2d, g, wu, wd, gf, *, tm, tf):
    m, d = h2d.shape
    dff = wu.shape[1]
    rowblk = lambda i, f: (i, 0)
    return pl.pallas_call(
        _mlp_kernel,
        grid=(m // tm, dff // tf),
        in_specs=[
            pl.BlockSpec((tm, d), rowblk),
            pl.BlockSpec((1, d), lambda i, f: (0, 0)),
            pl.BlockSpec((d, tf), lambda i, f: (0, f)),
            pl.BlockSpec((tf, d), lambda i, f: (f, 0)),
            pl.BlockSpec((1, d), lambda i, f: (0, 0)),
        ],
        out_specs=pl.BlockSpec((tm, d), rowblk),
        out_shape=jax.ShapeDtypeStruct((m, d), F32),
        scratch_shapes=[pltpu.VMEM((tm, d), BF16)],
        compiler_params=_params("parallel", "arbitrary"),
        name="mlp",
    )(h2d, g, wu, wd, gf)


def _constants():
    r = np.arange(128)
    ltri = (r[:, None] >= r[None, :]).astype(np.float32)
    ltri3 = np.concatenate([ltri, ltri, ltri], axis=1)

    sq = np.zeros((3 * LANES, LANES), np.float32)
    sk = np.zeros((3 * LANES, LANES), np.float32)
    cq = np.zeros((1, LANES), np.float32)
    ck = np.zeros((1, LANES), np.float32)
    for h in range(HEADS):
        for p in range(3):
            sq[p * LANES + SM_F + h, FOX_EXTRA * h + p] = 1.0
            sk[p * LANES + SM_F + h, FOX_EXTRA * h + 3 + p] = -1.0
            cq[0, FOX_EXTRA * h + 3 + p] = 1.0
            ck[0, FOX_EXTRA * h + p] = 1.0
        ck[0, FOX_EXTRA * h + FOX_M_LANE] = 1.0

    shift3 = np.zeros((3 * CHUNK, 2 * CHUNK), np.float32)
    for s in range(1, CONV_K):
        shift3[(s - 1) * CHUNK + r, CHUNK + r - s] = 1.0

    src_lane = np.zeros((HEADS * N_GQ * LANES,), np.int64)
    for h in range(HEADS):
        for q in range(N_GQ):
            src_lane[(h * N_GQ + q) * LANES:(h * N_GQ + q + 1) * LANES] = 16 * q + h
    piece_lane = np.arange(GQ_PIECES * LANES) % LANES
    sel = (piece_lane[:, None] == src_lane[None, :]).astype(np.float32)
    oneh = (np.repeat(16 * GQ_G + np.arange(HEADS), ROW_REP)[:, None]
            == piece_lane[None, :]).astype(np.float32)

    as_bf = lambda a: jnp.asarray(a, BF16)
    return dict(ltri3=as_bf(ltri3), sq=as_bf(sq), sk=as_bf(sk), cq=jnp.asarray(cq),
                ck=jnp.asarray(ck), shift3=as_bf(shift3), sel=as_bf(sel), oneh=as_bf(oneh))


def _score_bound(gq, gk):
    gmax = jnp.max(jnp.abs(gq.astype(F32))) * jnp.max(jnp.abs(gk.astype(F32)))
    return (gmax * (HEAD_DIM ** 0.5 * LOG2E * 1.02)).astype(BF16).astype(F32).reshape(1, 1)


def _row128(vec, offset):
    return jnp.zeros((1, LANES), F32).at[0, offset:offset + HEADS].set(vec.astype(F32))


def _pick_rows_tile(total, target):
    best = 128
    for t in range(128, target + 1, 128):
        if total % t == 0:
            best = t
    return best


def _layer(x, meta_tokens, mix_norm_g, w_in, conv_w, a_log, dt_bias, gdn_norm_g, w_o_gdn,
           fox_q_norm_g, fox_k_norm_g, fox_f_bias, w_o_fox, w_out, mlp_norm_g, w_up, w_down,
           final_norm_g):
    b, seq, d = x.shape
    lp = CHUNK + seq
    qk = HEADS * HEAD_DIM

    o_z = 3 * qk
    o_b = o_z + qk
    o_a = o_b + HEADS
    o_fq = o_a + HEADS
    o_f = o_fq + 3 * qk
    o_ga = o_f + HEADS
    o_gb = o_ga + d

    wt = jnp.swapaxes(w_in, 1, 2)
    row = lambda v: v.reshape(1, -1).astype(F32)
    cst = _constants()

    tm_in = _pick_rows_tile(lp, 1408)
    hn = _prenorm(x, meta_tokens.astype(x.dtype), row(mix_norm_g), tile=tm_in)
    p1, ps = _inproj(hn.reshape(b * lp, d), wt, o_b, o_fq, o_f - o_fq,
                     ((o_b, o_fq - o_b), (o_f, o_ga - o_f)),
                     row(fox_q_norm_g), row(fox_k_norm_g), HEAD_DIM ** -0.5 * LOG2E,
                     tm=tm_in, tn=1024)
    p1 = p1.reshape(b, lp, P1_COLS)
    ps = ps.reshape(b, lp, LANES)

    yp, eq, ek = _gate_prep(ps, _row128(a_log, SM_A), _row128(dt_bias, SM_A),
                            _row128(fox_f_bias, SM_F), cst["ltri3"], cst["sq"], cst["sk"],
                            cst["cq"], cst["ck"])
    ya = _gdn(p1, yp, cst["sel"], cst["oneh"], conv_w.astype(F32), row(gdn_norm_g),
              cst["shift3"], seq=seq, hg=16)
    yb, (wog, wof, wout, wup, wdown, wg_t) = _fox(
        p1, eq, ek, _score_bound(fox_q_norm_g, fox_k_norm_g),
        [w_o_gdn, w_o_fox, w_out, w_up, w_down], wt, (o_ga, 2 * d), seq=seq, tq=1024)

    x2d = x.reshape(b * seq, d)
    h1 = _merge(x2d, row(mix_norm_g), ya.reshape(b * seq, qk), yb.reshape(b * seq, qk),
                wg_t, wog, wof, wout, tm=512, tj=512)
    out = _mlp(h1, row(mlp_norm_g), wup, wdown, row(final_norm_g), tm=512, tf=1024)
    return out.reshape(b, seq, d)


def kernel(x, meta_tokens, mix_norm_g, w_in, conv_w, a_log, dt_bias, gdn_norm_g, w_o_gdn,
           fox_q_norm_g, fox_k_norm_g, fox_f_bias, w_o_fox, w_out, mlp_norm_g, w_up, w_down,
           final_norm_g):
    assert w_in.shape[0] == 1, "single-layer block"
    return _layer(x, meta_tokens, mix_norm_g[0], w_in, conv_w[0], a_log[0], dt_bias[0],
                  gdn_norm_g[0], w_o_gdn, fox_q_norm_g[0], fox_k_norm_g[0], fox_f_bias[0],
                  w_o_fox, w_out, mlp_norm_g[0], w_up, w_down, final_norm_g)
```

```python
import functools
import math

import numpy as np
import jax
import jax.numpy as jnp
from jax import lax
from jax.experimental import pallas as pl
from jax.experimental.pallas import tpu as pltpu

F32 = jnp.float32
BF16 = jnp.bfloat16

NORM_EPS = 1e-6
N_META = 16
HEADS = 16
HEAD_DIM = 128
CONV_K = 4
LANES = 128
F32_SUBLANES = 8
CHUNK = 128
FRONT_PAD = CHUNK - N_META
NEG_BIG = -1e30
LOG2E = math.log2(math.e)
VMEM_LIMIT = 56 * 1024 * 1024

COL_GDN_Q, COL_GDN_K, COL_GDN_V, COL_Z = 0, 16, 32, 48
COL_FOX_Q, COL_FOX_K, COL_FOX_V = 64, 80, 96
P1_COLS = 112 * LANES
SM_B, SM_A, SM_F = 0, 16, 32
GQ_G, GQ_BETA = range(2)
N_GQ = 2
GQ_PIECES = 2
ROW_REP = 8
FOX_EXTRA = 8
FOX_M_LANE = 6
FOX_SAFE_BOUND = 40.0
FOX_KEY_CHUNK = 1024
FOX_DIAG_PIECE = 512


def _params(*sem):
    return pltpu.CompilerParams(dimension_semantics=sem, vmem_limit_bytes=VMEM_LIMIT)


def _norm_rows(x, g):
    ms = jnp.mean(x * x, axis=-1, keepdims=True)
    return x * lax.rsqrt(ms + NORM_EPS) * g


def _split3(x):
    hi = x.astype(BF16)
    r = x - hi.astype(F32)
    mid = r.astype(BF16)
    lo = (r - mid.astype(F32)).astype(BF16)
    return hi, mid, lo


def _dot(a, b):
    return jnp.dot(a, b, preferred_element_type=F32)


def _dot_nt(a, b):
    return lax.dot_general(a, b, (((1,), (1,)), ((), ())), preferred_element_type=F32)


def _dot_tn(a, b):
    return lax.dot_general(a, b, (((0,), (0,)), ((), ())), preferred_element_type=F32)


def _sigmoid(x):
    return 1.0 / (1.0 + jnp.exp(-x))


def _silu(x):
    return x * _sigmoid(x)


def _softplus(x):
    return jnp.maximum(x, 0.0) + jnp.log1p(jnp.exp(-jnp.abs(x)))


def _prenorm_kernel(x_ref, meta_ref, g_ref, o_ref, *, tile):
    t = pl.program_id(1)

    def norm_chunks(first_out, n, shift):
        def body(c, carry):
            dst = pl.ds(pl.multiple_of(first_out + c * CHUNK, CHUNK), CHUNK)
            src = pl.ds(pl.multiple_of(first_out - shift + c * CHUNK, CHUNK), CHUNK)
            o_ref[dst, :] = _norm_rows(x_ref[src, :], g_ref[...]).astype(o_ref.dtype)
            return carry
        lax.fori_loop(0, n, body, 0)

    @pl.when(t == 0)
    def _():
        o_ref[0:FRONT_PAD, :] = jnp.zeros((FRONT_PAD, o_ref.shape[1]), o_ref.dtype)
        o_ref[FRONT_PAD:CHUNK, :] = _norm_rows(meta_ref[...], g_ref[...]).astype(o_ref.dtype)
        norm_chunks(CHUNK, tile // CHUNK - 1, CHUNK)

    @pl.when(t > 0)
    def _():
        norm_chunks(0, tile // CHUNK, 0)


def _prenorm(x, meta_tokens, g, *, tile):
    b, seq, d = x.shape
    lp = CHUNK + seq
    sub = F32_SUBLANES
    assert lp % tile == 0 and tile % CHUNK == 0 and CHUNK % sub == 0
    x_rows = lambda bi, t: sub * jnp.maximum(t * (tile // sub) - CHUNK // sub, 0)
    return pl.pallas_call(
        functools.partial(_prenorm_kernel, tile=tile),
        grid=(b, lp // tile),
        in_specs=[
            pl.BlockSpec((None, pl.Element(tile), pl.Element(d)),
                         lambda bi, t: (bi, x_rows(bi, t), 0)),
            pl.BlockSpec((N_META, d), lambda bi, t: (0, 0)),
            pl.BlockSpec((1, d), lambda bi, t: (0, 0)),
        ],
        out_specs=pl.BlockSpec((None, tile, d), lambda bi, t: (bi, t, 0)),
        out_shape=jax.ShapeDtypeStruct((b, lp, d), BF16),
        compiler_params=_params("parallel", "arbitrary"),
        name="prenorm",
    )(x, meta_tokens, g)


def _inproj_kernel(x_ref, w_ref, wsa_ref, wsb_ref, gq_ref, gk_ref, o_ref, os_ref, *,
                   q_tiles, k_tiles, q_scale):
    j = pl.program_id(1)

    @pl.when(j == 0)
    def _():
        pad = LANES - wsa_ref.shape[0] - wsb_ref.shape[0]
        ws = jnp.concatenate([wsa_ref[...], wsb_ref[...], jnp.zeros((pad, wsa_ref.shape[1]), F32)],
                             axis=0)
        os_ref[...] = _dot_nt(x_ref[...], ws.astype(BF16))

    def project():
        return _dot_nt(x_ref[...], w_ref[...].astype(BF16))

    def project_head_norm(g_ref, scale):
        acc = project()
        for h in range(o_ref.shape[1] // HEAD_DIM):
            cols = slice(h * HEAD_DIM, (h + 1) * HEAD_DIM)
            o_ref[:, cols] = (_norm_rows(acc[:, cols], g_ref[...]) * scale).astype(o_ref.dtype)

    is_q = jnp.logical_and(j >= q_tiles[0], j < q_tiles[1])
    is_k = jnp.logical_and(j >= k_tiles[0], j < k_tiles[1])

    @pl.when(is_q)
    def _():
        project_head_norm(gq_ref, q_scale)

    @pl.when(is_k)
    def _():
        project_head_norm(gk_ref, 1.0)

    @pl.when(jnp.logical_not(jnp.logical_or(is_q, is_k)))
    def _():
        o_ref[...] = project().astype(o_ref.dtype)


def _element_rows(tile, d, start_tiles):
    return pl.BlockSpec((None, pl.Element(tile), pl.Element(d)),
                        lambda *ids: (0, F32_SUBLANES * start_tiles(*ids), 0))


def _inproj(hn, wt, n_a, b_start, n_b, small, gq, gk, q_scale, *, tm, tn):
    m, d = hn.shape
    na, nb = n_a // tn, n_b // tn
    assert nb % 3 == 0
    q_tiles = (na, na + nb // 3)
    k_tiles = (na + nb // 3, na + 2 * nb // 3)
    (s0, c0), (s1, c1) = small
    sub = F32_SUBLANES
    assert all(v % sub == 0 for v in (tn, b_start - n_a, s0, s1))

    def w_rows(i, j):
        return j * (tn // sub) + jnp.where(j >= na, (b_start - n_a) // sub, 0)

    return pl.pallas_call(
        functools.partial(_inproj_kernel, q_tiles=q_tiles, k_tiles=k_tiles, q_scale=q_scale),
        grid=(m // tm, na + nb),
        in_specs=[
            pl.BlockSpec((tm, d), lambda i, j: (i, 0)),
            _element_rows(tn, d, w_rows),
            _element_rows(c0, d, lambda i, j: s0 // sub),
            _element_rows(c1, d, lambda i, j: s1 // sub),
            pl.BlockSpec((1, HEAD_DIM), lambda i, j: (0, 0)),
            pl.BlockSpec((1, HEAD_DIM), lambda i, j: (0, 0)),
        ],
        out_specs=[
            pl.BlockSpec((tm, tn), lambda i, j: (i, j)),
            pl.BlockSpec((tm, LANES), lambda i, j: (i, 0)),
        ],
        out_shape=[
            jax.ShapeDtypeStruct((m, (na + nb) * tn), BF16),
            jax.ShapeDtypeStruct((m, LANES), F32),
        ],
        compiler_params=_params("parallel", "arbitrary"),
        name="inproj",
    )(hn, wt, wt, wt, gq, gk)


def _gate_prep_kernel(ps_ref, alog_ref, dtb_ref, fb_ref, ltri_ref, sq_ref, sk_ref, cq_ref, ck_ref,
                      yp_ref, eq_ref, ek_ref, *, nblk):
    lane = lax.broadcasted_iota(jnp.int32, (CHUNK, LANES), 1)
    rowi = lax.broadcasted_iota(jnp.int32, (CHUNK, LANES), 0)

    def body(i, carry):
        rows = pl.ds(pl.multiple_of(i * CHUNK, CHUNK), CHUNK)
        x = ps_ref[rows, :]
        valid = (i * CHUNK + rowi) >= FRONT_PAD

        beta = jnp.where(valid, _sigmoid(x), 0.0)
        gstep = jnp.where(valid, -jnp.exp(alog_ref[...]) * _softplus(x + dtb_ref[...]), 0.0)
        gcum = _dot(ltri_ref[...], jnp.concatenate(_split3(gstep), axis=0))
        y = jnp.where(lane < 16 * (GQ_G + 1), pltpu.roll(gcum, LANES + 16 * GQ_G - SM_A, 1),
                      pltpu.roll(beta, 16 * GQ_BETA - SM_B, 1))
        yp_ref[rows, :] = jnp.concatenate(_split3(y)[:GQ_PIECES], axis=1)

        xf = x + fb_ref[...]
        ls = (jnp.minimum(xf, 0.0) - jnp.log1p(jnp.exp(-jnp.abs(xf)))) * LOG2E
        cum = _dot(ltri_ref[...], jnp.concatenate(_split3(ls), axis=0)) + carry
        cp = jnp.concatenate(_split3(cum), axis=1)
        eq_ref[rows, :] = (_dot(cp, sq_ref[...]) + cq_ref[...]).astype(BF16)
        ek_ref[rows, :] = (_dot(cp, sk_ref[...]) + ck_ref[...]).astype(BF16)
        return cum[CHUNK - 1:CHUNK, :]

    lax.fori_loop(0, nblk, body, jnp.zeros((1, LANES), F32))


def _gate_prep(ps, alog_row, dtb_row, fb_row, ltri3, sq, sk, cq, ck):
    b, lp, _ = ps.shape
    const = lambda bi: (0, 0)
    return pl.pallas_call(
        functools.partial(_gate_prep_kernel, nblk=lp // CHUNK),
        grid=(b,),
        in_specs=[
            pl.BlockSpec((None, lp, LANES), lambda bi: (bi, 0, 0)),
            pl.BlockSpec((1, LANES), const),
            pl.BlockSpec((1, LANES), const),
            pl.BlockSpec((1, LANES), const),
            pl.BlockSpec((CHUNK, 3 * CHUNK), const),
            pl.BlockSpec((3 * LANES, LANES), const),
            pl.BlockSpec((3 * LANES, LANES), const),
            pl.BlockSpec((1, LANES), const),
            pl.BlockSpec((1, LANES), const),
        ],
        out_specs=[
            pl.BlockSpec((None, lp, GQ_PIECES * LANES), lambda bi: (bi, 0, 0)),
            pl.BlockSpec((None, lp, LANES), lambda bi: (bi, 0, 0)),
            pl.BlockSpec((None, lp, LANES), lambda bi: (bi, 0, 0)),
        ],
        out_shape=[
            jax.ShapeDtypeStruct((b, lp, GQ_PIECES * LANES), BF16),
            jax.ShapeDtypeStruct((b, lp, LANES), BF16),
            jax.ShapeDtypeStruct((b, lp, LANES), BF16),
        ],
        compiler_params=_params("parallel"),
        name="gate_prep",
    )(ps, alog_row, dtb_row, fb_row, ltri3, sq, sk, cq, ck)


def _gdn_kernel(q_ref, k_ref, v_ref, z_ref, yp_ref, sel_ref, oneh_ref, wq_ref, wk_ref, wv_ref,
                gn_ref, shift_ref, o_ref, s_ref, prev_ref, *, hg):
    c = pl.program_id(2)

    @pl.when(c == 0)
    def _():
        s_ref[...] = jnp.zeros_like(s_ref)
        prev_ref[...] = jnp.zeros_like(prev_ref)

    row = lax.broadcasted_iota(jnp.int32, (CHUNK, CHUNK), 0)
    col = lax.broadcasted_iota(jnp.int32, (CHUNK, CHUNK), 1)
    tril = row >= col
    strict = row > col
    heads = range(hg)
    hsl = lambda a, u: a[:, u * HEAD_DIM:(u + 1) * HEAD_DIM]

    yp = yp_ref[...]
    gb = _dot(yp, sel_ref[...])
    g_rows = _dot_nt(oneh_ref[...], yp)
    gq = lambda u, q: gb[:, (u * N_GQ + q) * LANES:(u * N_GQ + q + 1) * LANES]
    g_col = [gq(u, GQ_G) for u in heads]
    beta = [gq(u, GQ_BETA) for u in heads]
    eg = [jnp.exp(g_col[u]) for u in heads]

    def conv_act(x_ref, w_ref, t):
        x16 = x_ref[...]
        x2 = jnp.concatenate([prev_ref[t], x16], axis=0)
        prev_ref[t] = x16
        sh = _dot(shift_ref[...], x2)
        w = w_ref[...]
        acc = x16.astype(F32) * w[CONV_K - 1:CONV_K, :]
        for s in range(1, CONV_K):
            acc = acc + sh[(s - 1) * CHUNK:s * CHUNK, :] * w[CONV_K - 1 - s:CONV_K - s, :]
        return _silu(acc)

    qa = conv_act(q_ref, wq_ref, 0)
    ka = conv_act(k_ref, wk_ref, 1)
    va = conv_act(v_ref, wv_ref, 2)

    def l2n(x):
        return x * lax.rsqrt(jnp.sum(x * x, axis=-1, keepdims=True) + NORM_EPS)

    qn = [l2n(hsl(qa, u)) * (HEAD_DIM ** -0.5) for u in heads]
    kn = [l2n(hsl(ka, u)) for u in heads]
    k16 = [kn[u].astype(BF16) for u in heads]

    state = [s_ref[u] for u in heads]
    kb = [kn[u] * beta[u] for u in heads]
    r1 = [_dot(jnp.concatenate([kb[u] * eg[u], qn[u] * eg[u]], axis=0).astype(BF16),
               state[u].astype(BF16)) for u in heads]

    kq = [_dot_nt(jnp.concatenate([kb[u], qn[u]], axis=0).astype(BF16), k16[u])
          for u in heads]
    ediff = [jnp.exp(jnp.minimum(g_col[u] - g_rows[u * ROW_REP:u * ROW_REP + 1, :], 0.0))
             for u in heads]
    a = [kq[u][:CHUNK] * jnp.where(strict, ediff[u], 0.0) for u in heads]
    qk_m = [(kq[u][CHUNK:] * jnp.where(tril, ediff[u], 0.0)).astype(BF16) for u in heads]

    a16 = [a[u].astype(BF16) for u in heads]
    yk = [_dot(a16[u], a16[u]) for u in heads]
    sk = [-a[u] for u in heads]
    nlev = int(np.log2(CHUNK)) - 1
    for lev in range(nlev):
        yk16 = [yk[u].astype(BF16) for u in heads]
        if lev < nlev - 1:
            zz = [_dot(jnp.concatenate([sk[u], yk[u]], axis=0).astype(BF16), yk16[u]) for u in heads]
            sk = [sk[u] + yk[u] + zz[u][:CHUNK] for u in heads]
            yk = [zz[u][CHUNK:] for u in heads]
        else:
            sk = [sk[u] + yk[u] + _dot(sk[u].astype(BF16), yk16[u]) for u in heads]

    rhs = [hsl(va, u) * beta[u] - r1[u][:CHUNK] for u in heads]
    v_new = [(rhs[u] + _dot(sk[u].astype(BF16), rhs[u].astype(BF16))).astype(BF16) for u in heads]
    o = [r1[u][CHUNK:] + _dot(qk_m[u], v_new[u]) for u in heads]
    for u in heads:
        k_dec = kn[u] * jnp.exp(g_col[u][CHUNK - 1:CHUNK, :] - g_col[u])
        s_ref[u] = state[u] * eg[u][CHUNK - 1:CHUNK, :] + _dot_tn(k_dec.astype(BF16), v_new[u])

    on = jnp.concatenate([_norm_rows(o[u], gn_ref[...]) for u in heads], axis=1)
    o_ref[...] = (on * _silu(z_ref[...].astype(F32))).astype(o_ref.dtype)


def _gdn(p1, yp, sel, oneh, conv_w, gnorm, shift3, *, seq, hg):
    b, lp, _ = p1.shape
    nc = lp // CHUNK
    gw = hg * HEAD_DIM
    ng = HEADS // hg
    col = lambda base: (lambda bi, gi, ci: (bi, ci, base * LANES // gw + gi))
    wcol = lambda base: (lambda bi, gi, ci: (0, base * LANES // gw + gi))
    const = lambda bi, gi, ci: (0, 0)
    return pl.pallas_call(
        functools.partial(_gdn_kernel, hg=hg),
        grid=(b, ng, nc),
        in_specs=[
            pl.BlockSpec((None, CHUNK, gw), col(COL_GDN_Q)),
            pl.BlockSpec((None, CHUNK, gw), col(COL_GDN_K)),
            pl.BlockSpec((None, CHUNK, gw), col(COL_GDN_V)),
            pl.BlockSpec((None, CHUNK, gw), col(COL_Z)),
            pl.BlockSpec((None, CHUNK, GQ_PIECES * LANES), lambda bi, gi, ci: (bi, ci, 0)),
            pl.BlockSpec((GQ_PIECES * LANES, hg * N_GQ * LANES), lambda bi, gi, ci: (0, gi)),
            pl.BlockSpec((hg * ROW_REP, GQ_PIECES * LANES), lambda bi, gi, ci: (gi, 0)),
            pl.BlockSpec((CONV_K, gw), wcol(COL_GDN_Q)),
            pl.BlockSpec((CONV_K, gw), wcol(COL_GDN_K)),
            pl.BlockSpec((CONV_K, gw), wcol(COL_GDN_V)),
            pl.BlockSpec((1, HEAD_DIM), const),
            pl.BlockSpec((3 * CHUNK, 2 * CHUNK), const),
        ],
        out_specs=pl.BlockSpec((None, CHUNK, gw),
                               lambda bi, gi, ci: (bi, jnp.maximum(ci - 1, 0), gi)),
        out_shape=jax.ShapeDtypeStruct((b, seq, HEADS * HEAD_DIM), BF16),
        scratch_shapes=[
            pltpu.VMEM((hg, HEAD_DIM, HEAD_DIM), F32),
            pltpu.VMEM((3, CHUNK, gw), BF16),
        ],
        compiler_params=_params("parallel", "parallel", "arbitrary"),
        name="gdn",
    )(p1, p1, p1, p1, yp, sel, oneh, conv_w, conv_w, conv_w, gnorm, shift3)


def _fox_kernel(*refs, tq, lp, n_cast):
    q_ref, k_ref, v_ref, eq_ref, ek_ref, bound_ref = refs[:6]
    o_ref = refs[6 + n_cast]
    kaug_ref, vaug_ref = refs[-2:]
    h = pl.program_id(1)
    i = pl.program_id(2)

    for w_ref, w16_ref in zip(refs[6:6 + n_cast], refs[7 + n_cast:7 + 2 * n_cast]):
        w16_ref[...] = w_ref[...].astype(w16_ref.dtype)

    @pl.when(i == 0)
    def _():
        def build(rows, n):
            kaug_ref[rows, 0:HEAD_DIM] = k_ref[rows, :]
            kaug_ref[rows, HEAD_DIM:2 * HEAD_DIM] = ek_ref[rows, :]
            vaug_ref[rows, 0:HEAD_DIM] = v_ref[rows, :]
            vaug_ref[rows, HEAD_DIM:2 * HEAD_DIM] = jnp.ones((n, HEAD_DIM), BF16)

        def body(r, c):
            build(pl.ds(pl.multiple_of(r * tq, 128), tq), tq)
            return c
        lax.fori_loop(0, lp // tq, body, 0)
        if lp % tq:
            build(pl.ds(lp - lp % tq, lp % tq), lp % tq)

    q0 = pl.multiple_of(CHUNK + i * tq, 128)
    rows_q = pl.ds(q0, tq)
    lane = lax.broadcasted_iota(jnp.int32, (tq, LANES), 1)
    mine = lax.shift_right_logical(lane, 3) == h
    eqm = jnp.where(mine, eq_ref[rows_q, :], jnp.zeros((), BF16))
    q16 = q_ref[rows_q, :]

    bound = bound_ref[0, 0]
    safe = bound <= FOX_SAFE_BOUND

    def finish(acc, rows):
        o_ref[rows, :] = (acc[:, :HEAD_DIM] / acc[:, HEAD_DIM:]).astype(o_ref.dtype)

    @pl.when(safe)
    def _():
        eqb = jnp.where(lane == FOX_EXTRA * h + FOX_M_LANE, jnp.full((tq, LANES), -bound, BF16), eqm)
        qaug = jnp.concatenate([q16, eqb], axis=1)

        def probs(qa, krows, mask=None):
            s = _dot_nt(qa, kaug_ref[krows, :])
            if mask is not None:
                s = jnp.where(mask, s, NEG_BIG)
            return jnp.exp2(s).astype(BF16)

        for ii in range(pl.cdiv(lp - CHUNK, tq)):
            @pl.when(i == ii)
            def _(ii=ii):
                acc = _dot(probs(qaug, slice(0, CHUNK), lane >= FRONT_PAD), vaug_ref[0:CHUNK, :])
                for start in range(CHUNK, CHUNK + ii * tq, FOX_KEY_CHUNK):
                    kr = slice(start, start + FOX_KEY_CHUNK)
                    acc = acc + _dot(probs(qaug, kr), vaug_ref[kr, :])
                base = CHUNK + ii * tq
                for r in range(tq // FOX_DIAG_PIECE):
                    rows = slice(r * FOX_DIAG_PIECE, (r + 1) * FOX_DIAG_PIECE)
                    keys = slice(base, base + (r + 1) * FOX_DIAG_PIECE)
                    shape = (FOX_DIAG_PIECE, (r + 1) * FOX_DIAG_PIECE)
                    qpos = lax.broadcasted_iota(jnp.int32, shape, 0) + r * FOX_DIAG_PIECE
                    kpos = lax.broadcasted_iota(jnp.int32, shape, 1)
                    p = probs(qaug[rows], keys, kpos <= qpos)
                    finish(acc[rows] + _dot(p, vaug_ref[keys, :]), rows)

    @pl.when(jnp.logical_not(safe))
    def _():
        qaug = jnp.concatenate([q16, eqm], axis=1)
        qpos = q0 + lax.broadcasted_iota(jnp.int32, (tq, LANES), 0)

        def body(c, carry):
            m, acc = carry
            krows = pl.ds(pl.multiple_of(c * CHUNK, CHUNK), CHUNK)
            kpos = c * CHUNK + lane
            s = _dot_nt(qaug, kaug_ref[krows, :])
            s = jnp.where(kpos >= FRONT_PAD, jnp.where(kpos <= qpos, s, NEG_BIG), NEG_BIG)
            m_new = jnp.maximum(m, jnp.max(s, axis=-1, keepdims=True))
            p = jnp.exp2(s - m_new)
            acc = jnp.exp2(m - m_new) * acc + _dot(p.astype(BF16), vaug_ref[krows, :])
            return m_new, acc

        init = (jnp.full((tq, 1), NEG_BIG, F32), jnp.zeros((tq, 2 * HEAD_DIM), F32))
        _, acc = lax.fori_loop(0, (q0 + tq) // CHUNK, body, init)
        finish(acc, slice(0, tq))


def _fox(p1, eq, ek, bound, weights, wt, wt_rows, *, seq, tq):
    b, lp, _ = p1.shape
    nq = seq // tq
    nsteps = b * HEADS * nq
    colmap = lambda base: (lambda bi, hi, qi: (bi, 0, base + hi))
    step = lambda bi, hi, qi: (bi * HEADS + hi) * nq + qi

    cast_in, cast_out, cast_shapes = [], [], []
    for w in weights:
        _, r, c = w.shape
        rp = r // nsteps
        assert r % nsteps == 0 and rp % 16 == 0
        cast_in.append(pl.BlockSpec((None, rp, c), lambda bi, hi, qi: (0, step(bi, hi, qi), 0)))
        cast_out.append(pl.BlockSpec((rp, c), lambda bi, hi, qi: (step(bi, hi, qi), 0)))
        cast_shapes.append(jax.ShapeDtypeStruct((r, c), BF16))
    start, n = wt_rows
    rp = n // nsteps
    assert n % nsteps == 0 and rp % 16 == 0 and start % F32_SUBLANES == 0
    cast_in.append(_element_rows(rp, wt.shape[2], lambda bi, hi, qi: (
        start // F32_SUBLANES + step(bi, hi, qi) * (rp // F32_SUBLANES))))
    cast_out.append(pl.BlockSpec((rp, wt.shape[2]), lambda bi, hi, qi: (step(bi, hi, qi), 0)))
    cast_shapes.append(jax.ShapeDtypeStruct((n, wt.shape[2]), BF16))

    outs = pl.pallas_call(
        functools.partial(_fox_kernel, tq=tq, lp=lp, n_cast=len(cast_in)),
        grid=(b, HEADS, nq),
        in_specs=[
            pl.BlockSpec((None, lp, HEAD_DIM), colmap(COL_FOX_Q)),
            pl.BlockSpec((None, lp, HEAD_DIM), colmap(COL_FOX_K)),
            pl.BlockSpec((None, lp, HEAD_DIM), colmap(COL_FOX_V)),
            pl.BlockSpec((None, lp, LANES), lambda bi, hi, qi: (bi, 0, 0)),
            pl.BlockSpec((None, lp, LANES), lambda bi, hi, qi: (bi, 0, 0)),
            pl.BlockSpec(memory_space=pltpu.SMEM),
        ] + cast_in,
        out_specs=[pl.BlockSpec((None, tq, HEAD_DIM), lambda bi, hi, qi: (bi, qi, hi))] + cast_out,
        out_shape=[jax.ShapeDtypeStruct((b, seq, HEADS * HEAD_DIM), BF16)] + cast_shapes,
        scratch_shapes=[
            pltpu.VMEM((lp, 2 * HEAD_DIM), BF16),
            pltpu.VMEM((lp, 2 * HEAD_DIM), BF16),
        ],
        compiler_params=_params("parallel", "parallel", "arbitrary"),
        name="fox",
    )(p1, p1, p1, eq, ek, bound, *weights, wt)
    return outs[0], outs[1:]


def _merge_kernel(x_ref, g_ref, ya_ref, yb_ref, wga_ref, wgb_ref, wa_ref, wb_ref, wo_ref,
                  o_ref, u_ref):
    j = pl.program_id(1)

    @pl.when(j == 0)
    def _():
        x = x_ref[...]
        u_ref[...] = _norm_rows(x, g_ref[...]).astype(BF16)
        o_ref[...] = x

    u = u_ref[...]
    ga = _dot_nt(u, wga_ref[...])
    gb = _dot_nt(u, wgb_ref[...])
    pa = _dot(ya_ref[...], wa_ref[...])
    pb = _dot(yb_ref[...], wb_ref[...])
    mix = _sigmoid(ga) * pa + _sigmoid(gb) * pb
    o_ref[...] += _dot(mix.astype(BF16), wo_ref[...])


def _merge(x2d, g, ya, yb, wg_t, wa, wb, wo, *, tm, tj):
    m, d = x2d.shape
    rowblk = lambda i, j: (i, 0)
    colblk = lambda i, j: (0, j)
    return pl.pallas_call(
        _merge_kernel,
        grid=(m // tm, d // tj),
        in_specs=[
            pl.BlockSpec((tm, d), rowblk),
            pl.BlockSpec((1, d), lambda i, j: (0, 0)),
            pl.BlockSpec((tm, d), rowblk),
            pl.BlockSpec((tm, d), rowblk),
            pl.BlockSpec((tj, d), lambda i, j: (j, 0)),
            pl.BlockSpec((tj, d), lambda i, j: (j + d // tj, 0)),
            pl.BlockSpec((d, tj), colblk),
            pl.BlockSpec((d, tj), colblk),
            pl.BlockSpec((tj, d), lambda i, j: (j, 0)),
        ],
        out_specs=pl.BlockSpec((tm, d), rowblk),
        out_shape=jax.ShapeDtypeStruct((m, d), F32),
        scratch_shapes=[pltpu.VMEM((tm, d), BF16)],
        compiler_params=_params("parallel", "arbitrary"),
        name="merge",
    )(x2d, g, ya, yb, wg_t, wg_t, wa, wb, wo)


def _mlp_kernel(h_ref, g_ref, wu_ref, wd_ref, gf_ref, o_ref, u_ref):
    f = pl.program_id(1)

    @pl.when(f == 0)
    def _():
        x = h_ref[...]
        u_ref[...] = _norm_rows(x, g_ref[...]).astype(BF16)
        o_ref[...] = x

    a = jnp.maximum(_dot(u_ref[...], wu_ref[...]), 0.0)
    o_ref[...] += _dot((a * a).astype(BF16), wd_ref[...])

    @pl.when(f == pl.num_programs(1) - 1)
    def _():
        o_ref[...] = _norm_rows(o_ref[...], gf_ref[...])


def _mlp(h2d, g, wu, wd, gf, *, tm, tf):
    m, d = h2d.shape
    dff = wu.shape[1]
    rowblk = lambda i, f: (i, 0)
    return pl.pallas_call(
        _mlp_kernel,
        grid=(m // tm, dff // tf),
        in_specs=[
            pl.BlockSpec((tm, d), rowblk),
            pl.BlockSpec((1, d), lambda i, f: (0, 0)),
            pl.BlockSpec((d, tf), lambda i, f: (0, f)),
            pl.BlockSpec((tf, d), lambda i, f: (f, 0)),
            pl.BlockSpec((1, d), lambda i, f: (0, 0)),
        ],
        out_specs=pl.BlockSpec((tm, d), rowblk),
        out_shape=jax.ShapeDtypeStruct((m, d), F32),
        scratch_shapes=[pltpu.VMEM((tm, d), BF16)],
        compiler_params=_params("parallel", "arbitrary"),
        name="mlp",
    )(h2d, g, wu, wd, gf)


def _constants():
    r = np.arange(128)
    ltri = (r[:, None] >= r[None, :]).astype(np.float32)
    ltri3 = np.concatenate([ltri, ltri, ltri], axis=1)

    sq = np.zeros((3 * LANES, LANES), np.float32)
    sk = np.zeros((3 * LANES, LANES), np.float32)
    cq = np.zeros((1, LANES), np.float32)
    ck = np.zeros((1, LANES), np.float32)
    for h in range(HEADS):
        for p in range(3):
            sq[p * LANES + SM_F + h, FOX_EXTRA * h + p] = 1.0
            sk[p * LANES + SM_F + h, FOX_EXTRA * h + 3 + p] = -1.0
            cq[0, FOX_EXTRA * h + 3 + p] = 1.0
            ck[0, FOX_EXTRA * h + p] = 1.0
        ck[0, FOX_EXTRA * h + FOX_M_LANE] = 1.0

    shift3 = np.zeros((3 * CHUNK, 2 * CHUNK), np.float32)
    for s in range(1, CONV_K):
        shift3[(s - 1) * CHUNK + r, CHUNK + r - s] = 1.0

    src_lane = np.zeros((HEADS * N_GQ * LANES,), np.int64)
    for h in range(HEADS):
        for q in range(N_GQ):
            src_lane[(h * N_GQ + q) * LANES:(h * N_GQ + q + 1) * LANES] = 16 * q + h
    piece_lane = np.arange(GQ_PIECES * LANES) % LANES
    sel = (piece_lane[:, None] == src_lane[None, :]).astype(np.float32)
    oneh = (np.repeat(16 * GQ_G + np.arange(HEADS), ROW_REP)[:, None]
            == piece_lane[None, :]).astype(np.float32)

    as_bf = lambda a: jnp.asarray(a, BF16)
    return dict(ltri3=as_bf(ltri3), sq=as_bf(sq), sk=as_bf(sk), cq=jnp.asarray(cq),
                ck=jnp.asarray(ck), shift3=as_bf(shift3), sel=as_bf(sel), oneh=as_bf(oneh))


def _score_bound(gq, gk):
    gmax = jnp.max(jnp.abs(gq.astype(F32))) * jnp.max(jnp.abs(gk.astype(F32)))
    return (gmax * (HEAD_DIM ** 0.5 * LOG2E * 1.02)).astype(BF16).astype(F32).reshape(1, 1)


def _row128(vec, offset):
    return jnp.zeros((1, LANES), F32).at[0, offset:offset + HEADS].set(vec.astype(F32))


def _pick_rows_tile(total, target):
    best = 128
    for t in range(128, target + 1, 128):
        if total % t == 0:
            best = t
    return best


def _layer(x, meta_tokens, mix_norm_g, w_in, conv_w, a_log, dt_bias, gdn_norm_g, w_o_gdn,
           fox_q_norm_g, fox_k_norm_g, fox_f_bias, w_o_fox, w_out, mlp_norm_g, w_up, w_down,
           final_norm_g):
    b, seq, d = x.shape
    lp = CHUNK + seq
    qk = HEADS * HEAD_DIM

    o_z = 3 * qk
    o_b = o_z + qk
    o_a = o_b + HEADS
    o_fq = o_a + HEADS
    o_f = o_fq + 3 * qk
    o_ga = o_f + HEADS
    o_gb = o_ga + d

    wt = jnp.swapaxes(w_in, 1, 2)
    row = lambda v: v.reshape(1, -1).astype(F32)
    cst = _constants()

    tm_in = _pick_rows_tile(lp, 1408)
    hn = _prenorm(x, meta_tokens.astype(x.dtype), row(mix_norm_g), tile=tm_in)
    p1, ps = _inproj(hn.reshape(b * lp, d), wt, o_b, o_fq, o_f - o_fq,
                     ((o_b, o_fq - o_b), (o_f, o_ga - o_f)),
                     row(fox_q_norm_g), row(fox_k_norm_g), HEAD_DIM ** -0.5 * LOG2E,
                     tm=tm_in, tn=1024)
    p1 = p1.reshape(b, lp, P1_COLS)
    ps = ps.reshape(b, lp, LANES)

    yp, eq, ek = _gate_prep(ps, _row128(a_log, SM_A), _row128(dt_bias, SM_A),
                            _row128(fox_f_bias, SM_F), cst["ltri3"], cst["sq"], cst["sk"],
                            cst["cq"], cst["ck"])
    ya = _gdn(p1, yp, cst["sel"], cst["oneh"], conv_w.astype(F32), row(gdn_norm_g),
              cst["shift3"], seq=seq, hg=16)
    yb, (wog, wof, wout, wup, wdown, wg_t) = _fox(
        p1, eq, ek, _score_bound(fox_q_norm_g, fox_k_norm_g),
        [w_o_gdn, w_o_fox, w_out, w_up, w_down], wt, (o_ga, 2 * d), seq=seq, tq=min(2048, seq))

    x2d = x.reshape(b * seq, d)
    h1 = _merge(x2d, row(mix_norm_g), ya.reshape(b * seq, qk), yb.reshape(b * seq, qk),
                wg_t, wog, wof, wout, tm=512, tj=512)
    out = _mlp(h1, row(mlp_norm_g), wup, wdown, row(final_norm_g), tm=512, tf=1024)
    return out.reshape(b, seq, d)


def kernel(x, meta_tokens, mix_norm_g, w_in, conv_w, a_log, dt_bias, gdn_norm_g, w_o_gdn,
           fox_q_norm_g, fox_k_norm_g, fox_f_bias, w_o_fox, w_out, mlp_norm_g, w_up, w_down,
           final_norm_g):
    assert w_in.shape[0] == 1, "single-layer block"
    return _layer(x, meta_tokens, mix_norm_g[0], w_in, conv_w[0], a_log[0], dt_bias[0],
                  gdn_norm_g[0], w_o_gdn, fox_q_norm_g[0], fox_k_norm_g[0], fox_f_bias[0],
                  w_o_fox, w_out, mlp_norm_g[0], w_up, w_down, final_norm_g)
```

```python
import functools
import math

import numpy as np
import jax
import jax.numpy as jnp
from jax import lax
from jax.experimental import pallas as pl
from jax.experimental.pallas import tpu as pltpu

F32 = jnp.float32
BF16 = jnp.bfloat16

NORM_EPS = 1e-6
N_META = 16
HEADS = 16
HEAD_DIM = 128
CONV_K = 4
LANES = 128
F32_SUBLANES = 8
CHUNK = 128
FRONT_PAD = CHUNK - N_META
NEG_BIG = -1e30
LOG2E = math.log2(math.e)
VMEM_LIMIT = 56 * 1024 * 1024

COL_GDN_Q, COL_GDN_K, COL_GDN_V, COL_Z = 0, 16, 32, 48
COL_FOX_Q, COL_FOX_K, COL_FOX_V = 64, 80, 96
P1_COLS = 112 * LANES
SM_B, SM_A, SM_F = 0, 16, 32
GQ_G, GQ_BETA = range(2)
N_GQ = 2
GQ_PIECES = 2
ROW_REP = 8
FOX_SAFE_BOUND = 40.0
FOX_KEY_CHUNK = 1024
FOX_DIAG_PIECE = 512


def _params(*sem):
    return pltpu.CompilerParams(dimension_semantics=sem, vmem_limit_bytes=VMEM_LIMIT)


def _norm_rows(x, g):
    ms = jnp.mean(x * x, axis=-1, keepdims=True)
    return x * lax.rsqrt(ms + NORM_EPS) * g


def _split3(x):
    hi = x.astype(BF16)
    r = x - hi.astype(F32)
    mid = r.astype(BF16)
    lo = (r - mid.astype(F32)).astype(BF16)
    return hi, mid, lo


def _dot(a, b):
    return jnp.dot(a, b, preferred_element_type=F32)


def _dot_nt(a, b):
    return lax.dot_general(a, b, (((1,), (1,)), ((), ())), preferred_element_type=F32)


def _dot_tn(a, b):
    return lax.dot_general(a, b, (((0,), (0,)), ((), ())), preferred_element_type=F32)


def _sigmoid(x):
    return 1.0 / (1.0 + jnp.exp(-x))


def _silu(x):
    return x * _sigmoid(x)


def _softplus(x):
    return jnp.maximum(x, 0.0) + jnp.log1p(jnp.exp(-jnp.abs(x)))


def _prenorm_kernel(x_ref, meta_ref, g_ref, o_ref, *, tile):
    t = pl.program_id(1)

    def norm_chunks(first_out, n, shift):
        def body(c, carry):
            dst = pl.ds(pl.multiple_of(first_out + c * CHUNK, CHUNK), CHUNK)
            src = pl.ds(pl.multiple_of(first_out - shift + c * CHUNK, CHUNK), CHUNK)
            o_ref[dst, :] = _norm_rows(x_ref[src, :], g_ref[...]).astype(o_ref.dtype)
            return carry
        lax.fori_loop(0, n, body, 0)

    @pl.when(t == 0)
    def _():
        o_ref[0:FRONT_PAD, :] = jnp.zeros((FRONT_PAD, o_ref.shape[1]), o_ref.dtype)
        o_ref[FRONT_PAD:CHUNK, :] = _norm_rows(meta_ref[...], g_ref[...]).astype(o_ref.dtype)
        norm_chunks(CHUNK, tile // CHUNK - 1, CHUNK)

    @pl.when(t > 0)
    def _():
        norm_chunks(0, tile // CHUNK, 0)


def _prenorm(x, meta_tokens, g, *, tile):
    b, seq, d = x.shape
    lp = CHUNK + seq
    sub = F32_SUBLANES
    assert lp % tile == 0 and tile % CHUNK == 0 and CHUNK % sub == 0
    x_rows = lambda bi, t: sub * jnp.maximum(t * (tile // sub) - CHUNK // sub, 0)
    return pl.pallas_call(
        functools.partial(_prenorm_kernel, tile=tile),
        grid=(b, lp // tile),
        in_specs=[
            pl.BlockSpec((None, pl.Element(tile), pl.Element(d)),
                         lambda bi, t: (bi, x_rows(bi, t), 0)),
            pl.BlockSpec((N_META, d), lambda bi, t: (0, 0)),
            pl.BlockSpec((1, d), lambda bi, t: (0, 0)),
        ],
        out_specs=pl.BlockSpec((None, tile, d), lambda bi, t: (bi, t, 0)),
        out_shape=jax.ShapeDtypeStruct((b, lp, d), BF16),
        compiler_params=_params("parallel", "arbitrary"),
        name="prenorm",
    )(x, meta_tokens, g)


def _inproj_kernel(x_ref, w_ref, wsa_ref, wsb_ref, gq_ref, gk_ref, o_ref, os_ref, *,
                   q_tiles, k_tiles, q_scale):
    j = pl.program_id(1)

    @pl.when(j == 0)
    def _():
        pad = LANES - wsa_ref.shape[0] - wsb_ref.shape[0]
        ws = jnp.concatenate([wsa_ref[...], wsb_ref[...], jnp.zeros((pad, wsa_ref.shape[1]), F32)],
                             axis=0)
        os_ref[...] = _dot_nt(x_ref[...], ws.astype(BF16))

    def project():
        return _dot_nt(x_ref[...], w_ref[...].astype(BF16))

    def project_head_norm(g_ref, scale):
        acc = project()
        for h in range(o_ref.shape[1] // HEAD_DIM):
            cols = slice(h * HEAD_DIM, (h + 1) * HEAD_DIM)
            o_ref[:, cols] = (_norm_rows(acc[:, cols], g_ref[...]) * scale).astype(o_ref.dtype)

    is_q = jnp.logical_and(j >= q_tiles[0], j < q_tiles[1])
    is_k = jnp.logical_and(j >= k_tiles[0], j < k_tiles[1])

    @pl.when(is_q)
    def _():
        project_head_norm(gq_ref, q_scale)

    @pl.when(is_k)
    def _():
        project_head_norm(gk_ref, 1.0)

    @pl.when(jnp.logical_not(jnp.logical_or(is_q, is_k)))
    def _():
        o_ref[...] = project().astype(o_ref.dtype)


def _element_rows(tile, d, start_tiles):
    return pl.BlockSpec((None, pl.Element(tile), pl.Element(d)),
                        lambda *ids: (0, F32_SUBLANES * start_tiles(*ids), 0))


def _inproj(hn, wt, n_a, b_start, n_b, small, gq, gk, q_scale, *, tm, tn):
    m, d = hn.shape
    na, nb = n_a // tn, n_b // tn
    assert nb % 3 == 0
    q_tiles = (na, na + nb // 3)
    k_tiles = (na + nb // 3, na + 2 * nb // 3)
    (s0, c0), (s1, c1) = small
    sub = F32_SUBLANES
    assert all(v % sub == 0 for v in (tn, b_start - n_a, s0, s1))

    def w_rows(i, j):
        return j * (tn // sub) + jnp.where(j >= na, (b_start - n_a) // sub, 0)

    return pl.pallas_call(
        functools.partial(_inproj_kernel, q_tiles=q_tiles, k_tiles=k_tiles, q_scale=q_scale),
        grid=(m // tm, na + nb),
        in_specs=[
            pl.BlockSpec((tm, d), lambda i, j: (i, 0)),
            _element_rows(tn, d, w_rows),
            _element_rows(c0, d, lambda i, j: s0 // sub),
            _element_rows(c1, d, lambda i, j: s1 // sub),
            pl.BlockSpec((1, HEAD_DIM), lambda i, j: (0, 0)),
            pl.BlockSpec((1, HEAD_DIM), lambda i, j: (0, 0)),
        ],
        out_specs=[
            pl.BlockSpec((tm, tn), lambda i, j: (i, j)),
            pl.BlockSpec((tm, LANES), lambda i, j: (i, 0)),
        ],
        out_shape=[
            jax.ShapeDtypeStruct((m, (na + nb) * tn), BF16),
            jax.ShapeDtypeStruct((m, LANES), F32),
        ],
        compiler_params=_params("parallel", "arbitrary"),
        name="inproj",
    )(hn, wt, wt, wt, gq, gk)


def _gate_prep_kernel(ps_ref, alog_ref, dtb_ref, fb_ref, ltri_ref, yp_ref, cp_ref, crow_ref, *, nblk):
    lane = lax.broadcasted_iota(jnp.int32, (CHUNK, LANES), 1)
    rowi = lax.broadcasted_iota(jnp.int32, (CHUNK, LANES), 0)

    def body(i, carry):
        rows = pl.ds(pl.multiple_of(i * CHUNK, CHUNK), CHUNK)
        x = ps_ref[rows, :]
        valid = (i * CHUNK + rowi) >= FRONT_PAD

        beta = jnp.where(valid, _sigmoid(x), 0.0)
        gstep = jnp.where(valid, -jnp.exp(alog_ref[...]) * _softplus(x + dtb_ref[...]), 0.0)
        gcum = _dot(ltri_ref[...], jnp.concatenate(_split3(gstep), axis=0))
        y = jnp.where(lane < 16 * (GQ_G + 1), pltpu.roll(gcum, LANES + 16 * GQ_G - SM_A, 1),
                      pltpu.roll(beta, 16 * GQ_BETA - SM_B, 1))
        yp_ref[rows, :] = jnp.concatenate(_split3(y)[:GQ_PIECES], axis=1)

        xf = x + fb_ref[...]
        ls = (jnp.minimum(xf, 0.0) - jnp.log1p(jnp.exp(-jnp.abs(xf)))) * LOG2E
        cum = _dot(ltri_ref[...], jnp.concatenate(_split3(ls), axis=0)) + carry
        cp_ref[rows, :] = jnp.concatenate(_split3(cum), axis=1)
        crow_ref[:, rows] = cum.T
        return cum[CHUNK - 1:CHUNK, :]

    lax.fori_loop(0, nblk, body, jnp.zeros((1, LANES), F32))


def _gate_prep(ps, alog_row, dtb_row, fb_row, ltri3):
    b, lp, _ = ps.shape
    const = lambda bi: (0, 0)
    return pl.pallas_call(
        functools.partial(_gate_prep_kernel, nblk=lp // CHUNK),
        grid=(b,),
        in_specs=[
            pl.BlockSpec((None, lp, LANES), lambda bi: (bi, 0, 0)),
            pl.BlockSpec((1, LANES), const),
            pl.BlockSpec((1, LANES), const),
            pl.BlockSpec((1, LANES), const),
            pl.BlockSpec((CHUNK, 3 * CHUNK), const),
        ],
        out_specs=[
            pl.BlockSpec((None, lp, GQ_PIECES * LANES), lambda bi: (bi, 0, 0)),
            pl.BlockSpec((None, lp, 3 * LANES), lambda bi: (bi, 0, 0)),
            pl.BlockSpec((None, LANES, lp), lambda bi: (bi, 0, 0)),
        ],
        out_shape=[
            jax.ShapeDtypeStruct((b, lp, GQ_PIECES * LANES), BF16),
            jax.ShapeDtypeStruct((b, lp, 3 * LANES), BF16),
            jax.ShapeDtypeStruct((b, LANES, lp), F32),
        ],
        compiler_params=_params("parallel"),
        name="gate_prep",
    )(ps, alog_row, dtb_row, fb_row, ltri3)


def _gdn_kernel(q_ref, k_ref, v_ref, z_ref, yp_ref, sel_ref, oneh_ref, wq_ref, wk_ref, wv_ref,
                gn_ref, shift_ref, o_ref, s_ref, prev_ref, *, hg):
    c = pl.program_id(2)

    @pl.when(c == 0)
    def _():
        s_ref[...] = jnp.zeros_like(s_ref)
        prev_ref[...] = jnp.zeros_like(prev_ref)

    row = lax.broadcasted_iota(jnp.int32, (CHUNK, CHUNK), 0)
    col = lax.broadcasted_iota(jnp.int32, (CHUNK, CHUNK), 1)
    tril = row >= col
    strict = row > col
    heads = range(hg)
    hsl = lambda a, u: a[:, u * HEAD_DIM:(u + 1) * HEAD_DIM]

    yp = yp_ref[...]
    gb = _dot(yp, sel_ref[...])
    g_rows = _dot_nt(oneh_ref[...], yp)
    gq = lambda u, q: gb[:, (u * N_GQ + q) * LANES:(u * N_GQ + q + 1) * LANES]
    g_col = [gq(u, GQ_G) for u in heads]
    beta = [gq(u, GQ_BETA) for u in heads]
    eg = [jnp.exp(g_col[u]) for u in heads]

    def conv_act(x_ref, w_ref, t):
        x16 = x_ref[...]
        x2 = jnp.concatenate([prev_ref[t], x16], axis=0)
        prev_ref[t] = x16
        sh = _dot(shift_ref[...], x2)
        w = w_ref[...]
        acc = x16.astype(F32) * w[CONV_K - 1:CONV_K, :]
        for s in range(1, CONV_K):
            acc = acc + sh[(s - 1) * CHUNK:s * CHUNK, :] * w[CONV_K - 1 - s:CONV_K - s, :]
        return _silu(acc)

    qa = conv_act(q_ref, wq_ref, 0)
    ka = conv_act(k_ref, wk_ref, 1)
    va = conv_act(v_ref, wv_ref, 2)

    def l2n(x):
        return x * lax.rsqrt(jnp.sum(x * x, axis=-1, keepdims=True) + NORM_EPS)

    qn = [l2n(hsl(qa, u)) * (HEAD_DIM ** -0.5) for u in heads]
    kn = [l2n(hsl(ka, u)) for u in heads]
    k16 = [kn[u].astype(BF16) for u in heads]

    state = [s_ref[u] for u in heads]
    kb = [kn[u] * beta[u] for u in heads]
    r1 = [_dot(jnp.concatenate([kb[u] * eg[u], qn[u] * eg[u]], axis=0).astype(BF16),
               state[u].astype(BF16)) for u in heads]

    kq = [_dot_nt(jnp.concatenate([kb[u], qn[u]], axis=0).astype(BF16), k16[u])
          for u in heads]
    ediff = [jnp.exp(jnp.minimum(g_col[u] - g_rows[u * ROW_REP:u * ROW_REP + 1, :], 0.0))
             for u in heads]
    a = [kq[u][:CHUNK] * jnp.where(strict, ediff[u], 0.0) for u in heads]
    qk_m = [(kq[u][CHUNK:] * jnp.where(tril, ediff[u], 0.0)).astype(BF16) for u in heads]

    a16 = [a[u].astype(BF16) for u in heads]
    yk = [_dot(a16[u], a16[u]) for u in heads]
    sk = [-a[u] for u in heads]
    nlev = int(np.log2(CHUNK)) - 1
    for lev in range(nlev):
        yk16 = [yk[u].astype(BF16) for u in heads]
        if lev < nlev - 1:
            zz = [_dot(jnp.concatenate([sk[u], yk[u]], axis=0).astype(BF16), yk16[u]) for u in heads]
            sk = [sk[u] + yk[u] + zz[u][:CHUNK] for u in heads]
            yk = [zz[u][CHUNK:] for u in heads]
        else:
            sk = [sk[u] + yk[u] + _dot(sk[u].astype(BF16), yk16[u]) for u in heads]

    rhs = [hsl(va, u) * beta[u] - r1[u][:CHUNK] for u in heads]
    v_new = [(rhs[u] + _dot(sk[u].astype(BF16), rhs[u].astype(BF16))).astype(BF16) for u in heads]
    o = [r1[u][CHUNK:] + _dot(qk_m[u], v_new[u]) for u in heads]
    for u in heads:
        k_dec = kn[u] * jnp.exp(g_col[u][CHUNK - 1:CHUNK, :] - g_col[u])
        s_ref[u] = state[u] * eg[u][CHUNK - 1:CHUNK, :] + _dot_tn(k_dec.astype(BF16), v_new[u])

    on = jnp.concatenate([_norm_rows(o[u], gn_ref[...]) for u in heads], axis=1)
    o_ref[...] = (on * _silu(z_ref[...].astype(F32))).astype(o_ref.dtype)


def _gdn(p1, yp, sel, oneh, conv_w, gnorm, shift3, *, seq, hg):
    b, lp, _ = p1.shape
    nc = lp // CHUNK
    gw = hg * HEAD_DIM
    ng = HEADS // hg
    col = lambda base: (lambda bi, gi, ci: (bi, ci, base * LANES // gw + gi))
    wcol = lambda base: (lambda bi, gi, ci: (0, base * LANES // gw + gi))
    const = lambda bi, gi, ci: (0, 0)
    return pl.pallas_call(
        functools.partial(_gdn_kernel, hg=hg),
        grid=(b, ng, nc),
        in_specs=[
            pl.BlockSpec((None, CHUNK, gw), col(COL_GDN_Q)),
            pl.BlockSpec((None, CHUNK, gw), col(COL_GDN_K)),
            pl.BlockSpec((None, CHUNK, gw), col(COL_GDN_V)),
            pl.BlockSpec((None, CHUNK, gw), col(COL_Z)),
            pl.BlockSpec((None, CHUNK, GQ_PIECES * LANES), lambda bi, gi, ci: (bi, ci, 0)),
            pl.BlockSpec((GQ_PIECES * LANES, hg * N_GQ * LANES), lambda bi, gi, ci: (0, gi)),
            pl.BlockSpec((hg * ROW_REP, GQ_PIECES * LANES), lambda bi, gi, ci: (gi, 0)),
            pl.BlockSpec((CONV_K, gw), wcol(COL_GDN_Q)),
            pl.BlockSpec((CONV_K, gw), wcol(COL_GDN_K)),
            pl.BlockSpec((CONV_K, gw), wcol(COL_GDN_V)),
            pl.BlockSpec((1, HEAD_DIM), const),
            pl.BlockSpec((3 * CHUNK, 2 * CHUNK), const),
        ],
        out_specs=pl.BlockSpec((None, CHUNK, gw),
                               lambda bi, gi, ci: (bi, jnp.maximum(ci - 1, 0), gi)),
        out_shape=jax.ShapeDtypeStruct((b, seq, HEADS * HEAD_DIM), BF16),
        scratch_shapes=[
            pltpu.VMEM((hg, HEAD_DIM, HEAD_DIM), F32),
            pltpu.VMEM((3, CHUNK, gw), BF16),
        ],
        compiler_params=_params("parallel", "parallel", "arbitrary"),
        name="gdn",
    )(p1, p1, p1, p1, yp, sel, oneh, conv_w, conv_w, conv_w, gnorm, shift3)


def _fox_kernel(*refs, tq, lp, n_cast):
    q_ref, k_ref, v_ref, cp_ref, crow_ref, bound_ref = refs[:6]
    o_ref = refs[6 + n_cast]
    vaug_ref = refs[-1]
    h = pl.program_id(1)
    i = pl.program_id(2)

    for w_ref, w16_ref in zip(refs[6:6 + n_cast], refs[7 + n_cast:7 + 2 * n_cast]):
        w16_ref[...] = w_ref[...].astype(w16_ref.dtype)

    @pl.when(i == 0)
    def _():
        def stage(rows, n):
            vaug_ref[rows, 0:HEAD_DIM] = v_ref[rows, :]
            vaug_ref[rows, HEAD_DIM:2 * HEAD_DIM] = jnp.ones((n, HEAD_DIM), BF16)

        def body(r, c):
            stage(pl.ds(pl.multiple_of(r * tq, 128), tq), tq)
            return c
        lax.fori_loop(0, lp // tq, body, 0)
        if lp % tq:
            stage(pl.ds(lp - lp % tq, lp % tq), lp % tq)

    q0 = pl.multiple_of(CHUNK + i * tq, 128)
    rows_q = pl.ds(q0, tq)
    q16 = q_ref[rows_q, :]
    lane = lax.broadcasted_iota(jnp.int32, (tq, LANES), 1)
    piece_lane = lax.broadcasted_iota(jnp.int32, (3 * LANES, LANES), 0) & (LANES - 1)
    pick = jnp.where(piece_lane == SM_F + h, 1.0, 0.0).astype(BF16)
    c_q = _dot(cp_ref[rows_q, :], pick)

    def c_k(krows):
        group = pl.multiple_of(SM_F + lax.shift_left(lax.shift_right_logical(h, 3), 3), 8)
        rows8 = crow_ref[pl.ds(group, F32_SUBLANES), krows]
        sub = lax.broadcasted_iota(jnp.int32, rows8.shape, 0)
        return jnp.sum(jnp.where(sub == (h & 7), rows8, 0.0), axis=0, keepdims=True)

    def scores(rows, bias_q, krows, n):
        s = _dot_nt(q16[rows], k_ref[krows, :])
        return s + pltpu.repeat(bias_q[rows], n // LANES, 1) - c_k(krows)

    def finish(acc, rows):
        o_ref[rows, :] = (acc[:, :HEAD_DIM] / acc[:, HEAD_DIM:]).astype(o_ref.dtype)

    bound = bound_ref[0, 0]
    safe = bound <= FOX_SAFE_BOUND

    @pl.when(safe)
    def _():
        bias_q = c_q - bound

        def probs(rows, krows, n, mask=None):
            s = scores(rows, bias_q, krows, n)
            if mask is not None:
                s = jnp.where(mask, s, NEG_BIG)
            return jnp.exp2(s).astype(BF16)

        every = slice(0, tq)
        for ii in range(pl.cdiv(lp - CHUNK, tq)):
            @pl.when(i == ii)
            def _(ii=ii):
                p = probs(every, slice(0, CHUNK), CHUNK, lane >= FRONT_PAD)
                acc = _dot(p, vaug_ref[0:CHUNK, :])
                for start in range(CHUNK, CHUNK + ii * tq, FOX_KEY_CHUNK):
                    kr = slice(start, start + FOX_KEY_CHUNK)
                    acc = acc + _dot(probs(every, kr, FOX_KEY_CHUNK), vaug_ref[kr, :])
                base = CHUNK + ii * tq
                for r in range(tq // FOX_DIAG_PIECE):
                    rows = slice(r * FOX_DIAG_PIECE, (r + 1) * FOX_DIAG_PIECE)
                    nk = (r + 1) * FOX_DIAG_PIECE
                    keys = slice(base, base + nk)
                    qpos = lax.broadcasted_iota(jnp.int32, (FOX_DIAG_PIECE, nk), 0) + r * FOX_DIAG_PIECE
                    kpos = lax.broadcasted_iota(jnp.int32, (FOX_DIAG_PIECE, nk), 1)
                    p = probs(rows, keys, nk, kpos <= qpos)
                    finish(acc[rows] + _dot(p, vaug_ref[keys, :]), rows)

    @pl.when(jnp.logical_not(safe))
    def _():
        qpos = q0 + lax.broadcasted_iota(jnp.int32, (tq, LANES), 0)

        def body(c, carry):
            m, acc = carry
            krows = pl.ds(pl.multiple_of(c * CHUNK, CHUNK), CHUNK)
            kpos = c * CHUNK + lane
            s = scores(slice(0, tq), c_q, krows, CHUNK)
            s = jnp.where(kpos >= FRONT_PAD, jnp.where(kpos <= qpos, s, NEG_BIG), NEG_BIG)
            m_new = jnp.maximum(m, jnp.max(s, axis=-1, keepdims=True))
            p = jnp.exp2(s - m_new)
            acc = jnp.exp2(m - m_new) * acc + _dot(p.astype(BF16), vaug_ref[krows, :])
            return m_new, acc

        init = (jnp.full((tq, 1), NEG_BIG, F32), jnp.zeros((tq, 2 * HEAD_DIM), F32))
        _, acc = lax.fori_loop(0, (q0 + tq) // CHUNK, body, init)
        finish(acc, slice(0, tq))


def _fox(p1, cp, crow, bound, weights, wt, wt_rows, *, seq, tq):
    b, lp, _ = p1.shape
    nq = seq // tq
    nsteps = b * HEADS * nq
    colmap = lambda base: (lambda bi, hi, qi: (bi, 0, base + hi))
    step = lambda bi, hi, qi: (bi * HEADS + hi) * nq + qi

    cast_in, cast_out, cast_shapes = [], [], []
    for w in weights:
        _, r, c = w.shape
        rp = r // nsteps
        assert r % nsteps == 0 and rp % 16 == 0
        cast_in.append(pl.BlockSpec((None, rp, c), lambda bi, hi, qi: (0, step(bi, hi, qi), 0)))
        cast_out.append(pl.BlockSpec((rp, c), lambda bi, hi, qi: (step(bi, hi, qi), 0)))
        cast_shapes.append(jax.ShapeDtypeStruct((r, c), BF16))
    start, n = wt_rows
    rp = n // nsteps
    assert n % nsteps == 0 and rp % 16 == 0 and start % F32_SUBLANES == 0
    cast_in.append(_element_rows(rp, wt.shape[2], lambda bi, hi, qi: (
        start // F32_SUBLANES + step(bi, hi, qi) * (rp // F32_SUBLANES))))
    cast_out.append(pl.BlockSpec((rp, wt.shape[2]), lambda bi, hi, qi: (step(bi, hi, qi), 0)))
    cast_shapes.append(jax.ShapeDtypeStruct((n, wt.shape[2]), BF16))

    outs = pl.pallas_call(
        functools.partial(_fox_kernel, tq=tq, lp=lp, n_cast=len(cast_in)),
        grid=(b, HEADS, nq),
        in_specs=[
            pl.BlockSpec((None, lp, HEAD_DIM), colmap(COL_FOX_Q)),
            pl.BlockSpec((None, lp, HEAD_DIM), colmap(COL_FOX_K)),
            pl.BlockSpec((None, lp, HEAD_DIM), colmap(COL_FOX_V)),
            pl.BlockSpec((None, lp, 3 * LANES), lambda bi, hi, qi: (bi, 0, 0)),
            pl.BlockSpec((None, LANES, lp), lambda bi, hi, qi: (bi, 0, 0)),
            pl.BlockSpec(memory_space=pltpu.SMEM),
        ] + cast_in,
        out_specs=[pl.BlockSpec((None, tq, HEAD_DIM), lambda bi, hi, qi: (bi, qi, hi))] + cast_out,
        out_shape=[jax.ShapeDtypeStruct((b, seq, HEADS * HEAD_DIM), BF16)] + cast_shapes,
        scratch_shapes=[pltpu.VMEM((lp, 2 * HEAD_DIM), BF16)],
        compiler_params=_params("parallel", "parallel", "arbitrary"),
        name="fox",
    )(p1, p1, p1, cp, crow, bound, *weights, wt)
    return outs[0], outs[1:]


def _merge_kernel(x_ref, g_ref, ya_ref, yb_ref, wga_ref, wgb_ref, wa_ref, wb_ref, wo_ref,
                  o_ref, u_ref):
    j = pl.program_id(1)

    @pl.when(j == 0)
    def _():
        x = x_ref[...]
        u_ref[...] = _norm_rows(x, g_ref[...]).astype(BF16)
        o_ref[...] = x

    u = u_ref[...]
    ga = _dot_nt(u, wga_ref[...])
    gb = _dot_nt(u, wgb_ref[...])
    pa = _dot(ya_ref[...], wa_ref[...])
    pb = _dot(yb_ref[...], wb_ref[...])
    mix = _sigmoid(ga) * pa + _sigmoid(gb) * pb
    o_ref[...] += _dot(mix.astype(BF16), wo_ref[...])


def _merge(x2d, g, ya, yb, wg_t, wa, wb, wo, *, tm, tj):
    m, d = x2d.shape
    rowblk = lambda i, j: (i, 0)
    colblk = lambda i, j: (0, j)
    return pl.pallas_call(
        _merge_kernel,
        grid=(m // tm, d // tj),
        in_specs=[
            pl.BlockSpec((tm, d), rowblk),
            pl.BlockSpec((1, d), lambda i, j: (0, 0)),
            pl.BlockSpec((tm, d), rowblk),
            pl.BlockSpec((tm, d), rowblk),
            pl.BlockSpec((tj, d), lambda i, j: (j, 0)),
            pl.BlockSpec((tj, d), lambda i, j: (j + d // tj, 0)),
            pl.BlockSpec((d, tj), colblk),
            pl.BlockSpec((d, tj), colblk),
            pl.BlockSpec((tj, d), lambda i, j: (j, 0)),
        ],
        out_specs=pl.BlockSpec((tm, d), rowblk),
        out_shape=jax.ShapeDtypeStruct((m, d), F32),
        scratch_shapes=[pltpu.VMEM((tm, d), BF16)],
        compiler_params=_params("parallel", "arbitrary"),
        name="merge",
    )(x2d, g, ya, yb, wg_t, wg_t, wa, wb, wo)


def _mlp_kernel(h_ref, g_ref, wu_ref, wd_ref, gf_ref, o_ref, u_ref):
    f = pl.program_id(1)

    @pl.when(f == 0)
    def _():
        x = h_ref[...]
        u_ref[...] = _norm_rows(x, g_ref[...]).astype(BF16)
        o_ref[...] = x

    a = jnp.maximum(_dot(u_ref[...], wu_ref[...]), 0.0)
    o_ref[...] += _dot((a * a).astype(BF16), wd_ref[...])

    @pl.when(f == pl.num_programs(1) - 1)
    def _():
        o_ref[...] = _norm_rows(o_ref[...], gf_ref[...])


def _mlp(h2d, g, wu, wd, gf, *, tm, tf):
    m, d = h2d.shape
    dff = wu.shape[1]
    rowblk = lambda i, f: (i, 0)
    return pl.pallas_call(
        _mlp_kernel,
        grid=(m // tm, dff // tf),
        in_specs=[
            pl.BlockSpec((tm, d), rowblk),
            pl.BlockSpec((1, d), lambda i, f: (0, 0)),
            pl.BlockSpec((d, tf), lambda i, f: (0, f)),
            pl.BlockSpec((tf, d), lambda i, f: (f, 0)),
            pl.BlockSpec((1, d), lambda i, f: (0, 0)),
        ],
        out_specs=pl.BlockSpec((tm, d), rowblk),
        out_shape=jax.ShapeDtypeStruct((m, d), F32),
        scratch_shapes=[pltpu.VMEM((tm, d), BF16)],
        compiler_params=_params("parallel", "arbitrary"),
        name="mlp",
    )(h2d, g, wu, wd, gf)


def _constants():
    r = np.arange(128)
    ltri = (r[:, None] >= r[None, :]).astype(np.float32)
    ltri3 = np.concatenate([ltri, ltri, ltri], axis=1)

    shift3 = np.zeros((3 * CHUNK, 2 * CHUNK), np.float32)
    for s in range(1, CONV_K):
        shift3[(s - 1) * CHUNK + r, CHUNK + r - s] = 1.0

    src_lane = np.zeros((HEADS * N_GQ * LANES,), np.int64)
    for h in range(HEADS):
        for q in range(N_GQ):
            src_lane[(h * N_GQ + q) * LANES:(h * N_GQ + q + 1) * LANES] = 16 * q + h
    piece_lane = np.arange(GQ_PIECES * LANES) % LANES
    sel = (piece_lane[:, None] == src_lane[None, :]).astype(np.float32)
    oneh = (np.repeat(16 * GQ_G + np.arange(HEADS), ROW_REP)[:, None]
            == piece_lane[None, :]).astype(np.float32)

    as_bf = lambda a: jnp.asarray(a, BF16)
    return dict(ltri3=as_bf(ltri3), shift3=as_bf(shift3), sel=as_bf(sel), oneh=as_bf(oneh))


def _score_bound(gq, gk):
    gmax = jnp.max(jnp.abs(gq.astype(F32))) * jnp.max(jnp.abs(gk.astype(F32)))
    return (gmax * (HEAD_DIM ** 0.5 * LOG2E * 1.02)).astype(BF16).astype(F32).reshape(1, 1)


def _row128(vec, offset):
    return jnp.zeros((1, LANES), F32).at[0, offset:offset + HEADS].set(vec.astype(F32))


def _pick_rows_tile(total, target):
    best = 128
    for t in range(128, target + 1, 128):
        if total % t == 0:
            best = t
    return best


def _layer(x, meta_tokens, mix_norm_g, w_in, conv_w, a_log, dt_bias, gdn_norm_g, w_o_gdn,
           fox_q_norm_g, fox_k_norm_g, fox_f_bias, w_o_fox, w_out, mlp_norm_g, w_up, w_down,
           final_norm_g):
    b, seq, d = x.shape
    lp = CHUNK + seq
    qk = HEADS * HEAD_DIM

    o_z = 3 * qk
    o_b = o_z + qk
    o_a = o_b + HEADS
    o_fq = o_a + HEADS
    o_f = o_fq + 3 * qk
    o_ga = o_f + HEADS
    o_gb = o_ga + d

    wt = jnp.swapaxes(w_in, 1, 2)
    row = lambda v: v.reshape(1, -1).astype(F32)
    cst = _constants()

    tm_in = _pick_rows_tile(lp, 1408)
    hn = _prenorm(x, meta_tokens.astype(x.dtype), row(mix_norm_g), tile=tm_in)
    p1, ps = _inproj(hn.reshape(b * lp, d), wt, o_b, o_fq, o_f - o_fq,
                     ((o_b, o_fq - o_b), (o_f, o_ga - o_f)),
                     row(fox_q_norm_g), row(fox_k_norm_g), HEAD_DIM ** -0.5 * LOG2E,
                     tm=tm_in, tn=1024)
    p1 = p1.reshape(b, lp, P1_COLS)
    ps = ps.reshape(b, lp, LANES)

    yp, cp, crow = _gate_prep(ps, _row128(a_log, SM_A), _row128(dt_bias, SM_A),
                              _row128(fox_f_bias, SM_F), cst["ltri3"])
    ya = _gdn(p1, yp, cst["sel"], cst["oneh"], conv_w.astype(F32), row(gdn_norm_g),
              cst["shift3"], seq=seq, hg=16)
    yb, (wog, wof, wout, wup, wdown, wg_t) = _fox(
        p1, cp, crow, _score_bound(fox_q_norm_g, fox_k_norm_g),
        [w_o_gdn, w_o_fox, w_out, w_up, w_down], wt, (o_ga, 2 * d), seq=seq, tq=min(2048, seq))

    x2d = x.reshape(b * seq, d)
    h1 = _merge(x2d, row(mix_norm_g), ya.reshape(b * seq, qk), yb.reshape(b * seq, qk),
                wg_t, wog, wof, wout, tm=512, tj=512)
    out = _mlp(h1, row(mlp_norm_g), wup, wdown, row(final_norm_g), tm=512, tf=1024)
    return out.reshape(b, seq, d)


def kernel(x, meta_tokens, mix_norm_g, w_in, conv_w, a_log, dt_bias, gdn_norm_g, w_o_gdn,
           fox_q_norm_g, fox_k_norm_g, fox_f_bias, w_o_fox, w_out, mlp_norm_g, w_up, w_down,
           final_norm_g):
    assert w_in.shape[0] == 1, "single-layer block"
    return _layer(x, meta_tokens, mix_norm_g[0], w_in, conv_w[0], a_log[0], dt_bias[0],
                  gdn_norm_g[0], w_o_gdn, fox_q_norm_g[0], fox_k_norm_g[0], fox_f_bias[0],
                  w_o_fox, w_out, mlp_norm_g[0], w_up, w_down, final_norm_g)
```

```python
import functools
import math

import numpy as np
import jax
import jax.numpy as jnp
from jax import lax
from jax.experimental import pallas as pl
from jax.experimental.pallas import tpu as pltpu

F32 = jnp.float32
BF16 = jnp.bfloat16

NORM_EPS = 1e-6
N_META = 16
HEADS = 16
HEAD_DIM = 128
CONV_K = 4
LANES = 128
F32_SUBLANES = 8
CHUNK = 128
FRONT_PAD = CHUNK - N_META
NEG_BIG = -1e30
LOG2E = math.log2(math.e)
VMEM_LIMIT = 56 * 1024 * 1024

COL_GDN_Q, COL_GDN_K, COL_GDN_V, COL_Z = 0, 16, 32, 48
COL_FOX_Q, COL_FOX_K, COL_FOX_V = 64, 80, 96
P1_COLS = 112 * LANES
SM_B, SM_A, SM_F = 0, 16, 32
GQ_G, GQ_BETA = range(2)
N_GQ = 2
GQ_PIECES = 2
ROW_REP = 8
FOX_SAFE_BOUND = 40.0
FOX_KEY_CHUNK = 1024
FOX_DIAG_PIECE = 512


def _params(*sem):
    return pltpu.CompilerParams(dimension_semantics=sem, vmem_limit_bytes=VMEM_LIMIT)


def _norm_rows(x, g):
    ms = jnp.mean(x * x, axis=-1, keepdims=True)
    return x * lax.rsqrt(ms + NORM_EPS) * g


def _split3(x):
    hi = x.astype(BF16)
    r = x - hi.astype(F32)
    mid = r.astype(BF16)
    lo = (r - mid.astype(F32)).astype(BF16)
    return hi, mid, lo


def _dot(a, b):
    return jnp.dot(a, b, preferred_element_type=F32)


def _dot_nt(a, b):
    return lax.dot_general(a, b, (((1,), (1,)), ((), ())), preferred_element_type=F32)


def _dot_tn(a, b):
    return lax.dot_general(a, b, (((0,), (0,)), ((), ())), preferred_element_type=F32)


def _sigmoid(x):
    return 1.0 / (1.0 + jnp.exp(-x))


def _silu(x):
    return x * _sigmoid(x)


def _softplus(x):
    return jnp.maximum(x, 0.0) + jnp.log1p(jnp.exp(-jnp.abs(x)))


def _prenorm_kernel(x_ref, meta_ref, g_ref, o_ref, *, tile):
    t = pl.program_id(1)

    def norm_chunks(first_out, n, shift):
        def body(c, carry):
            dst = pl.ds(pl.multiple_of(first_out + c * CHUNK, CHUNK), CHUNK)
            src = pl.ds(pl.multiple_of(first_out - shift + c * CHUNK, CHUNK), CHUNK)
            o_ref[dst, :] = _norm_rows(x_ref[src, :], g_ref[...]).astype(o_ref.dtype)
            return carry
        lax.fori_loop(0, n, body, 0)

    @pl.when(t == 0)
    def _():
        o_ref[0:FRONT_PAD, :] = jnp.zeros((FRONT_PAD, o_ref.shape[1]), o_ref.dtype)
        o_ref[FRONT_PAD:CHUNK, :] = _norm_rows(meta_ref[...], g_ref[...]).astype(o_ref.dtype)
        norm_chunks(CHUNK, tile // CHUNK - 1, CHUNK)

    @pl.when(t > 0)
    def _():
        norm_chunks(0, tile // CHUNK, 0)


def _prenorm(x, meta_tokens, g, *, tile):
    b, seq, d = x.shape
    lp = CHUNK + seq
    sub = F32_SUBLANES
    assert lp % tile == 0 and tile % CHUNK == 0 and CHUNK % sub == 0
    x_rows = lambda bi, t: sub * jnp.maximum(t * (tile // sub) - CHUNK // sub, 0)
    return pl.pallas_call(
        functools.partial(_prenorm_kernel, tile=tile),
        grid=(b, lp // tile),
        in_specs=[
            pl.BlockSpec((None, pl.Element(tile), pl.Element(d)),
                         lambda bi, t: (bi, x_rows(bi, t), 0)),
            pl.BlockSpec((N_META, d), lambda bi, t: (0, 0)),
            pl.BlockSpec((1, d), lambda bi, t: (0, 0)),
        ],
        out_specs=pl.BlockSpec((None, tile, d), lambda bi, t: (bi, t, 0)),
        out_shape=jax.ShapeDtypeStruct((b, lp, d), BF16),
        compiler_params=_params("parallel", "arbitrary"),
        name="prenorm",
    )(x, meta_tokens, g)


def _inproj_kernel(x_ref, w_ref, wsa_ref, wsb_ref, gq_ref, gk_ref, o_ref, os_ref, *,
                   q_tiles, k_tiles, q_scale):
    j = pl.program_id(1)

    @pl.when(j == 0)
    def _():
        pad = LANES - wsa_ref.shape[0] - wsb_ref.shape[0]
        ws = jnp.concatenate([wsa_ref[...], wsb_ref[...], jnp.zeros((pad, wsa_ref.shape[1]), F32)],
                             axis=0)
        os_ref[...] = _dot_nt(x_ref[...], ws.astype(BF16))

    def project():
        return _dot_nt(x_ref[...], w_ref[...].astype(BF16))

    def project_head_norm(g_ref, scale):
        acc = project()
        for h in range(o_ref.shape[1] // HEAD_DIM):
            cols = slice(h * HEAD_DIM, (h + 1) * HEAD_DIM)
            o_ref[:, cols] = (_norm_rows(acc[:, cols], g_ref[...]) * scale).astype(o_ref.dtype)

    is_q = jnp.logical_and(j >= q_tiles[0], j < q_tiles[1])
    is_k = jnp.logical_and(j >= k_tiles[0], j < k_tiles[1])

    @pl.when(is_q)
    def _():
        project_head_norm(gq_ref, q_scale)

    @pl.when(is_k)
    def _():
        project_head_norm(gk_ref, 1.0)

    @pl.when(jnp.logical_not(jnp.logical_or(is_q, is_k)))
    def _():
        o_ref[...] = project().astype(o_ref.dtype)


def _element_rows(tile, d, start_tiles):
    return pl.BlockSpec((None, pl.Element(tile), pl.Element(d)),
                        lambda *ids: (0, F32_SUBLANES * start_tiles(*ids), 0))


def _inproj(hn, wt, n_a, b_start, n_b, small, gq, gk, q_scale, *, tm, tn):
    m, d = hn.shape
    na, nb = n_a // tn, n_b // tn
    assert nb % 3 == 0
    q_tiles = (na, na + nb // 3)
    k_tiles = (na + nb // 3, na + 2 * nb // 3)
    (s0, c0), (s1, c1) = small
    sub = F32_SUBLANES
    assert all(v % sub == 0 for v in (tn, b_start - n_a, s0, s1))

    def w_rows(i, j):
        return j * (tn // sub) + jnp.where(j >= na, (b_start - n_a) // sub, 0)

    return pl.pallas_call(
        functools.partial(_inproj_kernel, q_tiles=q_tiles, k_tiles=k_tiles, q_scale=q_scale),
        grid=(m // tm, na + nb),
        in_specs=[
            pl.BlockSpec((tm, d), lambda i, j: (i, 0)),
            _element_rows(tn, d, w_rows),
            _element_rows(c0, d, lambda i, j: s0 // sub),
            _element_rows(c1, d, lambda i, j: s1 // sub),
            pl.BlockSpec((1, HEAD_DIM), lambda i, j: (0, 0)),
            pl.BlockSpec((1, HEAD_DIM), lambda i, j: (0, 0)),
        ],
        out_specs=[
            pl.BlockSpec((tm, tn), lambda i, j: (i, j)),
            pl.BlockSpec((tm, LANES), lambda i, j: (i, 0)),
        ],
        out_shape=[
            jax.ShapeDtypeStruct((m, (na + nb) * tn), BF16),
            jax.ShapeDtypeStruct((m, LANES), F32),
        ],
        compiler_params=_params("parallel", "arbitrary"),
        name="inproj",
    )(hn, wt, wt, wt, gq, gk)


def _gate_prep_kernel(ps_ref, alog_ref, dtb_ref, fb_ref, ltri_ref, yp_ref, cp_ref, crow_ref, *, nblk):
    lane = lax.broadcasted_iota(jnp.int32, (CHUNK, LANES), 1)
    rowi = lax.broadcasted_iota(jnp.int32, (CHUNK, LANES), 0)

    def body(i, carry):
        rows = pl.ds(pl.multiple_of(i * CHUNK, CHUNK), CHUNK)
        x = ps_ref[rows, :]
        valid = (i * CHUNK + rowi) >= FRONT_PAD

        beta = jnp.where(valid, _sigmoid(x), 0.0)
        gstep = jnp.where(valid, -jnp.exp(alog_ref[...]) * _softplus(x + dtb_ref[...]), 0.0)
        gcum = _dot(ltri_ref[...], jnp.concatenate(_split3(gstep), axis=0))
        y = jnp.where(lane < 16 * (GQ_G + 1), pltpu.roll(gcum, LANES + 16 * GQ_G - SM_A, 1),
                      pltpu.roll(beta, 16 * GQ_BETA - SM_B, 1))
        yp_ref[rows, :] = jnp.concatenate(_split3(y)[:GQ_PIECES], axis=1)

        xf = x + fb_ref[...]
        ls = (jnp.minimum(xf, 0.0) - jnp.log1p(jnp.exp(-jnp.abs(xf)))) * LOG2E
        cum = _dot(ltri_ref[...], jnp.concatenate(_split3(ls), axis=0)) + carry
        cp_ref[rows, :] = jnp.concatenate(_split3(cum), axis=1)
        crow_ref[:, rows] = cum.T
        return cum[CHUNK - 1:CHUNK, :]

    lax.fori_loop(0, nblk, body, jnp.zeros((1, LANES), F32))


def _gate_prep(ps, alog_row, dtb_row, fb_row, ltri3):
    b, lp, _ = ps.shape
    const = lambda bi: (0, 0)
    return pl.pallas_call(
        functools.partial(_gate_prep_kernel, nblk=lp // CHUNK),
        grid=(b,),
        in_specs=[
            pl.BlockSpec((None, lp, LANES), lambda bi: (bi, 0, 0)),
            pl.BlockSpec((1, LANES), const),
            pl.BlockSpec((1, LANES), const),
            pl.BlockSpec((1, LANES), const),
            pl.BlockSpec((CHUNK, 3 * CHUNK), const),
        ],
        out_specs=[
            pl.BlockSpec((None, lp, GQ_PIECES * LANES), lambda bi: (bi, 0, 0)),
            pl.BlockSpec((None, lp, 3 * LANES), lambda bi: (bi, 0, 0)),
            pl.BlockSpec((None, LANES, lp), lambda bi: (bi, 0, 0)),
        ],
        out_shape=[
            jax.ShapeDtypeStruct((b, lp, GQ_PIECES * LANES), BF16),
            jax.ShapeDtypeStruct((b, lp, 3 * LANES), BF16),
            jax.ShapeDtypeStruct((b, LANES, lp), F32),
        ],
        compiler_params=_params("parallel"),
        name="gate_prep",
    )(ps, alog_row, dtb_row, fb_row, ltri3)


def _gdn_kernel(*refs, hg):
    s_ref, prev_ref = refs[-2:]

    @pl.when(pl.program_id(2) == 0)
    def _():
        s_ref[...] = jnp.zeros_like(s_ref)
        prev_ref[...] = jnp.zeros_like(prev_ref)

    _run(_gdn_step(*refs, hg=hg))


def _run(*stage_generators):
    live = list(stage_generators)
    while live:
        for g in list(live):
            if next(g, StopIteration) is StopIteration:
                live.remove(g)


def _gdn_step(q_ref, k_ref, v_ref, z_ref, yp_ref, sel_ref, oneh_ref, wq_ref, wk_ref, wv_ref,
              gn_ref, shift_ref, o_ref, s_ref, prev_ref, *, hg):
    row = lax.broadcasted_iota(jnp.int32, (CHUNK, CHUNK), 0)
    col = lax.broadcasted_iota(jnp.int32, (CHUNK, CHUNK), 1)
    tril = row >= col
    strict = row > col
    heads = range(hg)
    hsl = lambda a, u: a[:, u * HEAD_DIM:(u + 1) * HEAD_DIM]

    yp = yp_ref[...]
    gb = _dot(yp, sel_ref[...])
    g_rows = _dot_nt(oneh_ref[...], yp)
    gq = lambda u, q: gb[:, (u * N_GQ + q) * LANES:(u * N_GQ + q + 1) * LANES]
    g_col = [gq(u, GQ_G) for u in heads]
    beta = [gq(u, GQ_BETA) for u in heads]
    eg = [jnp.exp(g_col[u]) for u in heads]
    yield

    def conv_act(x_ref, w_ref, t):
        x16 = x_ref[...]
        x2 = jnp.concatenate([prev_ref[t], x16], axis=0)
        prev_ref[t] = x16
        sh = _dot(shift_ref[...], x2)
        w = w_ref[...]
        acc = x16.astype(F32) * w[CONV_K - 1:CONV_K, :]
        for s in range(1, CONV_K):
            acc = acc + sh[(s - 1) * CHUNK:s * CHUNK, :] * w[CONV_K - 1 - s:CONV_K - s, :]
        return _silu(acc)

    qa = conv_act(q_ref, wq_ref, 0)
    yield
    ka = conv_act(k_ref, wk_ref, 1)
    yield
    va = conv_act(v_ref, wv_ref, 2)
    yield

    def l2n(x):
        return x * lax.rsqrt(jnp.sum(x * x, axis=-1, keepdims=True) + NORM_EPS)

    qn = [l2n(hsl(qa, u)) * (HEAD_DIM ** -0.5) for u in heads]
    kn = [l2n(hsl(ka, u)) for u in heads]
    k16 = [kn[u].astype(BF16) for u in heads]
    yield

    state = [s_ref[u] for u in heads]
    kb = [kn[u] * beta[u] for u in heads]
    r1 = [_dot(jnp.concatenate([kb[u] * eg[u], qn[u] * eg[u]], axis=0).astype(BF16),
               state[u].astype(BF16)) for u in heads]
    yield

    kq = [_dot_nt(jnp.concatenate([kb[u], qn[u]], axis=0).astype(BF16), k16[u])
          for u in heads]
    ediff = [jnp.exp(jnp.minimum(g_col[u] - g_rows[u * ROW_REP:u * ROW_REP + 1, :], 0.0))
             for u in heads]
    a = [kq[u][:CHUNK] * jnp.where(strict, ediff[u], 0.0) for u in heads]
    qk_m = [(kq[u][CHUNK:] * jnp.where(tril, ediff[u], 0.0)).astype(BF16) for u in heads]
    yield

    a16 = [a[u].astype(BF16) for u in heads]
    yk = [_dot(a16[u], a16[u]) for u in heads]
    sk = [-a[u] for u in heads]
    nlev = int(np.log2(CHUNK)) - 1
    for lev in range(nlev):
        yield
        yk16 = [yk[u].astype(BF16) for u in heads]
        if lev < nlev - 1:
            zz = [_dot(jnp.concatenate([sk[u], yk[u]], axis=0).astype(BF16), yk16[u]) for u in heads]
            sk = [sk[u] + yk[u] + zz[u][:CHUNK] for u in heads]
            yk = [zz[u][CHUNK:] for u in heads]
        else:
            sk = [sk[u] + yk[u] + _dot(sk[u].astype(BF16), yk16[u]) for u in heads]

    yield
    rhs = [hsl(va, u) * beta[u] - r1[u][:CHUNK] for u in heads]
    v_new = [(rhs[u] + _dot(sk[u].astype(BF16), rhs[u].astype(BF16))).astype(BF16) for u in heads]
    yield
    o = [r1[u][CHUNK:] + _dot(qk_m[u], v_new[u]) for u in heads]
    for u in heads:
        k_dec = kn[u] * jnp.exp(g_col[u][CHUNK - 1:CHUNK, :] - g_col[u])
        s_ref[u] = state[u] * eg[u][CHUNK - 1:CHUNK, :] + _dot_tn(k_dec.astype(BF16), v_new[u])
    yield

    on = jnp.concatenate([_norm_rows(o[u], gn_ref[...]) for u in heads], axis=1)
    o_ref[...] = (on * _silu(z_ref[...].astype(F32))).astype(o_ref.dtype)


def _gdn(p1, yp, sel, oneh, conv_w, gnorm, shift3, *, seq, hg):
    b, lp, _ = p1.shape
    nc = lp // CHUNK
    gw = hg * HEAD_DIM
    ng = HEADS // hg
    col = lambda base: (lambda bi, gi, ci: (bi, ci, base * LANES // gw + gi))
    wcol = lambda base: (lambda bi, gi, ci: (0, base * LANES // gw + gi))
    const = lambda bi, gi, ci: (0, 0)
    return pl.pallas_call(
        functools.partial(_gdn_kernel, hg=hg),
        grid=(b, ng, nc),
        in_specs=[
            pl.BlockSpec((None, CHUNK, gw), col(COL_GDN_Q)),
            pl.BlockSpec((None, CHUNK, gw), col(COL_GDN_K)),
            pl.BlockSpec((None, CHUNK, gw), col(COL_GDN_V)),
            pl.BlockSpec((None, CHUNK, gw), col(COL_Z)),
            pl.BlockSpec((None, CHUNK, GQ_PIECES * LANES), lambda bi, gi, ci: (bi, ci, 0)),
            pl.BlockSpec((GQ_PIECES * LANES, hg * N_GQ * LANES), lambda bi, gi, ci: (0, gi)),
            pl.BlockSpec((hg * ROW_REP, GQ_PIECES * LANES), lambda bi, gi, ci: (gi, 0)),
            pl.BlockSpec((CONV_K, gw), wcol(COL_GDN_Q)),
            pl.BlockSpec((CONV_K, gw), wcol(COL_GDN_K)),
            pl.BlockSpec((CONV_K, gw), wcol(COL_GDN_V)),
            pl.BlockSpec((1, HEAD_DIM), const),
            pl.BlockSpec((3 * CHUNK, 2 * CHUNK), const),
        ],
        out_specs=pl.BlockSpec((None, CHUNK, gw),
                               lambda bi, gi, ci: (bi, jnp.maximum(ci - 1, 0), gi)),
        out_shape=jax.ShapeDtypeStruct((b, seq, HEADS * HEAD_DIM), BF16),
        scratch_shapes=[
            pltpu.VMEM((hg, HEAD_DIM, HEAD_DIM), F32),
            pltpu.VMEM((3, CHUNK, gw), BF16),
        ],
        compiler_params=_params("parallel", "parallel", "arbitrary"),
        name="gdn",
    )(p1, p1, p1, p1, yp, sel, oneh, conv_w, conv_w, conv_w, gnorm, shift3)


def _fox_kernel(*refs, tq, lp, n_cast):
    q_ref, k_ref, v_ref, cp_ref, crow_ref, bound_ref = refs[:6]
    o_ref = refs[6 + n_cast]
    vaug_ref = refs[-1]
    h = pl.program_id(1)
    i = pl.program_id(2)

    for w_ref, w16_ref in zip(refs[6:6 + n_cast], refs[7 + n_cast:7 + 2 * n_cast]):
        w16_ref[...] = w_ref[...].astype(w16_ref.dtype)

    @pl.when(i == 0)
    def _():
        _fox_stage_v(v_ref, vaug_ref, tq=tq, lp=lp)

    bound = bound_ref[0, 0]

    @pl.when(bound <= FOX_SAFE_BOUND)
    def _():
        for ii in range(pl.cdiv(lp - CHUNK, tq)):
            @pl.when(i == ii)
            def _(ii=ii):
                _run(_fox_tile(ii, h, bound, q_ref, k_ref, cp_ref, crow_ref, vaug_ref, o_ref,
                               tq=tq, key_chunk=FOX_KEY_CHUNK))

    @pl.when(bound > FOX_SAFE_BOUND)
    def _():
        _fox_tile_online(i, h, q_ref, k_ref, cp_ref, crow_ref, vaug_ref, o_ref, tq=tq)


def _fox_stage_v(v_ref, vaug_ref, *, tq, lp):
    def stage(rows, n):
        vaug_ref[rows, 0:HEAD_DIM] = v_ref[rows, :]
        vaug_ref[rows, HEAD_DIM:2 * HEAD_DIM] = jnp.ones((n, HEAD_DIM), BF16)

    def body(r, c):
        stage(pl.ds(pl.multiple_of(r * tq, 128), tq), tq)
        return c
    lax.fori_loop(0, lp // tq, body, 0)
    if lp % tq:
        stage(pl.ds(lp - lp % tq, lp % tq), lp % tq)


def _fox_scores(h, q16, k_ref, crow_ref):
    def c_k(krows):
        group = pl.multiple_of(SM_F + lax.shift_left(lax.shift_right_logical(h, 3), 3), 8)
        rows8 = crow_ref[pl.ds(group, F32_SUBLANES), krows]
        sub = lax.broadcasted_iota(jnp.int32, rows8.shape, 0)
        return jnp.sum(jnp.where(sub == (h & 7), rows8, 0.0), axis=0, keepdims=True)

    def scores(rows, bias_q, krows, n):
        s = _dot_nt(q16[rows], k_ref[krows, :])
        return s + pltpu.repeat(bias_q[rows], n // LANES, 1) - c_k(krows)
    return scores


def _fox_query_side(h, rows_q, q_ref, cp_ref):
    piece_lane = lax.broadcasted_iota(jnp.int32, (3 * LANES, LANES), 0) & (LANES - 1)
    pick = jnp.where(piece_lane == SM_F + h, 1.0, 0.0).astype(BF16)
    return q_ref[rows_q, :], _dot(cp_ref[rows_q, :], pick)


def _fox_finish(o_ref, acc, rows):
    o_ref[rows, :] = (acc[:, :HEAD_DIM] / acc[:, HEAD_DIM:]).astype(o_ref.dtype)


def _fox_tile(ii, h, bound, q_ref, k_ref, cp_ref, crow_ref, vaug_ref, o_ref, *, tq, key_chunk):
    q0 = CHUNK + ii * tq
    q16, c_q = _fox_query_side(h, slice(q0, q0 + tq), q_ref, cp_ref)
    scores = _fox_scores(h, q16, k_ref, crow_ref)
    lane = lax.broadcasted_iota(jnp.int32, (tq, LANES), 1)
    bias_q = c_q - bound

    def probs(rows, krows, n, mask=None):
        s = scores(rows, bias_q, krows, n)
        if mask is not None:
            s = jnp.where(mask, s, NEG_BIG)
        return jnp.exp2(s).astype(BF16)

    every = slice(0, tq)
    p = probs(every, slice(0, CHUNK), CHUNK, lane >= FRONT_PAD)
    acc = _dot(p, vaug_ref[0:CHUNK, :])
    yield
    for start in range(CHUNK, q0, key_chunk):
        kr = slice(start, start + key_chunk)
        acc = acc + _dot(probs(every, kr, key_chunk), vaug_ref[kr, :])
        yield
    for r in range(tq // FOX_DIAG_PIECE):
        rows = slice(r * FOX_DIAG_PIECE, (r + 1) * FOX_DIAG_PIECE)
        nk = (r + 1) * FOX_DIAG_PIECE
        keys = slice(q0, q0 + nk)
        qpos = lax.broadcasted_iota(jnp.int32, (FOX_DIAG_PIECE, nk), 0) + r * FOX_DIAG_PIECE
        kpos = lax.broadcasted_iota(jnp.int32, (FOX_DIAG_PIECE, nk), 1)
        p = probs(rows, keys, nk, kpos <= qpos)
        _fox_finish(o_ref, acc[rows] + _dot(p, vaug_ref[keys, :]), rows)
        yield


def _fox_tile_online(i, h, q_ref, k_ref, cp_ref, crow_ref, vaug_ref, o_ref, *, tq):
    q0 = pl.multiple_of(CHUNK + i * tq, 128)
    q16, c_q = _fox_query_side(h, pl.ds(q0, tq), q_ref, cp_ref)
    scores = _fox_scores(h, q16, k_ref, crow_ref)
    lane = lax.broadcasted_iota(jnp.int32, (tq, LANES), 1)
    qpos = q0 + lax.broadcasted_iota(jnp.int32, (tq, LANES), 0)

    def body(c, carry):
        m, acc = carry
        krows = pl.ds(pl.multiple_of(c * CHUNK, CHUNK), CHUNK)
        kpos = c * CHUNK + lane
        s = scores(slice(0, tq), c_q, krows, CHUNK)
        s = jnp.where(kpos >= FRONT_PAD, jnp.where(kpos <= qpos, s, NEG_BIG), NEG_BIG)
        m_new = jnp.maximum(m, jnp.max(s, axis=-1, keepdims=True))
        p = jnp.exp2(s - m_new)
        acc = jnp.exp2(m - m_new) * acc + _dot(p.astype(BF16), vaug_ref[krows, :])
        return m_new, acc

    init = (jnp.full((tq, 1), NEG_BIG, F32), jnp.zeros((tq, 2 * HEAD_DIM), F32))
    _, acc = lax.fori_loop(0, (q0 + tq) // CHUNK, body, init)
    _fox_finish(o_ref, acc, slice(0, tq))


def _fox(p1, cp, crow, bound, weights, wt, wt_rows, *, seq, tq):
    b, lp, _ = p1.shape
    nq = seq // tq
    nsteps = b * HEADS * nq
    colmap = lambda base: (lambda bi, hi, qi: (bi, 0, base + hi))
    step = lambda bi, hi, qi: (bi * HEADS + hi) * nq + qi

    cast_in, cast_out, cast_shapes = [], [], []
    for w in weights:
        _, r, c = w.shape
        rp = r // nsteps
        assert r % nsteps == 0 and rp % 16 == 0
        cast_in.append(pl.BlockSpec((None, rp, c), lambda bi, hi, qi: (0, step(bi, hi, qi), 0)))
        cast_out.append(pl.BlockSpec((rp, c), lambda bi, hi, qi: (step(bi, hi, qi), 0)))
        cast_shapes.append(jax.ShapeDtypeStruct((r, c), BF16))
    start, n = wt_rows
    rp = n // nsteps
    assert n % nsteps == 0 and rp % 16 == 0 and start % F32_SUBLANES == 0
    cast_in.append(_element_rows(rp, wt.shape[2], lambda bi, hi, qi: (
        start // F32_SUBLANES + step(bi, hi, qi) * (rp // F32_SUBLANES))))
    cast_out.append(pl.BlockSpec((rp, wt.shape[2]), lambda bi, hi, qi: (step(bi, hi, qi), 0)))
    cast_shapes.append(jax.ShapeDtypeStruct((n, wt.shape[2]), BF16))

    outs = pl.pallas_call(
        functools.partial(_fox_kernel, tq=tq, lp=lp, n_cast=len(cast_in)),
        grid=(b, HEADS, nq),
        in_specs=[
            pl.BlockSpec((None, lp, HEAD_DIM), colmap(COL_FOX_Q)),
            pl.BlockSpec((None, lp, HEAD_DIM), colmap(COL_FOX_K)),
            pl.BlockSpec((None, lp, HEAD_DIM), colmap(COL_FOX_V)),
            pl.BlockSpec((None, lp, 3 * LANES), lambda bi, hi, qi: (bi, 0, 0)),
            pl.BlockSpec((None, LANES, lp), lambda bi, hi, qi: (bi, 0, 0)),
            pl.BlockSpec(memory_space=pltpu.SMEM),
        ] + cast_in,
        out_specs=[pl.BlockSpec((None, tq, HEAD_DIM), lambda bi, hi, qi: (bi, qi, hi))] + cast_out,
        out_shape=[jax.ShapeDtypeStruct((b, seq, HEADS * HEAD_DIM), BF16)] + cast_shapes,
        scratch_shapes=[pltpu.VMEM((lp, 2 * HEAD_DIM), BF16)],
        compiler_params=_params("parallel", "parallel", "arbitrary"),
        name="fox",
    )(p1, p1, p1, cp, crow, bound, *weights, wt)
    return outs[0], outs[1:]


N_GDN_IN = 12
MIX_KEY_CHUNK = 512


def _mix_kernel(*refs, tq, lp, nq, n_cast):
    gdn_in = refs[:N_GDN_IN]
    fq_ref, fk_ref, fv_ref, cp_ref, crow_ref, bound_ref = refs[N_GDN_IN:N_GDN_IN + 6]
    first_out = N_GDN_IN + 6 + n_cast
    ya_ref, yb_ref = refs[first_out:first_out + 2]
    s_ref, prev_ref, vaug_ref = refs[-3:]
    c = pl.program_id(1)
    t = jnp.maximum(c - 1, 0)
    h = t // nq
    i = t - h * nq

    for w_ref, w16_ref in zip(refs[N_GDN_IN + 6:first_out], refs[first_out + 2:first_out + 2 + n_cast]):
        w16_ref[...] = w_ref[...].astype(w16_ref.dtype)

    def gdn_step():
        return _gdn_step(*gdn_in, ya_ref, s_ref, prev_ref, hg=HEADS)

    @pl.when(c == 0)
    def _():
        s_ref[...] = jnp.zeros_like(s_ref)
        prev_ref[...] = jnp.zeros_like(prev_ref)
        _run(gdn_step())

    bound = bound_ref[0, 0]
    for ii in range(nq):
        @pl.when(jnp.logical_and(c >= 1, i == ii))
        def _(ii=ii):
            if ii == 0:
                _fox_stage_v(fv_ref, vaug_ref, tq=tq, lp=lp)
            _run(gdn_step(), _fox_tile(ii, h, bound, fq_ref, fk_ref, cp_ref, crow_ref, vaug_ref,
                                       yb_ref, tq=tq, key_chunk=MIX_KEY_CHUNK))


def _mix(p1, yp, sel, oneh, conv_w, gnorm, shift3, cp, crow, bound, weights, wt, wt_rows, *, seq, tq):
    b, lp, _ = p1.shape
    nc = lp // CHUNK
    nq = seq // tq
    assert nc - 1 == HEADS * nq
    gw = HEADS * HEAD_DIM
    tile = lambda c: jnp.maximum(c - 1, 0)
    head = lambda c: tile(c) // nq
    col = lambda base: (lambda bi, c: (bi, c, base * LANES // gw))
    wcol = lambda base: (lambda bi, c: (0, base * LANES // gw))
    fcol = lambda base: (lambda bi, c: (bi, 0, base + head(c)))
    const = lambda bi, c: (0, 0)
    slab = lambda bi, c: bi * (nc - 1) + tile(c)
    nslabs = b * (nc - 1)

    cast_in, cast_out, cast_shapes = [], [], []
    for w in weights:
        _, r, cc = w.shape
        rp = r // nslabs
        assert r % nslabs == 0 and rp % 16 == 0
        cast_in.append(pl.BlockSpec((None, rp, cc), lambda bi, c: (0, slab(bi, c), 0)))
        cast_out.append(pl.BlockSpec((rp, cc), lambda bi, c: (slab(bi, c), 0)))
        cast_shapes.append(jax.ShapeDtypeStruct((r, cc), BF16))
    start, n = wt_rows
    rp = n // nslabs
    assert n % nslabs == 0 and rp % 16 == 0 and start % F32_SUBLANES == 0
    cast_in.append(_element_rows(rp, wt.shape[2], lambda bi, c: (
        start // F32_SUBLANES + slab(bi, c) * (rp // F32_SUBLANES))))
    cast_out.append(pl.BlockSpec((rp, wt.shape[2]), lambda bi, c: (slab(bi, c), 0)))
    cast_shapes.append(jax.ShapeDtypeStruct((n, wt.shape[2]), BF16))

    outs = pl.pallas_call(
        functools.partial(_mix_kernel, tq=tq, lp=lp, nq=nq, n_cast=len(cast_in)),
        grid=(b, nc),
        in_specs=[
            pl.BlockSpec((None, CHUNK, gw), col(COL_GDN_Q)),
            pl.BlockSpec((None, CHUNK, gw), col(COL_GDN_K)),
            pl.BlockSpec((None, CHUNK, gw), col(COL_GDN_V)),
            pl.BlockSpec((None, CHUNK, gw), col(COL_Z)),
            pl.BlockSpec((None, CHUNK, GQ_PIECES * LANES), lambda bi, c: (bi, c, 0)),
            pl.BlockSpec((GQ_PIECES * LANES, HEADS * N_GQ * LANES), const),
            pl.BlockSpec((HEADS * ROW_REP, GQ_PIECES * LANES), const),
            pl.BlockSpec((CONV_K, gw), wcol(COL_GDN_Q)),
            pl.BlockSpec((CONV_K, gw), wcol(COL_GDN_K)),
            pl.BlockSpec((CONV_K, gw), wcol(COL_GDN_V)),
            pl.BlockSpec((1, HEAD_DIM), const),
            pl.BlockSpec((3 * CHUNK, 2 * CHUNK), const),
            pl.BlockSpec((None, lp, HEAD_DIM), fcol(COL_FOX_Q)),
            pl.BlockSpec((None, lp, HEAD_DIM), fcol(COL_FOX_K)),
            pl.BlockSpec((None, lp, HEAD_DIM), fcol(COL_FOX_V)),
            pl.BlockSpec((None, lp, 3 * LANES), lambda bi, c: (bi, 0, 0)),
            pl.BlockSpec((None, LANES, lp), lambda bi, c: (bi, 0, 0)),
            pl.BlockSpec(memory_space=pltpu.SMEM),
        ] + cast_in,
        out_specs=[
            pl.BlockSpec((None, CHUNK, gw), lambda bi, c: (bi, tile(c), 0)),
            pl.BlockSpec((None, tq, HEAD_DIM), lambda bi, c: (bi, tile(c) - head(c) * nq, head(c))),
        ] + cast_out,
        out_shape=[jax.ShapeDtypeStruct((b, seq, gw), BF16)] * 2 + cast_shapes,
        scratch_shapes=[
            pltpu.VMEM((HEADS, HEAD_DIM, HEAD_DIM), F32),
            pltpu.VMEM((3, CHUNK, gw), BF16),
            pltpu.VMEM((lp, 2 * HEAD_DIM), BF16),
        ],
        compiler_params=_params("parallel", "arbitrary"),
        name="mix",
    )(p1, p1, p1, p1, yp, sel, oneh, conv_w, conv_w, conv_w, gnorm, shift3,
      p1, p1, p1, cp, crow, bound, *weights, wt)
    return outs[0], outs[1], outs[2:]


def _merge_kernel(x_ref, g_ref, ya_ref, yb_ref, wga_ref, wgb_ref, wa_ref, wb_ref, wo_ref,
                  o_ref, u_ref):
    j = pl.program_id(1)

    @pl.when(j == 0)
    def _():
        x = x_ref[...]
        u_ref[...] = _norm_rows(x, g_ref[...]).astype(BF16)
        o_ref[...] = x

    u = u_ref[...]
    ga = _dot_nt(u, wga_ref[...])
    gb = _dot_nt(u, wgb_ref[...])
    pa = _dot(ya_ref[...], wa_ref[...])
    pb = _dot(yb_ref[...], wb_ref[...])
    mix = _sigmoid(ga) * pa + _sigmoid(gb) * pb
    o_ref[...] += _dot(mix.astype(BF16), wo_ref[...])


def _merge(x2d, g, ya, yb, wg_t, wa, wb, wo, *, tm, tj):
    m, d = x2d.shape
    rowblk = lambda i, j: (i, 0)
    colblk = lambda i, j: (0, j)
    return pl.pallas_call(
        _merge_kernel,
        grid=(m // tm, d // tj),
        in_specs=[
            pl.BlockSpec((tm, d), rowblk),
            pl.BlockSpec((1, d), lambda i, j: (0, 0)),
            pl.BlockSpec((tm, d), rowblk),
            pl.BlockSpec((tm, d), rowblk),
            pl.BlockSpec((tj, d), lambda i, j: (j, 0)),
            pl.BlockSpec((tj, d), lambda i, j: (j + d // tj, 0)),
            pl.BlockSpec((d, tj), colblk),
            pl.BlockSpec((d, tj), colblk),
            pl.BlockSpec((tj, d), lambda i, j: (j, 0)),
        ],
        out_specs=pl.BlockSpec((tm, d), rowblk),
        out_shape=jax.ShapeDtypeStruct((m, d), F32),
        scratch_shapes=[pltpu.VMEM((tm, d), BF16)],
        compiler_params=_params("parallel", "arbitrary"),
        name="merge",
    )(x2d, g, ya, yb, wg_t, wg_t, wa, wb, wo)


def _mlp_kernel(h_ref, g_ref, wu_ref, wd_ref, gf_ref, o_ref, u_ref):
    f = pl.program_id(1)

    @pl.when(f == 0)
    def _():
        x = h_ref[...]
        u_ref[...] = _norm_rows(x, g_ref[...]).astype(BF16)
        o_ref[...] = x

    a = jnp.maximum(_dot(u_ref[...], wu_ref[...]), 0.0)
    o_ref[...] += _dot((a * a).astype(BF16), wd_ref[...])

    @pl.when(f == pl.num_programs(1) - 1)
    def _():
        o_ref[...] = _norm_rows(o_ref[...], gf_ref[...])


def _mlp(h2d, g, wu, wd, gf, *, tm, tf):
    m, d = h2d.shape
    dff = wu.shape[1]
    rowblk = lambda i, f: (i, 0)
    return pl.pallas_call(
        _mlp_kernel,
        grid=(m // tm, dff // tf),
        in_specs=[
            pl.BlockSpec((tm, d), rowblk),
            pl.BlockSpec((1, d), lambda i, f: (0, 0)),
            pl.BlockSpec((d, tf), lambda i, f: (0, f)),
            pl.BlockSpec((tf, d), lambda i, f: (f, 0)),
            pl.BlockSpec((1, d), lambda i, f: (0, 0)),
        ],
        out_specs=pl.BlockSpec((tm, d), rowblk),
        out_shape=jax.ShapeDtypeStruct((m, d), F32),
        scratch_shapes=[pltpu.VMEM((tm, d), BF16)],
        compiler_params=_params("parallel", "arbitrary"),
        name="mlp",
    )(h2d, g, wu, wd, gf)


def _constants():
    r = np.arange(128)
    ltri = (r[:, None] >= r[None, :]).astype(np.float32)
    ltri3 = np.concatenate([ltri, ltri, ltri], axis=1)

    shift3 = np.zeros((3 * CHUNK, 2 * CHUNK), np.float32)
    for s in range(1, CONV_K):
        shift3[(s - 1) * CHUNK + r, CHUNK + r - s] = 1.0

    src_lane = np.zeros((HEADS * N_GQ * LANES,), np.int64)
    for h in range(HEADS):
        for q in range(N_GQ):
            src_lane[(h * N_GQ + q) * LANES:(h * N_GQ + q + 1) * LANES] = 16 * q + h
    piece_lane = np.arange(GQ_PIECES * LANES) % LANES
    sel = (piece_lane[:, None] == src_lane[None, :]).astype(np.float32)
    oneh = (np.repeat(16 * GQ_G + np.arange(HEADS), ROW_REP)[:, None]
            == piece_lane[None, :]).astype(np.float32)

    as_bf = lambda a: jnp.asarray(a, BF16)
    return dict(ltri3=as_bf(ltri3), shift3=as_bf(shift3), sel=as_bf(sel), oneh=as_bf(oneh))


def _score_bound(gq, gk):
    gmax = jnp.max(jnp.abs(gq.astype(F32))) * jnp.max(jnp.abs(gk.astype(F32)))
    return (gmax * (HEAD_DIM ** 0.5 * LOG2E * 1.02)).astype(BF16).astype(F32).reshape(1, 1)


def _row128(vec, offset):
    return jnp.zeros((1, LANES), F32).at[0, offset:offset + HEADS].set(vec.astype(F32))


def _pick_rows_tile(total, target):
    best = 128
    for t in range(128, target + 1, 128):
        if total % t == 0:
            best = t
    return best


def _layer(x, meta_tokens, mix_norm_g, w_in, conv_w, a_log, dt_bias, gdn_norm_g, w_o_gdn,
           fox_q_norm_g, fox_k_norm_g, fox_f_bias, w_o_fox, w_out, mlp_norm_g, w_up, w_down,
           final_norm_g):
    b, seq, d = x.shape
    lp = CHUNK + seq
    qk = HEADS * HEAD_DIM

    o_z = 3 * qk
    o_b = o_z + qk
    o_a = o_b + HEADS
    o_fq = o_a + HEADS
    o_f = o_fq + 3 * qk
    o_ga = o_f + HEADS
    o_gb = o_ga + d

    wt = jnp.swapaxes(w_in, 1, 2)
    row = lambda v: v.reshape(1, -1).astype(F32)
    cst = _constants()

    tm_in = _pick_rows_tile(lp, 1408)
    hn = _prenorm(x, meta_tokens.astype(x.dtype), row(mix_norm_g), tile=tm_in)
    p1, ps = _inproj(hn.reshape(b * lp, d), wt, o_b, o_fq, o_f - o_fq,
                     ((o_b, o_fq - o_b), (o_f, o_ga - o_f)),
                     row(fox_q_norm_g), row(fox_k_norm_g), HEAD_DIM ** -0.5 * LOG2E,
                     tm=tm_in, tn=1024)
    p1 = p1.reshape(b, lp, P1_COLS)
    ps = ps.reshape(b, lp, LANES)

    yp, cp, crow = _gate_prep(ps, _row128(a_log, SM_A), _row128(dt_bias, SM_A),
                              _row128(fox_f_bias, SM_F), cst["ltri3"])
    bound = _score_bound(fox_q_norm_g, fox_k_norm_g)
    later_weights = [w_o_gdn, w_o_fox, w_out, w_up, w_down]
    gdn_args = (p1, yp, cst["sel"], cst["oneh"], conv_w.astype(F32), row(gdn_norm_g), cst["shift3"])
    tq = min(CHUNK * HEADS, seq)

    def mixers_separate():
        ya = _gdn(*gdn_args, seq=seq, hg=HEADS)
        yb, w16 = _fox(p1, cp, crow, bound, later_weights, wt, (o_ga, 2 * d), seq=seq, tq=tq)
        return ya, yb, tuple(w16)

    def mixers_fused():
        ya, yb, w16 = _mix(*gdn_args, cp, crow, bound, later_weights, wt, (o_ga, 2 * d),
                           seq=seq, tq=tq)
        return ya, yb, tuple(w16)

    if lp // CHUNK - 1 == HEADS * (seq // tq):
        ya, yb, w16 = lax.cond(bound[0, 0] <= FOX_SAFE_BOUND, mixers_fused, mixers_separate)
    else:
        ya, yb, w16 = mixers_separate()
    wog, wof, wout, wup, wdown, wg_t = w16

    x2d = x.reshape(b * seq, d)
    h1 = _merge(x2d, row(mix_norm_g), ya.reshape(b * seq, qk), yb.reshape(b * seq, qk),
                wg_t, wog, wof, wout, tm=512, tj=512)
    out = _mlp(h1, row(mlp_norm_g), wup, wdown, row(final_norm_g), tm=512, tf=1024)
    return out.reshape(b, seq, d)


def kernel(x, meta_tokens, mix_norm_g, w_in, conv_w, a_log, dt_bias, gdn_norm_g, w_o_gdn,
           fox_q_norm_g, fox_k_norm_g, fox_f_bias, w_o_fox, w_out, mlp_norm_g, w_up, w_down,
           final_norm_g):
    assert w_in.shape[0] == 1, "single-layer block"
    return _layer(x, meta_tokens, mix_norm_g[0], w_in, conv_w[0], a_log[0], dt_bias[0],
                  gdn_norm_g[0], w_o_gdn, fox_q_norm_g[0], fox_k_norm_g[0], fox_f_bias[0],
                  w_o_fox, w_out, mlp_norm_g[0], w_up, w_down, final_norm_g)
```

```python
import functools
import math

import numpy as np
import jax
import jax.numpy as jnp
from jax import lax
from jax.experimental import pallas as pl
from jax.experimental.pallas import tpu as pltpu

F32 = jnp.float32
BF16 = jnp.bfloat16

NORM_EPS = 1e-6
N_META = 16
HEADS = 16
HEAD_DIM = 128
CONV_K = 4
LANES = 128
F32_SUBLANES = 8
CHUNK = 128
FRONT_PAD = CHUNK - N_META
NEG_BIG = -1e30
LOG2E = math.log2(math.e)
VMEM_LIMIT = 56 * 1024 * 1024

COL_GDN_Q, COL_GDN_K, COL_GDN_V, COL_Z = 0, 16, 32, 48
COL_FOX_Q, COL_FOX_K, COL_FOX_V = 64, 80, 96
P1_COLS = 112 * LANES
SM_B, SM_A, SM_F = 0, 16, 32
GQ_G, GQ_BETA = range(2)
N_GQ = 2
GQ_PIECES = 2
ROW_REP = 8
FOX_SAFE_BOUND = 40.0
FOX_KEY_CHUNK = 1024
FOX_DIAG_PIECE = 512


def _params(*sem):
    return pltpu.CompilerParams(dimension_semantics=sem, vmem_limit_bytes=VMEM_LIMIT)


def _norm_rows(x, g):
    ms = jnp.mean(x * x, axis=-1, keepdims=True)
    return x * lax.rsqrt(ms + NORM_EPS) * g


def _split3(x):
    hi = x.astype(BF16)
    r = x - hi.astype(F32)
    mid = r.astype(BF16)
    lo = (r - mid.astype(F32)).astype(BF16)
    return hi, mid, lo


def _dot(a, b):
    return jnp.dot(a, b, preferred_element_type=F32)


def _dot_nt(a, b):
    return lax.dot_general(a, b, (((1,), (1,)), ((), ())), preferred_element_type=F32)


def _dot_tn(a, b):
    return lax.dot_general(a, b, (((0,), (0,)), ((), ())), preferred_element_type=F32)


def _sigmoid(x):
    return 1.0 / (1.0 + jnp.exp(-x))


def _silu(x):
    return x * _sigmoid(x)


def _softplus(x):
    return jnp.maximum(x, 0.0) + jnp.log1p(jnp.exp(-jnp.abs(x)))


def _prenorm_kernel(x_ref, meta_ref, g_ref, o_ref, *, tile):
    t = pl.program_id(1)

    def norm_chunks(first_out, n, shift):
        def body(c, carry):
            dst = pl.ds(pl.multiple_of(first_out + c * CHUNK, CHUNK), CHUNK)
            src = pl.ds(pl.multiple_of(first_out - shift + c * CHUNK, CHUNK), CHUNK)
            o_ref[dst, :] = _norm_rows(x_ref[src, :], g_ref[...]).astype(o_ref.dtype)
            return carry
        lax.fori_loop(0, n, body, 0)

    @pl.when(t == 0)
    def _():
        o_ref[0:FRONT_PAD, :] = jnp.zeros((FRONT_PAD, o_ref.shape[1]), o_ref.dtype)
        o_ref[FRONT_PAD:CHUNK, :] = _norm_rows(meta_ref[...], g_ref[...]).astype(o_ref.dtype)
        norm_chunks(CHUNK, tile // CHUNK - 1, CHUNK)

    @pl.when(t > 0)
    def _():
        norm_chunks(0, tile // CHUNK, 0)


def _prenorm(x, meta_tokens, g, *, tile):
    b, seq, d = x.shape
    lp = CHUNK + seq
    sub = F32_SUBLANES
    assert lp % tile == 0 and tile % CHUNK == 0 and CHUNK % sub == 0
    x_rows = lambda bi, t: sub * jnp.maximum(t * (tile // sub) - CHUNK // sub, 0)
    return pl.pallas_call(
        functools.partial(_prenorm_kernel, tile=tile),
        grid=(b, lp // tile),
        in_specs=[
            pl.BlockSpec((None, pl.Element(tile), pl.Element(d)),
                         lambda bi, t: (bi, x_rows(bi, t), 0)),
            pl.BlockSpec((N_META, d), lambda bi, t: (0, 0)),
            pl.BlockSpec((1, d), lambda bi, t: (0, 0)),
        ],
        out_specs=pl.BlockSpec((None, tile, d), lambda bi, t: (bi, t, 0)),
        out_shape=jax.ShapeDtypeStruct((b, lp, d), BF16),
        compiler_params=_params("parallel", "arbitrary"),
        name="prenorm",
    )(x, meta_tokens, g)


def _inproj_kernel(x_ref, w_ref, wsa_ref, wsb_ref, gq_ref, gk_ref, o_ref, os_ref, *,
                   q_tiles, k_tiles, q_scale):
    j = pl.program_id(1)

    @pl.when(j == 0)
    def _():
        pad = LANES - wsa_ref.shape[0] - wsb_ref.shape[0]
        ws = jnp.concatenate([wsa_ref[...], wsb_ref[...], jnp.zeros((pad, wsa_ref.shape[1]), F32)],
                             axis=0)
        os_ref[...] = _dot_nt(x_ref[...], ws.astype(BF16))

    def project():
        return _dot_nt(x_ref[...], w_ref[...].astype(BF16))

    def project_head_norm(g_ref, scale):
        acc = project()
        for h in range(o_ref.shape[1] // HEAD_DIM):
            cols = slice(h * HEAD_DIM, (h + 1) * HEAD_DIM)
            o_ref[:, cols] = (_norm_rows(acc[:, cols], g_ref[...]) * scale).astype(o_ref.dtype)

    is_q = jnp.logical_and(j >= q_tiles[0], j < q_tiles[1])
    is_k = jnp.logical_and(j >= k_tiles[0], j < k_tiles[1])

    @pl.when(is_q)
    def _():
        project_head_norm(gq_ref, q_scale)

    @pl.when(is_k)
    def _():
        project_head_norm(gk_ref, 1.0)

    @pl.when(jnp.logical_not(jnp.logical_or(is_q, is_k)))
    def _():
        o_ref[...] = project().astype(o_ref.dtype)


def _element_rows(tile, d, start_tiles):
    return pl.BlockSpec((None, pl.Element(tile), pl.Element(d)),
                        lambda *ids: (0, F32_SUBLANES * start_tiles(*ids), 0))


def _inproj(hn, wt, n_a, b_start, n_b, small, gq, gk, q_scale, *, tm, tn):
    m, d = hn.shape
    na, nb = n_a // tn, n_b // tn
    assert nb % 3 == 0
    q_tiles = (na, na + nb // 3)
    k_tiles = (na + nb // 3, na + 2 * nb // 3)
    (s0, c0), (s1, c1) = small
    sub = F32_SUBLANES
    assert all(v % sub == 0 for v in (tn, b_start - n_a, s0, s1))

    def w_rows(i, j):
        return j * (tn // sub) + jnp.where(j >= na, (b_start - n_a) // sub, 0)

    return pl.pallas_call(
        functools.partial(_inproj_kernel, q_tiles=q_tiles, k_tiles=k_tiles, q_scale=q_scale),
        grid=(m // tm, na + nb),
        in_specs=[
            pl.BlockSpec((tm, d), lambda i, j: (i, 0)),
            _element_rows(tn, d, w_rows),
            _element_rows(c0, d, lambda i, j: s0 // sub),
            _element_rows(c1, d, lambda i, j: s1 // sub),
            pl.BlockSpec((1, HEAD_DIM), lambda i, j: (0, 0)),
            pl.BlockSpec((1, HEAD_DIM), lambda i, j: (0, 0)),
        ],
        out_specs=[
            pl.BlockSpec((tm, tn), lambda i, j: (i, j)),
            pl.BlockSpec((tm, LANES), lambda i, j: (i, 0)),
        ],
        out_shape=[
            jax.ShapeDtypeStruct((m, (na + nb) * tn), BF16),
            jax.ShapeDtypeStruct((m, LANES), F32),
        ],
        compiler_params=_params("parallel", "arbitrary"),
        name="inproj",
    )(hn, wt, wt, wt, gq, gk)


def _gate_prep_kernel(ps_ref, alog_ref, dtb_ref, fb_ref, ltri_ref, yp_ref, cp_ref, crow_ref, *, nblk):
    lane = lax.broadcasted_iota(jnp.int32, (CHUNK, LANES), 1)
    rowi = lax.broadcasted_iota(jnp.int32, (CHUNK, LANES), 0)

    def body(i, carry):
        rows = pl.ds(pl.multiple_of(i * CHUNK, CHUNK), CHUNK)
        x = ps_ref[rows, :]
        valid = (i * CHUNK + rowi) >= FRONT_PAD

        beta = jnp.where(valid, _sigmoid(x), 0.0)
        gstep = jnp.where(valid, -jnp.exp(alog_ref[...]) * _softplus(x + dtb_ref[...]), 0.0)
        gcum = _dot(ltri_ref[...], jnp.concatenate(_split3(gstep), axis=0))
        y = jnp.where(lane < 16 * (GQ_G + 1), pltpu.roll(gcum, LANES + 16 * GQ_G - SM_A, 1),
                      pltpu.roll(beta, 16 * GQ_BETA - SM_B, 1))
        yp_ref[rows, :] = jnp.concatenate(_split3(y)[:GQ_PIECES], axis=1)

        xf = x + fb_ref[...]
        ls = (jnp.minimum(xf, 0.0) - jnp.log1p(jnp.exp(-jnp.abs(xf)))) * LOG2E
        cum = _dot(ltri_ref[...], jnp.concatenate(_split3(ls), axis=0)) + carry
        cp_ref[rows, :] = jnp.concatenate(_split3(cum), axis=1)
        crow_ref[:, rows] = cum.T
        return cum[CHUNK - 1:CHUNK, :]

    lax.fori_loop(0, nblk, body, jnp.zeros((1, LANES), F32))


def _gate_prep(ps, alog_row, dtb_row, fb_row, ltri3):
    b, lp, _ = ps.shape
    const = lambda bi: (0, 0)
    return pl.pallas_call(
        functools.partial(_gate_prep_kernel, nblk=lp // CHUNK),
        grid=(b,),
        in_specs=[
            pl.BlockSpec((None, lp, LANES), lambda bi: (bi, 0, 0)),
            pl.BlockSpec((1, LANES), const),
            pl.BlockSpec((1, LANES), const),
            pl.BlockSpec((1, LANES), const),
            pl.BlockSpec((CHUNK, 3 * CHUNK), const),
        ],
        out_specs=[
            pl.BlockSpec((None, lp, GQ_PIECES * LANES), lambda bi: (bi, 0, 0)),
            pl.BlockSpec((None, lp, 3 * LANES), lambda bi: (bi, 0, 0)),
            pl.BlockSpec((None, LANES, lp), lambda bi: (bi, 0, 0)),
        ],
        out_shape=[
            jax.ShapeDtypeStruct((b, lp, GQ_PIECES * LANES), BF16),
            jax.ShapeDtypeStruct((b, lp, 3 * LANES), BF16),
            jax.ShapeDtypeStruct((b, LANES, lp), F32),
        ],
        compiler_params=_params("parallel"),
        name="gate_prep",
    )(ps, alog_row, dtb_row, fb_row, ltri3)


def _gdn_kernel(*refs, hg):
    s_ref, prev_ref = refs[-2:]

    @pl.when(pl.program_id(2) == 0)
    def _():
        s_ref[...] = jnp.zeros_like(s_ref)
        prev_ref[...] = jnp.zeros_like(prev_ref)

    _run(_gdn_step(*refs, hg=hg))


def _run(*stage_generators):
    live = list(stage_generators)
    while live:
        for g in list(live):
            if next(g, StopIteration) is StopIteration:
                live.remove(g)


def _gdn_step(q_ref, k_ref, v_ref, z_ref, yp_ref, sel_ref, oneh_ref, wq_ref, wk_ref, wv_ref,
              gn_ref, shift_ref, o_ref, s_ref, prev_ref, *, hg):
    row = lax.broadcasted_iota(jnp.int32, (CHUNK, CHUNK), 0)
    col = lax.broadcasted_iota(jnp.int32, (CHUNK, CHUNK), 1)
    tril = row >= col
    strict = row > col
    heads = range(hg)
    hsl = lambda a, u: a[:, u * HEAD_DIM:(u + 1) * HEAD_DIM]

    yp = yp_ref[...]
    gb = _dot(yp, sel_ref[...])
    g_rows = _dot_nt(oneh_ref[...], yp)
    gq = lambda u, q: gb[:, (u * N_GQ + q) * LANES:(u * N_GQ + q + 1) * LANES]
    g_col = [gq(u, GQ_G) for u in heads]
    beta = [gq(u, GQ_BETA) for u in heads]
    eg = [jnp.exp(g_col[u]) for u in heads]
    yield

    def conv_act(x_ref, w_ref, t):
        x16 = x_ref[...]
        x2 = jnp.concatenate([prev_ref[t], x16], axis=0)
        prev_ref[t] = x16
        sh = _dot(shift_ref[...], x2)
        w = w_ref[...]
        acc = x16.astype(F32) * w[CONV_K - 1:CONV_K, :]
        for s in range(1, CONV_K):
            acc = acc + sh[(s - 1) * CHUNK:s * CHUNK, :] * w[CONV_K - 1 - s:CONV_K - s, :]
        return _silu(acc)

    qa = conv_act(q_ref, wq_ref, 0)
    yield
    ka = conv_act(k_ref, wk_ref, 1)
    yield
    va = conv_act(v_ref, wv_ref, 2)
    yield

    def l2n(x):
        return x * lax.rsqrt(jnp.sum(x * x, axis=-1, keepdims=True) + NORM_EPS)

    qn = [l2n(hsl(qa, u)) * (HEAD_DIM ** -0.5) for u in heads]
    kn = [l2n(hsl(ka, u)) for u in heads]
    k16 = [kn[u].astype(BF16) for u in heads]
    yield

    state = [s_ref[u] for u in heads]
    kb = [kn[u] * beta[u] for u in heads]
    r1 = [_dot(jnp.concatenate([kb[u] * eg[u], qn[u] * eg[u]], axis=0).astype(BF16),
               state[u].astype(BF16)) for u in heads]
    yield

    kq = [_dot_nt(jnp.concatenate([kb[u], qn[u]], axis=0).astype(BF16), k16[u])
          for u in heads]
    ediff = [jnp.exp(jnp.minimum(g_col[u] - g_rows[u * ROW_REP:u * ROW_REP + 1, :], 0.0))
             for u in heads]
    a = [kq[u][:CHUNK] * jnp.where(strict, ediff[u], 0.0) for u in heads]
    qk_m = [(kq[u][CHUNK:] * jnp.where(tril, ediff[u], 0.0)).astype(BF16) for u in heads]
    yield

    a16 = [a[u].astype(BF16) for u in heads]
    yk = [_dot(a16[u], a16[u]) for u in heads]
    sk = [-a[u] for u in heads]
    nlev = int(np.log2(CHUNK)) - 1
    for lev in range(nlev):
        yield
        yk16 = [yk[u].astype(BF16) for u in heads]
        if lev < nlev - 1:
            zz = [_dot(jnp.concatenate([sk[u], yk[u]], axis=0).astype(BF16), yk16[u]) for u in heads]
            sk = [sk[u] + yk[u] + zz[u][:CHUNK] for u in heads]
            yk = [zz[u][CHUNK:] for u in heads]
        else:
            sk = [sk[u] + yk[u] + _dot(sk[u].astype(BF16), yk16[u]) for u in heads]

    yield
    rhs = [hsl(va, u) * beta[u] - r1[u][:CHUNK] for u in heads]
    v_new = [(rhs[u] + _dot(sk[u].astype(BF16), rhs[u].astype(BF16))).astype(BF16) for u in heads]
    yield
    o = [r1[u][CHUNK:] + _dot(qk_m[u], v_new[u]) for u in heads]
    for u in heads:
        k_dec = kn[u] * jnp.exp(g_col[u][CHUNK - 1:CHUNK, :] - g_col[u])
        s_ref[u] = state[u] * eg[u][CHUNK - 1:CHUNK, :] + _dot_tn(k_dec.astype(BF16), v_new[u])
    yield

    on = jnp.concatenate([_norm_rows(o[u], gn_ref[...]) for u in heads], axis=1)
    o_ref[...] = (on * _silu(z_ref[...].astype(F32))).astype(o_ref.dtype)


def _gdn(p1, yp, sel, oneh, conv_w, gnorm, shift3, *, seq, hg):
    b, lp, _ = p1.shape
    nc = lp // CHUNK
    gw = hg * HEAD_DIM
    ng = HEADS // hg
    col = lambda base: (lambda bi, gi, ci: (bi, ci, base * LANES // gw + gi))
    wcol = lambda base: (lambda bi, gi, ci: (0, base * LANES // gw + gi))
    const = lambda bi, gi, ci: (0, 0)
    return pl.pallas_call(
        functools.partial(_gdn_kernel, hg=hg),
        grid=(b, ng, nc),
        in_specs=[
            pl.BlockSpec((None, CHUNK, gw), col(COL_GDN_Q)),
            pl.BlockSpec((None, CHUNK, gw), col(COL_GDN_K)),
            pl.BlockSpec((None, CHUNK, gw), col(COL_GDN_V)),
            pl.BlockSpec((None, CHUNK, gw), col(COL_Z)),
            pl.BlockSpec((None, CHUNK, GQ_PIECES * LANES), lambda bi, gi, ci: (bi, ci, 0)),
            pl.BlockSpec((GQ_PIECES * LANES, hg * N_GQ * LANES), lambda bi, gi, ci: (0, gi)),
            pl.BlockSpec((hg * ROW_REP, GQ_PIECES * LANES), lambda bi, gi, ci: (gi, 0)),
            pl.BlockSpec((CONV_K, gw), wcol(COL_GDN_Q)),
            pl.BlockSpec((CONV_K, gw), wcol(COL_GDN_K)),
            pl.BlockSpec((CONV_K, gw), wcol(COL_GDN_V)),
            pl.BlockSpec((1, HEAD_DIM), const),
            pl.BlockSpec((3 * CHUNK, 2 * CHUNK), const),
        ],
        out_specs=pl.BlockSpec((None, CHUNK, gw),
                               lambda bi, gi, ci: (bi, jnp.maximum(ci - 1, 0), gi)),
        out_shape=jax.ShapeDtypeStruct((b, seq, HEADS * HEAD_DIM), BF16),
        scratch_shapes=[
            pltpu.VMEM((hg, HEAD_DIM, HEAD_DIM), F32),
            pltpu.VMEM((3, CHUNK, gw), BF16),
        ],
        compiler_params=_params("parallel", "parallel", "arbitrary"),
        name="gdn",
    )(p1, p1, p1, p1, yp, sel, oneh, conv_w, conv_w, conv_w, gnorm, shift3)


def _fox_kernel(*refs, tq, lp, n_cast):
    q_ref, k_ref, v_ref, cp_ref, crow_ref, bound_ref = refs[:6]
    o_ref = refs[6 + n_cast]
    vaug_ref = refs[-1]
    h = pl.program_id(1)
    i = pl.program_id(2)

    for w_ref, w16_ref in zip(refs[6:6 + n_cast], refs[7 + n_cast:7 + 2 * n_cast]):
        w16_ref[...] = w_ref[...].astype(w16_ref.dtype)

    @pl.when(i == 0)
    def _():
        _fox_stage_v(v_ref, vaug_ref, tq=tq, lp=lp)

    bound = bound_ref[0, 0]

    @pl.when(bound <= FOX_SAFE_BOUND)
    def _():
        for ii in range(pl.cdiv(lp - CHUNK, tq)):
            @pl.when(i == ii)
            def _(ii=ii):
                _run(_fox_tile(ii, h, bound, q_ref, k_ref, cp_ref, crow_ref, vaug_ref, o_ref,
                               tq=tq, key_chunk=FOX_KEY_CHUNK))

    @pl.when(bound > FOX_SAFE_BOUND)
    def _():
        _fox_tile_online(i, h, q_ref, k_ref, cp_ref, crow_ref, vaug_ref, o_ref, tq=tq)


def _fox_stage_v(v_ref, vaug_ref, *, tq, lp):
    def stage(rows, n):
        vaug_ref[rows, 0:HEAD_DIM] = v_ref[rows, :]
        vaug_ref[rows, HEAD_DIM:2 * HEAD_DIM] = jnp.ones((n, HEAD_DIM), BF16)

    def body(r, c):
        stage(pl.ds(pl.multiple_of(r * tq, 128), tq), tq)
        return c
    lax.fori_loop(0, lp // tq, body, 0)
    if lp % tq:
        stage(pl.ds(lp - lp % tq, lp % tq), lp % tq)


def _fox_scores(h, q16, k_ref, crow_ref):
    def c_k(krows):
        group = pl.multiple_of(SM_F + lax.shift_left(lax.shift_right_logical(h, 3), 3), 8)
        rows8 = crow_ref[pl.ds(group, F32_SUBLANES), krows]
        sub = lax.broadcasted_iota(jnp.int32, rows8.shape, 0)
        return jnp.sum(jnp.where(sub == (h & 7), rows8, 0.0), axis=0, keepdims=True)

    def scores(rows, bias_q, krows, n):
        s = _dot_nt(q16[rows], k_ref[krows, :])
        return s + pltpu.repeat(bias_q[rows], n // LANES, 1) - c_k(krows)
    return scores


def _fox_query_side(h, rows_q, q_ref, cp_ref):
    piece_lane = lax.broadcasted_iota(jnp.int32, (3 * LANES, LANES), 0) & (LANES - 1)
    pick = jnp.where(piece_lane == SM_F + h, 1.0, 0.0).astype(BF16)
    return q_ref[rows_q, :], _dot(cp_ref[rows_q, :], pick)


def _fox_finish(o_ref, acc, rows):
    o_ref[rows, :] = (acc[:, :HEAD_DIM] / acc[:, HEAD_DIM:]).astype(o_ref.dtype)


def _fox_tile(ii, h, bound, q_ref, k_ref, cp_ref, crow_ref, vaug_ref, o_ref, *, tq, key_chunk):
    q0 = CHUNK + ii * tq
    q16, c_q = _fox_query_side(h, slice(q0, q0 + tq), q_ref, cp_ref)
    scores = _fox_scores(h, q16, k_ref, crow_ref)
    lane = lax.broadcasted_iota(jnp.int32, (tq, LANES), 1)
    bias_q = c_q - bound

    def probs(rows, krows, n, mask=None):
        s = scores(rows, bias_q, krows, n)
        if mask is not None:
            s = jnp.where(mask, s, NEG_BIG)
        return jnp.exp2(s).astype(BF16)

    every = slice(0, tq)
    p = probs(every, slice(0, CHUNK), CHUNK, lane >= FRONT_PAD)
    acc = _dot(p, vaug_ref[0:CHUNK, :])
    yield
    for start in range(CHUNK, q0, key_chunk):
        kr = slice(start, start + key_chunk)
        acc = acc + _dot(probs(every, kr, key_chunk), vaug_ref[kr, :])
        yield
    for r in range(tq // FOX_DIAG_PIECE):
        rows = slice(r * FOX_DIAG_PIECE, (r + 1) * FOX_DIAG_PIECE)
        nk = (r + 1) * FOX_DIAG_PIECE
        keys = slice(q0, q0 + nk)
        qpos = lax.broadcasted_iota(jnp.int32, (FOX_DIAG_PIECE, nk), 0) + r * FOX_DIAG_PIECE
        kpos = lax.broadcasted_iota(jnp.int32, (FOX_DIAG_PIECE, nk), 1)
        p = probs(rows, keys, nk, kpos <= qpos)
        _fox_finish(o_ref, acc[rows] + _dot(p, vaug_ref[keys, :]), rows)
        yield


def _fox_tile_online(i, h, q_ref, k_ref, cp_ref, crow_ref, vaug_ref, o_ref, *, tq):
    q0 = pl.multiple_of(CHUNK + i * tq, 128)
    q16, c_q = _fox_query_side(h, pl.ds(q0, tq), q_ref, cp_ref)
    scores = _fox_scores(h, q16, k_ref, crow_ref)
    lane = lax.broadcasted_iota(jnp.int32, (tq, LANES), 1)
    qpos = q0 + lax.broadcasted_iota(jnp.int32, (tq, LANES), 0)

    def body(c, carry):
        m, acc = carry
        krows = pl.ds(pl.multiple_of(c * CHUNK, CHUNK), CHUNK)
        kpos = c * CHUNK + lane
        s = scores(slice(0, tq), c_q, krows, CHUNK)
        s = jnp.where(kpos >= FRONT_PAD, jnp.where(kpos <= qpos, s, NEG_BIG), NEG_BIG)
        m_new = jnp.maximum(m, jnp.max(s, axis=-1, keepdims=True))
        p = jnp.exp2(s - m_new)
        acc = jnp.exp2(m - m_new) * acc + _dot(p.astype(BF16), vaug_ref[krows, :])
        return m_new, acc

    init = (jnp.full((tq, 1), NEG_BIG, F32), jnp.zeros((tq, 2 * HEAD_DIM), F32))
    _, acc = lax.fori_loop(0, (q0 + tq) // CHUNK, body, init)
    _fox_finish(o_ref, acc, slice(0, tq))


def _fox(p1, cp, crow, bound, weights, wt, wt_rows, *, seq, tq):
    b, lp, _ = p1.shape
    nq = seq // tq
    nsteps = b * HEADS * nq
    colmap = lambda base: (lambda bi, hi, qi: (bi, 0, base + hi))
    step = lambda bi, hi, qi: (bi * HEADS + hi) * nq + qi

    cast_in, cast_out, cast_shapes = [], [], []
    for w in weights:
        _, r, c = w.shape
        rp = r // nsteps
        assert r % nsteps == 0 and rp % 16 == 0
        cast_in.append(pl.BlockSpec((None, rp, c), lambda bi, hi, qi: (0, step(bi, hi, qi), 0)))
        cast_out.append(pl.BlockSpec((rp, c), lambda bi, hi, qi: (step(bi, hi, qi), 0)))
        cast_shapes.append(jax.ShapeDtypeStruct((r, c), BF16))
    start, n = wt_rows
    rp = n // nsteps
    assert n % nsteps == 0 and rp % 16 == 0 and start % F32_SUBLANES == 0
    cast_in.append(_element_rows(rp, wt.shape[2], lambda bi, hi, qi: (
        start // F32_SUBLANES + step(bi, hi, qi) * (rp // F32_SUBLANES))))
    cast_out.append(pl.BlockSpec((rp, wt.shape[2]), lambda bi, hi, qi: (step(bi, hi, qi), 0)))
    cast_shapes.append(jax.ShapeDtypeStruct((n, wt.shape[2]), BF16))

    outs = pl.pallas_call(
        functools.partial(_fox_kernel, tq=tq, lp=lp, n_cast=len(cast_in)),
        grid=(b, HEADS, nq),
        in_specs=[
            pl.BlockSpec((None, lp, HEAD_DIM), colmap(COL_FOX_Q)),
            pl.BlockSpec((None, lp, HEAD_DIM), colmap(COL_FOX_K)),
            pl.BlockSpec((None, lp, HEAD_DIM), colmap(COL_FOX_V)),
            pl.BlockSpec((None, lp, 3 * LANES), lambda bi, hi, qi: (bi, 0, 0)),
            pl.BlockSpec((None, LANES, lp), lambda bi, hi, qi: (bi, 0, 0)),
            pl.BlockSpec(memory_space=pltpu.SMEM),
        ] + cast_in,
        out_specs=[pl.BlockSpec((None, tq, HEAD_DIM), lambda bi, hi, qi: (bi, qi, hi))] + cast_out,
        out_shape=[jax.ShapeDtypeStruct((b, seq, HEADS * HEAD_DIM), BF16)] + cast_shapes,
        scratch_shapes=[pltpu.VMEM((lp, 2 * HEAD_DIM), BF16)],
        compiler_params=_params("parallel", "parallel", "arbitrary"),
        name="fox",
    )(p1, p1, p1, cp, crow, bound, *weights, wt)
    return outs[0], outs[1:]


N_GDN_IN = 12
MIX_KEY_CHUNK = 512


def _mix_kernel(*refs, tq, lp, nq, n_cast):
    gdn_in = refs[:N_GDN_IN]
    fq_ref, fk_ref, fv_ref, cp_ref, crow_ref, bound_ref = refs[N_GDN_IN:N_GDN_IN + 6]
    first_out = N_GDN_IN + 6 + n_cast
    ya_ref, yb_ref = refs[first_out:first_out + 2]
    s_ref, prev_ref, vaug_ref = refs[-3:]
    c = pl.program_id(1)
    t = jnp.maximum(c - 1, 0)
    h = t // nq
    i = t - h * nq

    for w_ref, w16_ref in zip(refs[N_GDN_IN + 6:first_out], refs[first_out + 2:first_out + 2 + n_cast]):
        w16_ref[...] = w_ref[...].astype(w16_ref.dtype)

    def gdn_step():
        return _gdn_step(*gdn_in, ya_ref, s_ref, prev_ref, hg=HEADS)

    @pl.when(c == 0)
    def _():
        s_ref[...] = jnp.zeros_like(s_ref)
        prev_ref[...] = jnp.zeros_like(prev_ref)
        _run(gdn_step())

    bound = bound_ref[0, 0]
    safe = bound <= FOX_SAFE_BOUND
    for ii in range(nq):
        @pl.when(jnp.logical_and(safe, jnp.logical_and(c >= 1, i == ii)))
        def _(ii=ii):
            if ii == 0:
                _fox_stage_v(fv_ref, vaug_ref, tq=tq, lp=lp)
            _run(gdn_step(), _fox_tile(ii, h, bound, fq_ref, fk_ref, cp_ref, crow_ref, vaug_ref,
                                       yb_ref, tq=tq, key_chunk=MIX_KEY_CHUNK))

    @pl.when(jnp.logical_and(jnp.logical_not(safe), c >= 1))
    def _():
        @pl.when(i == 0)
        def _():
            _fox_stage_v(fv_ref, vaug_ref, tq=tq, lp=lp)
        _run(gdn_step())
        _fox_tile_online(i, h, fq_ref, fk_ref, cp_ref, crow_ref, vaug_ref, yb_ref, tq=tq)


def _mix(p1, yp, sel, oneh, conv_w, gnorm, shift3, cp, crow, bound, weights, wt, wt_rows, *, seq, tq):
    b, lp, _ = p1.shape
    nc = lp // CHUNK
    nq = seq // tq
    assert nc - 1 == HEADS * nq
    gw = HEADS * HEAD_DIM
    tile = lambda c: jnp.maximum(c - 1, 0)
    head = lambda c: tile(c) // nq
    col = lambda base: (lambda bi, c: (bi, c, base * LANES // gw))
    wcol = lambda base: (lambda bi, c: (0, base * LANES // gw))
    fcol = lambda base: (lambda bi, c: (bi, 0, base + head(c)))
    const = lambda bi, c: (0, 0)
    slab = lambda bi, c: bi * (nc - 1) + tile(c)
    nslabs = b * (nc - 1)

    cast_in, cast_out, cast_shapes = [], [], []
    for w in weights:
        _, r, cc = w.shape
        rp = r // nslabs
        assert r % nslabs == 0 and rp % 16 == 0
        cast_in.append(pl.BlockSpec((None, rp, cc), lambda bi, c: (0, slab(bi, c), 0)))
        cast_out.append(pl.BlockSpec((rp, cc), lambda bi, c: (slab(bi, c), 0)))
        cast_shapes.append(jax.ShapeDtypeStruct((r, cc), BF16))
    start, n = wt_rows
    rp = n // nslabs
    assert n % nslabs == 0 and rp % 16 == 0 and start % F32_SUBLANES == 0
    cast_in.append(_element_rows(rp, wt.shape[2], lambda bi, c: (
        start // F32_SUBLANES + slab(bi, c) * (rp // F32_SUBLANES))))
    cast_out.append(pl.BlockSpec((rp, wt.shape[2]), lambda bi, c: (slab(bi, c), 0)))
    cast_shapes.append(jax.ShapeDtypeStruct((n, wt.shape[2]), BF16))

    outs = pl.pallas_call(
        functools.partial(_mix_kernel, tq=tq, lp=lp, nq=nq, n_cast=len(cast_in)),
        grid=(b, nc),
        in_specs=[
            pl.BlockSpec((None, CHUNK, gw), col(COL_GDN_Q)),
            pl.BlockSpec((None, CHUNK, gw), col(COL_GDN_K)),
            pl.BlockSpec((None, CHUNK, gw), col(COL_GDN_V)),
            pl.BlockSpec((None, CHUNK, gw), col(COL_Z)),
            pl.BlockSpec((None, CHUNK, GQ_PIECES * LANES), lambda bi, c: (bi, c, 0)),
            pl.BlockSpec((GQ_PIECES * LANES, HEADS * N_GQ * LANES), const),
            pl.BlockSpec((HEADS * ROW_REP, GQ_PIECES * LANES), const),
            pl.BlockSpec((CONV_K, gw), wcol(COL_GDN_Q)),
            pl.BlockSpec((CONV_K, gw), wcol(COL_GDN_K)),
            pl.BlockSpec((CONV_K, gw), wcol(COL_GDN_V)),
            pl.BlockSpec((1, HEAD_DIM), const),
            pl.BlockSpec((3 * CHUNK, 2 * CHUNK), const),
            pl.BlockSpec((None, lp, HEAD_DIM), fcol(COL_FOX_Q)),
            pl.BlockSpec((None, lp, HEAD_DIM), fcol(COL_FOX_K)),
            pl.BlockSpec((None, lp, HEAD_DIM), fcol(COL_FOX_V)),
            pl.BlockSpec((None, lp, 3 * LANES), lambda bi, c: (bi, 0, 0)),
            pl.BlockSpec((None, LANES, lp), lambda bi, c: (bi, 0, 0)),
            pl.BlockSpec(memory_space=pltpu.SMEM),
        ] + cast_in,
        out_specs=[
            pl.BlockSpec((None, CHUNK, gw), lambda bi, c: (bi, tile(c), 0)),
            pl.BlockSpec((None, tq, HEAD_DIM), lambda bi, c: (bi, tile(c) - head(c) * nq, head(c))),
        ] + cast_out,
        out_shape=[jax.ShapeDtypeStruct((b, seq, gw), BF16)] * 2 + cast_shapes,
        scratch_shapes=[
            pltpu.VMEM((HEADS, HEAD_DIM, HEAD_DIM), F32),
            pltpu.VMEM((3, CHUNK, gw), BF16),
            pltpu.VMEM((lp, 2 * HEAD_DIM), BF16),
        ],
        compiler_params=_params("parallel", "arbitrary"),
        name="mix",
    )(p1, p1, p1, p1, yp, sel, oneh, conv_w, conv_w, conv_w, gnorm, shift3,
      p1, p1, p1, cp, crow, bound, *weights, wt)
    return outs[0], outs[1], outs[2:]


def _merge_kernel(x_ref, g_ref, ya_ref, yb_ref, wga_ref, wgb_ref, wa_ref, wb_ref, wo_ref,
                  o_ref, u_ref):
    j = pl.program_id(1)

    @pl.when(j == 0)
    def _():
        x = x_ref[...]
        u_ref[...] = _norm_rows(x, g_ref[...]).astype(BF16)
        o_ref[...] = x

    u = u_ref[...]
    ga = _dot_nt(u, wga_ref[...])
    gb = _dot_nt(u, wgb_ref[...])
    pa = _dot(ya_ref[...], wa_ref[...])
    pb = _dot(yb_ref[...], wb_ref[...])
    mix = _sigmoid(ga) * pa + _sigmoid(gb) * pb
    o_ref[...] += _dot(mix.astype(BF16), wo_ref[...])


def _merge(x2d, g, ya, yb, wg_t, wa, wb, wo, *, tm, tj):
    m, d = x2d.shape
    rowblk = lambda i, j: (i, 0)
    colblk = lambda i, j: (0, j)
    return pl.pallas_call(
        _merge_kernel,
        grid=(m // tm, d // tj),
        in_specs=[
            pl.BlockSpec((tm, d), rowblk),
            pl.BlockSpec((1, d), lambda i, j: (0, 0)),
            pl.BlockSpec((tm, d), rowblk),
            pl.BlockSpec((tm, d), rowblk),
            pl.BlockSpec((tj, d), lambda i, j: (j, 0)),
            pl.BlockSpec((tj, d), lambda i, j: (j + d // tj, 0)),
            pl.BlockSpec((d, tj), colblk),
            pl.BlockSpec((d, tj), colblk),
            pl.BlockSpec((tj, d), lambda i, j: (j, 0)),
        ],
        out_specs=pl.BlockSpec((tm, d), rowblk),
        out_shape=jax.ShapeDtypeStruct((m, d), F32),
        scratch_shapes=[pltpu.VMEM((tm, d), BF16)],
        compiler_params=_params("parallel", "arbitrary"),
        name="merge",
    )(x2d, g, ya, yb, wg_t, wg_t, wa, wb, wo)


def _mlp_kernel(h_ref, g_ref, wu_ref, wd_ref, gf_ref, o_ref, u_ref):
    f = pl.program_id(1)

    @pl.when(f == 0)
    def _():
        x = h_ref[...]
        u_ref[...] = _norm_rows(x, g_ref[...]).astype(BF16)
        o_ref[...] = x

    a = jnp.maximum(_dot(u_ref[...], wu_ref[...]), 0.0)
    o_ref[...] += _dot((a * a).astype(BF16), wd_ref[...])

    @pl.when(f == pl.num_programs(1) - 1)
    def _():
        o_ref[...] = _norm_rows(o_ref[...], gf_ref[...])


def _mlp(h2d, g, wu, wd, gf, *, tm, tf):
    m, d = h2d.shape
    dff = wu.shape[1]
    rowblk = lambda i, f: (i, 0)
    return pl.pallas_call(
        _mlp_kernel,
        grid=(m // tm, dff // tf),
        in_specs=[
            pl.BlockSpec((tm, d), rowblk),
            pl.BlockSpec((1, d), lambda i, f: (0, 0)),
            pl.BlockSpec((d, tf), lambda i, f: (0, f)),
            pl.BlockSpec((tf, d), lambda i, f: (f, 0)),
            pl.BlockSpec((1, d), lambda i, f: (0, 0)),
        ],
        out_specs=pl.BlockSpec((tm, d), rowblk),
        out_shape=jax.ShapeDtypeStruct((m, d), F32),
        scratch_shapes=[pltpu.VMEM((tm, d), BF16)],
        compiler_params=_params("parallel", "arbitrary"),
        name="mlp",
    )(h2d, g, wu, wd, gf)


def _constants():
    r = np.arange(128)
    ltri = (r[:, None] >= r[None, :]).astype(np.float32)
    ltri3 = np.concatenate([ltri, ltri, ltri], axis=1)

    shift3 = np.zeros((3 * CHUNK, 2 * CHUNK), np.float32)
    for s in range(1, CONV_K):
        shift3[(s - 1) * CHUNK + r, CHUNK + r - s] = 1.0

    src_lane = np.zeros((HEADS * N_GQ * LANES,), np.int64)
    for h in range(HEADS):
        for q in range(N_GQ):
            src_lane[(h * N_GQ + q) * LANES:(h * N_GQ + q + 1) * LANES] = 16 * q + h
    piece_lane = np.arange(GQ_PIECES * LANES) % LANES
    sel = (piece_lane[:, None] == src_lane[None, :]).astype(np.float32)
    oneh = (np.repeat(16 * GQ_G + np.arange(HEADS), ROW_REP)[:, None]
            == piece_lane[None, :]).astype(np.float32)

    as_bf = lambda a: jnp.asarray(a, BF16)
    return dict(ltri3=as_bf(ltri3), shift3=as_bf(shift3), sel=as_bf(sel), oneh=as_bf(oneh))


def _score_bound(gq, gk):
    gmax = jnp.max(jnp.abs(gq.astype(F32))) * jnp.max(jnp.abs(gk.astype(F32)))
    return (gmax * (HEAD_DIM ** 0.5 * LOG2E * 1.02)).astype(BF16).astype(F32).reshape(1, 1)


def _row128(vec, offset):
    return jnp.zeros((1, LANES), F32).at[0, offset:offset + HEADS].set(vec.astype(F32))


def _pick_rows_tile(total, target):
    best = 128
    for t in range(128, target + 1, 128):
        if total % t == 0:
            best = t
    return best


def _layer(x, meta_tokens, mix_norm_g, w_in, conv_w, a_log, dt_bias, gdn_norm_g, w_o_gdn,
           fox_q_norm_g, fox_k_norm_g, fox_f_bias, w_o_fox, w_out, mlp_norm_g, w_up, w_down,
           final_norm_g):
    b, seq, d = x.shape
    lp = CHUNK + seq
    qk = HEADS * HEAD_DIM

    o_z = 3 * qk
    o_b = o_z + qk
    o_a = o_b + HEADS
    o_fq = o_a + HEADS
    o_f = o_fq + 3 * qk
    o_ga = o_f + HEADS
    o_gb = o_ga + d

    wt = jnp.swapaxes(w_in, 1, 2)
    row = lambda v: v.reshape(1, -1).astype(F32)
    cst = _constants()

    tm_in = _pick_rows_tile(lp, 1408)
    hn = _prenorm(x, meta_tokens.astype(x.dtype), row(mix_norm_g), tile=tm_in)
    p1, ps = _inproj(hn.reshape(b * lp, d), wt, o_b, o_fq, o_f - o_fq,
                     ((o_b, o_fq - o_b), (o_f, o_ga - o_f)),
                     row(fox_q_norm_g), row(fox_k_norm_g), HEAD_DIM ** -0.5 * LOG2E,
                     tm=tm_in, tn=1024)
    p1 = p1.reshape(b, lp, P1_COLS)
    ps = ps.reshape(b, lp, LANES)

    yp, cp, crow = _gate_prep(ps, _row128(a_log, SM_A), _row128(dt_bias, SM_A),
                              _row128(fox_f_bias, SM_F), cst["ltri3"])
    bound = _score_bound(fox_q_norm_g, fox_k_norm_g)
    later_weights = [w_o_gdn, w_o_fox, w_out, w_up, w_down]
    gdn_args = (p1, yp, cst["sel"], cst["oneh"], conv_w.astype(F32), row(gdn_norm_g), cst["shift3"])
    tq = min(CHUNK * HEADS, seq)

    def mixers_separate():
        ya = _gdn(*gdn_args, seq=seq, hg=HEADS)
        yb, w16 = _fox(p1, cp, crow, bound, later_weights, wt, (o_ga, 2 * d), seq=seq, tq=tq)
        return ya, yb, tuple(w16)

    def mixers_fused():
        ya, yb, w16 = _mix(*gdn_args, cp, crow, bound, later_weights, wt, (o_ga, 2 * d),
                           seq=seq, tq=tq)
        return ya, yb, tuple(w16)

    fusable = lp // CHUNK - 1 == HEADS * (seq // tq)
    ya, yb, w16 = mixers_fused() if fusable else mixers_separate()
    wog, wof, wout, wup, wdown, wg_t = w16

    x2d = x.reshape(b * seq, d)
    h1 = _merge(x2d, row(mix_norm_g), ya.reshape(b * seq, qk), yb.reshape(b * seq, qk),
                wg_t, wog, wof, wout, tm=512, tj=512)
    out = _mlp(h1, row(mlp_norm_g), wup, wdown, row(final_norm_g), tm=512, tf=1024)
    return out.reshape(b, seq, d)


def kernel(x, meta_tokens, mix_norm_g, w_in, conv_w, a_log, dt_bias, gdn_norm_g, w_o_gdn,
           fox_q_norm_g, fox_k_norm_g, fox_f_bias, w_o_fox, w_out, mlp_norm_g, w_up, w_down,
           final_norm_g):
    assert w_in.shape[0] == 1, "single-layer block"
    return _layer(x, meta_tokens, mix_norm_g[0], w_in, conv_w[0], a_log[0], dt_bias[0],
                  gdn_norm_g[0], w_o_gdn, fox_q_norm_g[0], fox_k_norm_g[0], fox_f_bias[0],
                  w_o_fox, w_out, mlp_norm_g[0], w_up, w_down, final_norm_g)
```

```python
import functools
import math

import numpy as np
import jax
import jax.numpy as jnp
from jax import lax
from jax.experimental import pallas as pl
from jax.experimental.pallas import tpu as pltpu

F32 = jnp.float32
BF16 = jnp.bfloat16

NORM_EPS = 1e-6
N_META = 16
HEADS = 16
HEAD_DIM = 128
CONV_K = 4
LANES = 128
F32_SUBLANES = 8
CHUNK = 128
FRONT_PAD = CHUNK - N_META
NEG_BIG = -1e30
LOG2E = math.log2(math.e)
VMEM_LIMIT = 56 * 1024 * 1024

COL_GDN_Q, COL_GDN_K, COL_GDN_V, COL_Z = 0, 16, 32, 48
COL_FOX_Q, COL_FOX_K, COL_FOX_V = 64, 80, 96
P1_COLS = 112 * LANES
SM_B, SM_A, SM_F = 0, 16, 32
GQ_G, GQ_BETA = range(2)
N_GQ = 2
GQ_PIECES = 2
ROW_REP = 8
FOX_SAFE_BOUND = 40.0
FOX_KEY_CHUNK = 1024
FOX_DIAG_PIECE = 512


def _params(*sem):
    return pltpu.CompilerParams(dimension_semantics=sem, vmem_limit_bytes=VMEM_LIMIT)


def _norm_rows(x, g):
    ms = jnp.mean(x * x, axis=-1, keepdims=True)
    return x * lax.rsqrt(ms + NORM_EPS) * g


def _split3(x):
    hi = x.astype(BF16)
    r = x - hi.astype(F32)
    mid = r.astype(BF16)
    lo = (r - mid.astype(F32)).astype(BF16)
    return hi, mid, lo


def _dot(a, b):
    return jnp.dot(a, b, preferred_element_type=F32)


def _dot_nt(a, b):
    return lax.dot_general(a, b, (((1,), (1,)), ((), ())), preferred_element_type=F32)


def _dot_tn(a, b):
    return lax.dot_general(a, b, (((0,), (0,)), ((), ())), preferred_element_type=F32)


def _sigmoid(x):
    return 1.0 / (1.0 + jnp.exp(-x))


def _silu(x):
    return x * _sigmoid(x)


def _softplus(x):
    return jnp.maximum(x, 0.0) + jnp.log1p(jnp.exp(-jnp.abs(x)))


def _prenorm_kernel(x_ref, meta_ref, g_ref, o_ref, *, tile):
    t = pl.program_id(1)

    def norm_chunks(first_out, n, shift):
        def body(c, carry):
            dst = pl.ds(pl.multiple_of(first_out + c * CHUNK, CHUNK), CHUNK)
            src = pl.ds(pl.multiple_of(first_out - shift + c * CHUNK, CHUNK), CHUNK)
            o_ref[dst, :] = _norm_rows(x_ref[src, :], g_ref[...]).astype(o_ref.dtype)
            return carry
        lax.fori_loop(0, n, body, 0)

    @pl.when(t == 0)
    def _():
        o_ref[0:FRONT_PAD, :] = jnp.zeros((FRONT_PAD, o_ref.shape[1]), o_ref.dtype)
        o_ref[FRONT_PAD:CHUNK, :] = _norm_rows(meta_ref[...], g_ref[...]).astype(o_ref.dtype)
        norm_chunks(CHUNK, tile // CHUNK - 1, CHUNK)

    @pl.when(t > 0)
    def _():
        norm_chunks(0, tile // CHUNK, 0)


def _prenorm(x, meta_tokens, g, *, tile):
    b, seq, d = x.shape
    lp = CHUNK + seq
    sub = F32_SUBLANES
    assert lp % tile == 0 and tile % CHUNK == 0 and CHUNK % sub == 0
    x_rows = lambda bi, t: sub * jnp.maximum(t * (tile // sub) - CHUNK // sub, 0)
    return pl.pallas_call(
        functools.partial(_prenorm_kernel, tile=tile),
        grid=(b, lp // tile),
        in_specs=[
            pl.BlockSpec((None, pl.Element(tile), pl.Element(d)),
                         lambda bi, t: (bi, x_rows(bi, t), 0)),
            pl.BlockSpec((N_META, d), lambda bi, t: (0, 0)),
            pl.BlockSpec((1, d), lambda bi, t: (0, 0)),
        ],
        out_specs=pl.BlockSpec((None, tile, d), lambda bi, t: (bi, t, 0)),
        out_shape=jax.ShapeDtypeStruct((b, lp, d), BF16),
        compiler_params=_params("parallel", "arbitrary"),
        name="prenorm",
    )(x, meta_tokens, g)


def _inproj_kernel(x_ref, w_ref, wsa_ref, wsb_ref, gq_ref, gk_ref, o_ref, os_ref, *,
                   q_tiles, k_tiles, q_scale):
    j = pl.program_id(1)

    @pl.when(j == 0)
    def _():
        pad = LANES - wsa_ref.shape[0] - wsb_ref.shape[0]
        ws = jnp.concatenate([wsa_ref[...], wsb_ref[...], jnp.zeros((pad, wsa_ref.shape[1]), F32)],
                             axis=0)
        os_ref[...] = _dot_nt(x_ref[...], ws.astype(BF16))

    def project():
        return _dot_nt(x_ref[...], w_ref[...].astype(BF16))

    def project_head_norm(g_ref, scale):
        acc = project()
        for h in range(o_ref.shape[1] // HEAD_DIM):
            cols = slice(h * HEAD_DIM, (h + 1) * HEAD_DIM)
            o_ref[:, cols] = (_norm_rows(acc[:, cols], g_ref[...]) * scale).astype(o_ref.dtype)

    is_q = jnp.logical_and(j >= q_tiles[0], j < q_tiles[1])
    is_k = jnp.logical_and(j >= k_tiles[0], j < k_tiles[1])

    @pl.when(is_q)
    def _():
        project_head_norm(gq_ref, q_scale)

    @pl.when(is_k)
    def _():
        project_head_norm(gk_ref, 1.0)

    @pl.when(jnp.logical_not(jnp.logical_or(is_q, is_k)))
    def _():
        o_ref[...] = project().astype(o_ref.dtype)


def _element_rows(tile, d, start_tiles):
    return pl.BlockSpec((None, pl.Element(tile), pl.Element(d)),
                        lambda *ids: (0, F32_SUBLANES * start_tiles(*ids), 0))


def _inproj(hn, wt, n_a, b_start, n_b, small, gq, gk, q_scale, *, tm, tn):
    m, d = hn.shape
    na, nb = n_a // tn, n_b // tn
    assert nb % 3 == 0
    q_tiles = (na, na + nb // 3)
    k_tiles = (na + nb // 3, na + 2 * nb // 3)
    (s0, c0), (s1, c1) = small
    sub = F32_SUBLANES
    assert all(v % sub == 0 for v in (tn, b_start - n_a, s0, s1))

    def w_rows(i, j):
        return j * (tn // sub) + jnp.where(j >= na, (b_start - n_a) // sub, 0)

    return pl.pallas_call(
        functools.partial(_inproj_kernel, q_tiles=q_tiles, k_tiles=k_tiles, q_scale=q_scale),
        grid=(m // tm, na + nb),
        in_specs=[
            pl.BlockSpec((tm, d), lambda i, j: (i, 0)),
            _element_rows(tn, d, w_rows),
            _element_rows(c0, d, lambda i, j: s0 // sub),
            _element_rows(c1, d, lambda i, j: s1 // sub),
            pl.BlockSpec((1, HEAD_DIM), lambda i, j: (0, 0)),
            pl.BlockSpec((1, HEAD_DIM), lambda i, j: (0, 0)),
        ],
        out_specs=[
            pl.BlockSpec((tm, tn), lambda i, j: (i, j)),
            pl.BlockSpec((tm, LANES), lambda i, j: (i, 0)),
        ],
        out_shape=[
            jax.ShapeDtypeStruct((m, (na + nb) * tn), BF16),
            jax.ShapeDtypeStruct((m, LANES), F32),
        ],
        compiler_params=_params("parallel", "arbitrary"),
        name="inproj",
    )(hn, wt, wt, wt, gq, gk)


def _gate_prep_kernel(ps_ref, alog_ref, dtb_ref, fb_ref, ltri_ref, yp_ref, cp_ref, crow_ref, *, nblk):
    lane = lax.broadcasted_iota(jnp.int32, (CHUNK, LANES), 1)
    rowi = lax.broadcasted_iota(jnp.int32, (CHUNK, LANES), 0)

    def body(i, carry):
        rows = pl.ds(pl.multiple_of(i * CHUNK, CHUNK), CHUNK)
        x = ps_ref[rows, :]
        valid = (i * CHUNK + rowi) >= FRONT_PAD

        beta = jnp.where(valid, _sigmoid(x), 0.0)
        gstep = jnp.where(valid, -jnp.exp(alog_ref[...]) * _softplus(x + dtb_ref[...]), 0.0)
        gcum = _dot(ltri_ref[...], jnp.concatenate(_split3(gstep), axis=0))
        y = jnp.where(lane < 16 * (GQ_G + 1), pltpu.roll(gcum, LANES + 16 * GQ_G - SM_A, 1),
                      pltpu.roll(beta, 16 * GQ_BETA - SM_B, 1))
        yp_ref[rows, :] = jnp.concatenate(_split3(y)[:GQ_PIECES], axis=1)

        xf = x + fb_ref[...]
        ls = (jnp.minimum(xf, 0.0) - jnp.log1p(jnp.exp(-jnp.abs(xf)))) * LOG2E
        cum = _dot(ltri_ref[...], jnp.concatenate(_split3(ls), axis=0)) + carry
        cp_ref[rows, :] = jnp.concatenate(_split3(cum), axis=1)
        crow_ref[:, rows] = cum.T
        return cum[CHUNK - 1:CHUNK, :]

    lax.fori_loop(0, nblk, body, jnp.zeros((1, LANES), F32))


def _gate_prep(ps, alog_row, dtb_row, fb_row, ltri3):
    b, lp, _ = ps.shape
    const = lambda bi: (0, 0)
    return pl.pallas_call(
        functools.partial(_gate_prep_kernel, nblk=lp // CHUNK),
        grid=(b,),
        in_specs=[
            pl.BlockSpec((None, lp, LANES), lambda bi: (bi, 0, 0)),
            pl.BlockSpec((1, LANES), const),
            pl.BlockSpec((1, LANES), const),
            pl.BlockSpec((1, LANES), const),
            pl.BlockSpec((CHUNK, 3 * CHUNK), const),
        ],
        out_specs=[
            pl.BlockSpec((None, lp, GQ_PIECES * LANES), lambda bi: (bi, 0, 0)),
            pl.BlockSpec((None, lp, 3 * LANES), lambda bi: (bi, 0, 0)),
            pl.BlockSpec((None, LANES, lp), lambda bi: (bi, 0, 0)),
        ],
        out_shape=[
            jax.ShapeDtypeStruct((b, lp, GQ_PIECES * LANES), BF16),
            jax.ShapeDtypeStruct((b, lp, 3 * LANES), BF16),
            jax.ShapeDtypeStruct((b, LANES, lp), F32),
        ],
        compiler_params=_params("parallel"),
        name="gate_prep",
    )(ps, alog_row, dtb_row, fb_row, ltri3)


def _gdn_kernel(*refs, hg):
    s_ref, prev_ref = refs[-2:]

    @pl.when(pl.program_id(2) == 0)
    def _():
        s_ref[...] = jnp.zeros_like(s_ref)
        prev_ref[...] = jnp.zeros_like(prev_ref)

    _run(_gdn_step(*refs, hg=hg))


def _run(*stage_generators):
    live = list(stage_generators)
    while live:
        for g in list(live):
            if next(g, StopIteration) is StopIteration:
                live.remove(g)


def _gdn_step(q_ref, k_ref, v_ref, z_ref, yp_ref, sel_ref, oneh_ref, wq_ref, wk_ref, wv_ref,
              gn_ref, shift_ref, o_ref, s_ref, prev_ref, *, hg):
    row = lax.broadcasted_iota(jnp.int32, (CHUNK, CHUNK), 0)
    col = lax.broadcasted_iota(jnp.int32, (CHUNK, CHUNK), 1)
    tril = row >= col
    strict = row > col
    heads = range(hg)
    hsl = lambda a, u: a[:, u * HEAD_DIM:(u + 1) * HEAD_DIM]

    yp = yp_ref[...]
    gb = _dot(yp, sel_ref[...])
    g_rows = _dot_nt(oneh_ref[...], yp)
    gq = lambda u, q: gb[:, (u * N_GQ + q) * LANES:(u * N_GQ + q + 1) * LANES]
    g_col = [gq(u, GQ_G) for u in heads]
    beta = [gq(u, GQ_BETA) for u in heads]
    eg = [jnp.exp(g_col[u]) for u in heads]
    yield

    def conv_act(x_ref, w_ref, t):
        x16 = x_ref[...]
        x2 = jnp.concatenate([prev_ref[t], x16], axis=0)
        prev_ref[t] = x16
        sh = _dot(shift_ref[...], x2)
        w = w_ref[...]
        acc = x16.astype(F32) * w[CONV_K - 1:CONV_K, :]
        for s in range(1, CONV_K):
            acc = acc + sh[(s - 1) * CHUNK:s * CHUNK, :] * w[CONV_K - 1 - s:CONV_K - s, :]
        return _silu(acc)

    qa = conv_act(q_ref, wq_ref, 0)
    yield
    ka = conv_act(k_ref, wk_ref, 1)
    yield
    va = conv_act(v_ref, wv_ref, 2)
    yield

    def l2n(x):
        return x * lax.rsqrt(jnp.sum(x * x, axis=-1, keepdims=True) + NORM_EPS)

    qn = [l2n(hsl(qa, u)) * (HEAD_DIM ** -0.5) for u in heads]
    kn = [l2n(hsl(ka, u)) for u in heads]
    k16 = [kn[u].astype(BF16) for u in heads]
    yield

    state = [s_ref[u] for u in heads]
    kb = [kn[u] * beta[u] for u in heads]
    r1 = [_dot(jnp.concatenate([kb[u] * eg[u], qn[u] * eg[u]], axis=0).astype(BF16),
               state[u].astype(BF16)) for u in heads]
    yield

    kq = [_dot_nt(jnp.concatenate([kb[u], qn[u]], axis=0).astype(BF16), k16[u])
          for u in heads]
    ediff = [jnp.exp(jnp.minimum(g_col[u] - g_rows[u * ROW_REP:u * ROW_REP + 1, :], 0.0))
             for u in heads]
    a = [kq[u][:CHUNK] * jnp.where(strict, ediff[u], 0.0) for u in heads]
    qk_m = [(kq[u][CHUNK:] * jnp.where(tril, ediff[u], 0.0)).astype(BF16) for u in heads]
    yield

    a16 = [a[u].astype(BF16) for u in heads]
    yk = [_dot(a16[u], a16[u]) for u in heads]
    sk = [-a[u] for u in heads]
    nlev = int(np.log2(CHUNK)) - 1
    for lev in range(nlev):
        yield
        yk16 = [yk[u].astype(BF16) for u in heads]
        if lev < nlev - 1:
            zz = [_dot(jnp.concatenate([sk[u], yk[u]], axis=0).astype(BF16), yk16[u]) for u in heads]
            sk = [sk[u] + yk[u] + zz[u][:CHUNK] for u in heads]
            yk = [zz[u][CHUNK:] for u in heads]
        else:
            sk = [sk[u] + yk[u] + _dot(sk[u].astype(BF16), yk16[u]) for u in heads]

    yield
    rhs = [hsl(va, u) * beta[u] - r1[u][:CHUNK] for u in heads]
    v_new = [(rhs[u] + _dot(sk[u].astype(BF16), rhs[u].astype(BF16))).astype(BF16) for u in heads]
    yield
    o = [r1[u][CHUNK:] + _dot(qk_m[u], v_new[u]) for u in heads]
    for u in heads:
        k_dec = kn[u] * jnp.exp(g_col[u][CHUNK - 1:CHUNK, :] - g_col[u])
        s_ref[u] = state[u] * eg[u][CHUNK - 1:CHUNK, :] + _dot_tn(k_dec.astype(BF16), v_new[u])
    yield

    on = jnp.concatenate([_norm_rows(o[u], gn_ref[...]) for u in heads], axis=1)
    o_ref[...] = (on * _silu(z_ref[...].astype(F32))).astype(o_ref.dtype)


def _gdn(p1, yp, sel, oneh, conv_w, gnorm, shift3, *, seq, hg):
    b, lp, _ = p1.shape
    nc = lp // CHUNK
    gw = hg * HEAD_DIM
    ng = HEADS // hg
    col = lambda base: (lambda bi, gi, ci: (bi, ci, base * LANES // gw + gi))
    wcol = lambda base: (lambda bi, gi, ci: (0, base * LANES // gw + gi))
    const = lambda bi, gi, ci: (0, 0)
    return pl.pallas_call(
        functools.partial(_gdn_kernel, hg=hg),
        grid=(b, ng, nc),
        in_specs=[
            pl.BlockSpec((None, CHUNK, gw), col(COL_GDN_Q)),
            pl.BlockSpec((None, CHUNK, gw), col(COL_GDN_K)),
            pl.BlockSpec((None, CHUNK, gw), col(COL_GDN_V)),
            pl.BlockSpec((None, CHUNK, gw), col(COL_Z)),
            pl.BlockSpec((None, CHUNK, GQ_PIECES * LANES), lambda bi, gi, ci: (bi, ci, 0)),
            pl.BlockSpec((GQ_PIECES * LANES, hg * N_GQ * LANES), lambda bi, gi, ci: (0, gi)),
            pl.BlockSpec((hg * ROW_REP, GQ_PIECES * LANES), lambda bi, gi, ci: (gi, 0)),
            pl.BlockSpec((CONV_K, gw), wcol(COL_GDN_Q)),
            pl.BlockSpec((CONV_K, gw), wcol(COL_GDN_K)),
            pl.BlockSpec((CONV_K, gw), wcol(COL_GDN_V)),
            pl.BlockSpec((1, HEAD_DIM), const),
            pl.BlockSpec((3 * CHUNK, 2 * CHUNK), const),
        ],
        out_specs=pl.BlockSpec((None, CHUNK, gw),
                               lambda bi, gi, ci: (bi, jnp.maximum(ci - 1, 0), gi)),
        out_shape=jax.ShapeDtypeStruct((b, seq, HEADS * HEAD_DIM), BF16),
        scratch_shapes=[
            pltpu.VMEM((hg, HEAD_DIM, HEAD_DIM), F32),
            pltpu.VMEM((3, CHUNK, gw), BF16),
        ],
        compiler_params=_params("parallel", "parallel", "arbitrary"),
        name="gdn",
    )(p1, p1, p1, p1, yp, sel, oneh, conv_w, conv_w, conv_w, gnorm, shift3)


def _fox_kernel(*refs, tq, lp, n_cast):
    q_ref, k_ref, v_ref, cp_ref, crow_ref, bound_ref = refs[:6]
    o_ref = refs[6 + n_cast]
    vaug_ref = refs[-1]
    h = pl.program_id(1)
    i = pl.program_id(2)

    for w_ref, w16_ref in zip(refs[6:6 + n_cast], refs[7 + n_cast:7 + 2 * n_cast]):
        w16_ref[...] = w_ref[...].astype(w16_ref.dtype)

    @pl.when(i == 0)
    def _():
        _fox_stage_v(v_ref, vaug_ref, tq=tq, lp=lp)

    bound = bound_ref[0, 0]

    @pl.when(bound <= FOX_SAFE_BOUND)
    def _():
        for ii in range(pl.cdiv(lp - CHUNK, tq)):
            @pl.when(i == ii)
            def _(ii=ii):
                _run(_fox_tile(ii, h, bound, q_ref, k_ref, cp_ref, crow_ref, vaug_ref, o_ref,
                               tq=tq, key_chunk=FOX_KEY_CHUNK))

    @pl.when(bound > FOX_SAFE_BOUND)
    def _():
        _fox_tile_online(i, h, q_ref, k_ref, cp_ref, crow_ref, vaug_ref, o_ref, tq=tq)


def _fox_stage_v(v_ref, vaug_ref, *, tq, lp):
    def stage(rows, n):
        vaug_ref[rows, 0:HEAD_DIM] = v_ref[rows, :]
        vaug_ref[rows, HEAD_DIM:2 * HEAD_DIM] = jnp.ones((n, HEAD_DIM), BF16)

    def body(r, c):
        stage(pl.ds(pl.multiple_of(r * tq, 128), tq), tq)
        return c
    lax.fori_loop(0, lp // tq, body, 0)
    if lp % tq:
        stage(pl.ds(lp - lp % tq, lp % tq), lp % tq)


def _fox_scores(h, q16, k_ref, crow_ref):
    def c_k(krows):
        group = pl.multiple_of(SM_F + lax.shift_left(lax.shift_right_logical(h, 3), 3), 8)
        rows8 = crow_ref[pl.ds(group, F32_SUBLANES), krows]
        sub = lax.broadcasted_iota(jnp.int32, rows8.shape, 0)
        return jnp.sum(jnp.where(sub == (h & 7), rows8, 0.0), axis=0, keepdims=True)

    def scores(rows, bias_q, krows, n):
        s = _dot_nt(q16[rows], k_ref[krows, :])
        return s + pltpu.repeat(bias_q[rows], n // LANES, 1) - c_k(krows)
    return scores


def _fox_query_side(h, rows_q, q_ref, cp_ref):
    piece_lane = lax.broadcasted_iota(jnp.int32, (3 * LANES, LANES), 0) & (LANES - 1)
    pick = jnp.where(piece_lane == SM_F + h, 1.0, 0.0).astype(BF16)
    return q_ref[rows_q, :], _dot(cp_ref[rows_q, :], pick)


def _fox_finish(o_ref, acc, rows):
    o_ref[rows, :] = (acc[:, :HEAD_DIM] / acc[:, HEAD_DIM:]).astype(o_ref.dtype)


def _fox_tile(ii, h, bound, q_ref, k_ref, cp_ref, crow_ref, vaug_ref, o_ref, *, tq, key_chunk):
    q0 = CHUNK + ii * tq
    q16, c_q = _fox_query_side(h, slice(q0, q0 + tq), q_ref, cp_ref)
    scores = _fox_scores(h, q16, k_ref, crow_ref)
    lane = lax.broadcasted_iota(jnp.int32, (tq, LANES), 1)
    bias_q = c_q - bound

    def probs(rows, krows, n, mask=None):
        s = scores(rows, bias_q, krows, n)
        if mask is not None:
            s = jnp.where(mask, s, NEG_BIG)
        return jnp.exp2(s).astype(BF16)

    every = slice(0, tq)
    p = probs(every, slice(0, CHUNK), CHUNK, lane >= FRONT_PAD)
    acc = _dot(p, vaug_ref[0:CHUNK, :])
    yield
    for start in range(CHUNK, q0, key_chunk):
        kr = slice(start, start + key_chunk)
        acc = acc + _dot(probs(every, kr, key_chunk), vaug_ref[kr, :])
        yield
    for r in range(tq // FOX_DIAG_PIECE):
        rows = slice(r * FOX_DIAG_PIECE, (r + 1) * FOX_DIAG_PIECE)
        nk = (r + 1) * FOX_DIAG_PIECE
        keys = slice(q0, q0 + nk)
        qpos = lax.broadcasted_iota(jnp.int32, (FOX_DIAG_PIECE, nk), 0) + r * FOX_DIAG_PIECE
        kpos = lax.broadcasted_iota(jnp.int32, (FOX_DIAG_PIECE, nk), 1)
        p = probs(rows, keys, nk, kpos <= qpos)
        _fox_finish(o_ref, acc[rows] + _dot(p, vaug_ref[keys, :]), rows)
        yield


def _fox_tile_online(i, h, q_ref, k_ref, cp_ref, crow_ref, vaug_ref, o_ref, *, tq):
    rows_n = FOX_DIAG_PIECE
    lane = lax.broadcasted_iota(jnp.int32, (rows_n, LANES), 1)

    def piece(r, carry):
        q0 = pl.multiple_of(CHUNK + i * tq + r * rows_n, 128)
        q16, c_q = _fox_query_side(h, pl.ds(q0, rows_n), q_ref, cp_ref)
        scores = _fox_scores(h, q16, k_ref, crow_ref)
        qpos = q0 + lax.broadcasted_iota(jnp.int32, (rows_n, LANES), 0)

        def body(c, carry):
            m, acc = carry
            krows = pl.ds(pl.multiple_of(c * CHUNK, CHUNK), CHUNK)
            kpos = c * CHUNK + lane
            s = scores(slice(0, rows_n), c_q, krows, CHUNK)
            s = jnp.where(kpos >= FRONT_PAD, jnp.where(kpos <= qpos, s, NEG_BIG), NEG_BIG)
            m_new = jnp.maximum(m, jnp.max(s, axis=-1, keepdims=True))
            p = jnp.exp2(s - m_new)
            acc = jnp.exp2(m - m_new) * acc + _dot(p.astype(BF16), vaug_ref[krows, :])
            return m_new, acc

        init = (jnp.full((rows_n, 1), NEG_BIG, F32), jnp.zeros((rows_n, 2 * HEAD_DIM), F32))
        _, acc = lax.fori_loop(0, (q0 + rows_n) // CHUNK, body, init)
        _fox_finish(o_ref, acc, pl.ds(pl.multiple_of(r * rows_n, 128), rows_n))
        return carry

    lax.fori_loop(0, tq // rows_n, piece, 0)


def _fox(p1, cp, crow, bound, weights, wt, wt_rows, *, seq, tq):
    b, lp, _ = p1.shape
    nq = seq // tq
    nsteps = b * HEADS * nq
    colmap = lambda base: (lambda bi, hi, qi: (bi, 0, base + hi))
    step = lambda bi, hi, qi: (bi * HEADS + hi) * nq + qi

    cast_in, cast_out, cast_shapes = [], [], []
    for w in weights:
        _, r, c = w.shape
        rp = r // nsteps
        assert r % nsteps == 0 and rp % 16 == 0
        cast_in.append(pl.BlockSpec((None, rp, c), lambda bi, hi, qi: (0, step(bi, hi, qi), 0)))
        cast_out.append(pl.BlockSpec((rp, c), lambda bi, hi, qi: (step(bi, hi, qi), 0)))
        cast_shapes.append(jax.ShapeDtypeStruct((r, c), BF16))
    start, n = wt_rows
    rp = n // nsteps
    assert n % nsteps == 0 and rp % 16 == 0 and start % F32_SUBLANES == 0
    cast_in.append(_element_rows(rp, wt.shape[2], lambda bi, hi, qi: (
        start // F32_SUBLANES + step(bi, hi, qi) * (rp // F32_SUBLANES))))
    cast_out.append(pl.BlockSpec((rp, wt.shape[2]), lambda bi, hi, qi: (step(bi, hi, qi), 0)))
    cast_shapes.append(jax.ShapeDtypeStruct((n, wt.shape[2]), BF16))

    outs = pl.pallas_call(
        functools.partial(_fox_kernel, tq=tq, lp=lp, n_cast=len(cast_in)),
        grid=(b, HEADS, nq),
        in_specs=[
            pl.BlockSpec((None, lp, HEAD_DIM), colmap(COL_FOX_Q)),
            pl.BlockSpec((None, lp, HEAD_DIM), colmap(COL_FOX_K)),
            pl.BlockSpec((None, lp, HEAD_DIM), colmap(COL_FOX_V)),
            pl.BlockSpec((None, lp, 3 * LANES), lambda bi, hi, qi: (bi, 0, 0)),
            pl.BlockSpec((None, LANES, lp), lambda bi, hi, qi: (bi, 0, 0)),
            pl.BlockSpec(memory_space=pltpu.SMEM),
        ] + cast_in,
        out_specs=[pl.BlockSpec((None, tq, HEAD_DIM), lambda bi, hi, qi: (bi, qi, hi))] + cast_out,
        out_shape=[jax.ShapeDtypeStruct((b, seq, HEADS * HEAD_DIM), BF16)] + cast_shapes,
        scratch_shapes=[pltpu.VMEM((lp, 2 * HEAD_DIM), BF16)],
        compiler_params=_params("parallel", "parallel", "arbitrary"),
        name="fox",
    )(p1, p1, p1, cp, crow, bound, *weights, wt)
    return outs[0], outs[1:]


N_GDN_IN = 12
MIX_KEY_CHUNK = 512


def _mix_kernel(*refs, tq, lp, nq, n_cast):
    gdn_in = refs[:N_GDN_IN]
    fq_ref, fk_ref, fv_ref, cp_ref, crow_ref, bound_ref = refs[N_GDN_IN:N_GDN_IN + 6]
    first_out = N_GDN_IN + 6 + n_cast
    ya_ref, yb_ref = refs[first_out:first_out + 2]
    s_ref, prev_ref, vaug_ref = refs[-3:]
    c = pl.program_id(1)
    t = jnp.maximum(c - 1, 0)
    h = t // nq
    i = t - h * nq

    for w_ref, w16_ref in zip(refs[N_GDN_IN + 6:first_out], refs[first_out + 2:first_out + 2 + n_cast]):
        w16_ref[...] = w_ref[...].astype(w16_ref.dtype)

    def gdn_step():
        return _gdn_step(*gdn_in, ya_ref, s_ref, prev_ref, hg=HEADS)

    @pl.when(c == 0)
    def _():
        s_ref[...] = jnp.zeros_like(s_ref)
        prev_ref[...] = jnp.zeros_like(prev_ref)
        _run(gdn_step())

    bound = bound_ref[0, 0]
    for ii in range(nq):
        @pl.when(jnp.logical_and(c >= 1, i == ii))
        def _(ii=ii):
            if ii == 0:
                _fox_stage_v(fv_ref, vaug_ref, tq=tq, lp=lp)
            _run(gdn_step(), _fox_tile(ii, h, bound, fq_ref, fk_ref, cp_ref, crow_ref, vaug_ref,
                                       yb_ref, tq=tq, key_chunk=MIX_KEY_CHUNK))

    @pl.when(jnp.logical_and(c >= 1, bound > FOX_SAFE_BOUND))
    def _():
        _fox_tile_online(i, h, fq_ref, fk_ref, cp_ref, crow_ref, vaug_ref, yb_ref, tq=tq)


def _mix(p1, yp, sel, oneh, conv_w, gnorm, shift3, cp, crow, bound, weights, wt, wt_rows, *, seq, tq):
    b, lp, _ = p1.shape
    nc = lp // CHUNK
    nq = seq // tq
    assert nc - 1 == HEADS * nq
    gw = HEADS * HEAD_DIM
    tile = lambda c: jnp.maximum(c - 1, 0)
    head = lambda c: tile(c) // nq
    col = lambda base: (lambda bi, c: (bi, c, base * LANES // gw))
    wcol = lambda base: (lambda bi, c: (0, base * LANES // gw))
    fcol = lambda base: (lambda bi, c: (bi, 0, base + head(c)))
    const = lambda bi, c: (0, 0)
    slab = lambda bi, c: bi * (nc - 1) + tile(c)
    nslabs = b * (nc - 1)

    cast_in, cast_out, cast_shapes = [], [], []
    for w in weights:
        _, r, cc = w.shape
        rp = r // nslabs
        assert r % nslabs == 0 and rp % 16 == 0
        cast_in.append(pl.BlockSpec((None, rp, cc), lambda bi, c: (0, slab(bi, c), 0)))
        cast_out.append(pl.BlockSpec((rp, cc), lambda bi, c: (slab(bi, c), 0)))
        cast_shapes.append(jax.ShapeDtypeStruct((r, cc), BF16))
    start, n = wt_rows
    rp = n // nslabs
    assert n % nslabs == 0 and rp % 16 == 0 and start % F32_SUBLANES == 0
    cast_in.append(_element_rows(rp, wt.shape[2], lambda bi, c: (
        start // F32_SUBLANES + slab(bi, c) * (rp // F32_SUBLANES))))
    cast_out.append(pl.BlockSpec((rp, wt.shape[2]), lambda bi, c: (slab(bi, c), 0)))
    cast_shapes.append(jax.ShapeDtypeStruct((n, wt.shape[2]), BF16))

    outs = pl.pallas_call(
        functools.partial(_mix_kernel, tq=tq, lp=lp, nq=nq, n_cast=len(cast_in)),
        grid=(b, nc),
        in_specs=[
            pl.BlockSpec((None, CHUNK, gw), col(COL_GDN_Q)),
            pl.BlockSpec((None, CHUNK, gw), col(COL_GDN_K)),
            pl.BlockSpec((None, CHUNK, gw), col(COL_GDN_V)),
            pl.BlockSpec((None, CHUNK, gw), col(COL_Z)),
            pl.BlockSpec((None, CHUNK, GQ_PIECES * LANES), lambda bi, c: (bi, c, 0)),
            pl.BlockSpec((GQ_PIECES * LANES, HEADS * N_GQ * LANES), const),
            pl.BlockSpec((HEADS * ROW_REP, GQ_PIECES * LANES), const),
            pl.BlockSpec((CONV_K, gw), wcol(COL_GDN_Q)),
            pl.BlockSpec((CONV_K, gw), wcol(COL_GDN_K)),
            pl.BlockSpec((CONV_K, gw), wcol(COL_GDN_V)),
            pl.BlockSpec((1, HEAD_DIM), const),
            pl.BlockSpec((3 * CHUNK, 2 * CHUNK), const),
            pl.BlockSpec((None, lp, HEAD_DIM), fcol(COL_FOX_Q)),
            pl.BlockSpec((None, lp, HEAD_DIM), fcol(COL_FOX_K)),
            pl.BlockSpec((None, lp, HEAD_DIM), fcol(COL_FOX_V)),
            pl.BlockSpec((None, lp, 3 * LANES), lambda bi, c: (bi, 0, 0)),
            pl.BlockSpec((None, LANES, lp), lambda bi, c: (bi, 0, 0)),
            pl.BlockSpec(memory_space=pltpu.SMEM),
        ] + cast_in,
        out_specs=[
            pl.BlockSpec((None, CHUNK, gw), lambda bi, c: (bi, tile(c), 0)),
            pl.BlockSpec((None, tq, HEAD_DIM), lambda bi, c: (bi, tile(c) - head(c) * nq, head(c))),
        ] + cast_out,
        out_shape=[jax.ShapeDtypeStruct((b, seq, gw), BF16)] * 2 + cast_shapes,
        scratch_shapes=[
            pltpu.VMEM((HEADS, HEAD_DIM, HEAD_DIM), F32),
            pltpu.VMEM((3, CHUNK, gw), BF16),
            pltpu.VMEM((lp, 2 * HEAD_DIM), BF16),
        ],
        compiler_params=_params("parallel", "arbitrary"),
        name="mix",
    )(p1, p1, p1, p1, yp, sel, oneh, conv_w, conv_w, conv_w, gnorm, shift3,
      p1, p1, p1, cp, crow, bound, *weights, wt)
    return outs[0], outs[1], outs[2:]


def _merge_kernel(x_ref, g_ref, ya_ref, yb_ref, wga_ref, wgb_ref, wa_ref, wb_ref, wo_ref,
                  o_ref, u_ref):
    j = pl.program_id(1)

    @pl.when(j == 0)
    def _():
        x = x_ref[...]
        u_ref[...] = _norm_rows(x, g_ref[...]).astype(BF16)
        o_ref[...] = x

    u = u_ref[...]
    ga = _dot_nt(u, wga_ref[...])
    gb = _dot_nt(u, wgb_ref[...])
    pa = _dot(ya_ref[...], wa_ref[...])
    pb = _dot(yb_ref[...], wb_ref[...])
    mix = _sigmoid(ga) * pa + _sigmoid(gb) * pb
    o_ref[...] += _dot(mix.astype(BF16), wo_ref[...])


def _merge(x2d, g, ya, yb, wg_t, wa, wb, wo, *, tm, tj):
    m, d = x2d.shape
    rowblk = lambda i, j: (i, 0)
    colblk = lambda i, j: (0, j)
    return pl.pallas_call(
        _merge_kernel,
        grid=(m // tm, d // tj),
        in_specs=[
            pl.BlockSpec((tm, d), rowblk),
            pl.BlockSpec((1, d), lambda i, j: (0, 0)),
            pl.BlockSpec((tm, d), rowblk),
            pl.BlockSpec((tm, d), rowblk),
            pl.BlockSpec((tj, d), lambda i, j: (j, 0)),
            pl.BlockSpec((tj, d), lambda i, j: (j + d // tj, 0)),
            pl.BlockSpec((d, tj), colblk),
            pl.BlockSpec((d, tj), colblk),
            pl.BlockSpec((tj, d), lambda i, j: (j, 0)),
        ],
        out_specs=pl.BlockSpec((tm, d), rowblk),
        out_shape=jax.ShapeDtypeStruct((m, d), F32),
        scratch_shapes=[pltpu.VMEM((tm, d), BF16)],
        compiler_params=_params("parallel", "arbitrary"),
        name="merge",
    )(x2d, g, ya, yb, wg_t, wg_t, wa, wb, wo)


def _mlp_kernel(h_ref, g_ref, wu_ref, wd_ref, gf_ref, o_ref, u_ref):
    f = pl.program_id(1)

    @pl.when(f == 0)
    def _():
        x = h_ref[...]
        u_ref[...] = _norm_rows(x, g_ref[...]).astype(BF16)
        o_ref[...] = x

    a = jnp.maximum(_dot(u_ref[...], wu_ref[...]), 0.0)
    o_ref[...] += _dot((a * a).astype(BF16), wd_ref[...])

    @pl.when(f == pl.num_programs(1) - 1)
    def _():
        o_ref[...] = _norm_rows(o_ref[...], gf_ref[...])


def _mlp(h2d, g, wu, wd, gf, *, tm, tf):
    m, d = h2d.shape
    dff = wu.shape[1]
    rowblk = lambda i, f: (i, 0)
    return pl.pallas_call(
        _mlp_kernel,
        grid=(m // tm, dff // tf),
        in_specs=[
            pl.BlockSpec((tm, d), rowblk),
            pl.BlockSpec((1, d), lambda i, f: (0, 0)),
            pl.BlockSpec((d, tf), lambda i, f: (0, f)),
            pl.BlockSpec((tf, d), lambda i, f: (f, 0)),
            pl.BlockSpec((1, d), lambda i, f: (0, 0)),
        ],
        out_specs=pl.BlockSpec((tm, d), rowblk),
        out_shape=jax.ShapeDtypeStruct((m, d), F32),
        scratch_shapes=[pltpu.VMEM((tm, d), BF16)],
        compiler_params=_params("parallel", "arbitrary"),
        name="mlp",
    )(h2d, g, wu, wd, gf)


def _constants():
    r = np.arange(128)
    ltri = (r[:, None] >= r[None, :]).astype(np.float32)
    ltri3 = np.concatenate([ltri, ltri, ltri], axis=1)

    shift3 = np.zeros((3 * CHUNK, 2 * CHUNK), np.float32)
    for s in range(1, CONV_K):
        shift3[(s - 1) * CHUNK + r, CHUNK + r - s] = 1.0

    src_lane = np.zeros((HEADS * N_GQ * LANES,), np.int64)
    for h in range(HEADS):
        for q in range(N_GQ):
            src_lane[(h * N_GQ + q) * LANES:(h * N_GQ + q + 1) * LANES] = 16 * q + h
    piece_lane = np.arange(GQ_PIECES * LANES) % LANES
    sel = (piece_lane[:, None] == src_lane[None, :]).astype(np.float32)
    oneh = (np.repeat(16 * GQ_G + np.arange(HEADS), ROW_REP)[:, None]
            == piece_lane[None, :]).astype(np.float32)

    as_bf = lambda a: jnp.asarray(a, BF16)
    return dict(ltri3=as_bf(ltri3), shift3=as_bf(shift3), sel=as_bf(sel), oneh=as_bf(oneh))


def _score_bound(gq, gk):
    gmax = jnp.max(jnp.abs(gq.astype(F32))) * jnp.max(jnp.abs(gk.astype(F32)))
    return (gmax * (HEAD_DIM ** 0.5 * LOG2E * 1.02)).astype(BF16).astype(F32).reshape(1, 1)


def _row128(vec, offset):
    return jnp.zeros((1, LANES), F32).at[0, offset:offset + HEADS].set(vec.astype(F32))


def _pick_rows_tile(total, target):
    best = 128
    for t in range(128, target + 1, 128):
        if total % t == 0:
            best = t
    return best


def _layer(x, meta_tokens, mix_norm_g, w_in, conv_w, a_log, dt_bias, gdn_norm_g, w_o_gdn,
           fox_q_norm_g, fox_k_norm_g, fox_f_bias, w_o_fox, w_out, mlp_norm_g, w_up, w_down,
           final_norm_g):
    b, seq, d = x.shape
    lp = CHUNK + seq
    qk = HEADS * HEAD_DIM

    o_z = 3 * qk
    o_b = o_z + qk
    o_a = o_b + HEADS
    o_fq = o_a + HEADS
    o_f = o_fq + 3 * qk
    o_ga = o_f + HEADS
    o_gb = o_ga + d

    wt = jnp.swapaxes(w_in, 1, 2)
    row = lambda v: v.reshape(1, -1).astype(F32)
    cst = _constants()

    tm_in = _pick_rows_tile(lp, 1408)
    hn = _prenorm(x, meta_tokens.astype(x.dtype), row(mix_norm_g), tile=tm_in)
    p1, ps = _inproj(hn.reshape(b * lp, d), wt, o_b, o_fq, o_f - o_fq,
                     ((o_b, o_fq - o_b), (o_f, o_ga - o_f)),
                     row(fox_q_norm_g), row(fox_k_norm_g), HEAD_DIM ** -0.5 * LOG2E,
                     tm=tm_in, tn=1024)
    p1 = p1.reshape(b, lp, P1_COLS)
    ps = ps.reshape(b, lp, LANES)

    yp, cp, crow = _gate_prep(ps, _row128(a_log, SM_A), _row128(dt_bias, SM_A),
                              _row128(fox_f_bias, SM_F), cst["ltri3"])
    bound = _score_bound(fox_q_norm_g, fox_k_norm_g)
    later_weights = [w_o_gdn, w_o_fox, w_out, w_up, w_down]
    gdn_args = (p1, yp, cst["sel"], cst["oneh"], conv_w.astype(F32), row(gdn_norm_g), cst["shift3"])
    tq = min(CHUNK * HEADS, seq)

    def mixers_separate():
        ya = _gdn(*gdn_args, seq=seq, hg=HEADS)
        yb, w16 = _fox(p1, cp, crow, bound, later_weights, wt, (o_ga, 2 * d), seq=seq, tq=tq)
        return ya, yb, tuple(w16)

    def mixers_fused():
        ya, yb, w16 = _mix(*gdn_args, cp, crow, bound, later_weights, wt, (o_ga, 2 * d),
                           seq=seq, tq=tq)
        return ya, yb, tuple(w16)

    fusable = lp // CHUNK - 1 == HEADS * (seq // tq)
    ya, yb, w16 = mixers_fused() if fusable else mixers_separate()
    wog, wof, wout, wup, wdown, wg_t = w16

    x2d = x.reshape(b * seq, d)
    h1 = _merge(x2d, row(mix_norm_g), ya.reshape(b * seq, qk), yb.reshape(b * seq, qk),
                wg_t, wog, wof, wout, tm=512, tj=512)
    out = _mlp(h1, row(mlp_norm_g), wup, wdown, row(final_norm_g), tm=512, tf=1024)
    return out.reshape(b, seq, d)


def kernel(x, meta_tokens, mix_norm_g, w_in, conv_w, a_log, dt_bias, gdn_norm_g, w_o_gdn,
           fox_q_norm_g, fox_k_norm_g, fox_f_bias, w_o_fox, w_out, mlp_norm_g, w_up, w_down,
           final_norm_g):
    assert w_in.shape[0] == 1, "single-layer block"
    return _layer(x, meta_tokens, mix_norm_g[0], w_in, conv_w[0], a_log[0], dt_bias[0],
                  gdn_norm_g[0], w_o_gdn, fox_q_norm_g[0], fox_k_norm_g[0], fox_f_bias[0],
                  w_o_fox, w_out, mlp_norm_g[0], w_up, w_down, final_norm_g)
```
